```python
import jax, jax.numpy as jnp
from jax import lax
import numpy as np


D_MODEL = 4096
BATCH = 4
SEQ = 2048
DEPTH = 1

GM_WIDTH = 2048
GM_GROUPS = 8
GM_GROUP_W = GM_WIDTH // GM_GROUPS
CHUNK = 128
MLA_HEADS = 32
QK_NOPE = 128
QK_ROPE = 64
QK_HEAD = QK_NOPE + QK_ROPE
V_HEAD = 128
Q_LORA = 1024
KV_LORA = 512
ROPE_THETA = 10000.0
Q_BLOCK = 128
D_FF = 4 * D_MODEL
N_BRANCH = 2
N_MOD = 6
EPS = 1e-6
OFF_Q = 2 * GM_WIDTH
OFF_KV = OFF_Q + Q_LORA
OFF_KPE = OFF_KV + KV_LORA
OFF_GATE = OFF_KPE + QK_ROPE
IN_COLS = OFF_GATE + N_BRANCH * D_MODEL

kernel_name = 'hybrid_gmlp_mla_block'


def rms_norm(x, g):
    xf = x.astype(jnp.float32)
    y = xf * lax.rsqrt(jnp.mean(xf * xf, axis=-1, keepdims=True) + EPS)
    return (y * g.astype(jnp.float32)).astype(x.dtype)


def modulate(h, shift, scale):
    return h * (1 + scale[:, None, :]) + shift[:, None, :]


def rope_tables(positions, dtype):
    inv_freq = 1.0 / (ROPE_THETA ** (jnp.arange(0, QK_ROPE, 2, dtype=jnp.float32) / QK_ROPE))
    ang = positions.astype(jnp.float32)[..., None] * inv_freq
    return (jnp.cos(ang)[:, :, None, :].astype(dtype),
            jnp.sin(ang)[:, :, None, :].astype(dtype))


def apply_rope(x, cos, sin):
    x1, x2 = jnp.split(x, 2, axis=-1)
    return jnp.concatenate([x1 * cos - x2 * sin, x2 * cos + x1 * sin], axis=-1)


def gmlp_branch(uv, g_v, w_s, b_s):
    z = jax.nn.gelu(uv, approximate=False)
    u, v = z[..., :GM_WIDTH], z[..., GM_WIDTH:]
    v = rms_norm(v, g_v)
    b, s, _ = v.shape
    v = v.reshape(b, s // CHUNK, CHUNK, GM_GROUPS, GM_GROUP_W)
    causal = jnp.tril(jnp.ones((CHUNK, CHUNK), dtype=bool))
    w = jnp.where(causal[None], w_s, 0.0)
    mixed = jnp.einsum('gts,bnsgc->bntgc', w, v) + b_s.T[None, None, :, :, None]
    return u * mixed.reshape(b, s, GM_WIDTH)


def mla_branch(q_lat, kv_lat, k_pe, cos, sin, g_q_lat, g_kv_lat, w_uq, w_ukv, g_qn, g_kn):
    b, s, _ = q_lat.shape
    q = (rms_norm(q_lat, g_q_lat) @ w_uq).reshape(b, s, MLA_HEADS, QK_HEAD)
    kv = (rms_norm(kv_lat, g_kv_lat) @ w_ukv).reshape(b, s, MLA_HEADS, QK_NOPE + V_HEAD)
    k_nope, v = kv[..., :QK_NOPE], kv[..., QK_NOPE:]
    k = jnp.concatenate([k_nope, jnp.broadcast_to(k_pe[:, :, None, :], (b, s, MLA_HEADS, QK_ROPE))], axis=-1)
    q = rms_norm(q, g_qn)
    k = rms_norm(k, g_kn)
    q = jnp.concatenate([q[..., :QK_NOPE], apply_rope(q[..., QK_NOPE:], cos, sin)], axis=-1)
    k = jnp.concatenate([k[..., :QK_NOPE], apply_rope(k[..., QK_NOPE:], cos, sin)], axis=-1)
    q = q.transpose(0, 2, 1, 3)
    k = k.transpose(0, 2, 1, 3)
    v = v.transpose(0, 2, 1, 3)
    nb = s // Q_BLOCK
    q_blocks = q.reshape(b, MLA_HEADS, nb, Q_BLOCK, QK_HEAD).transpose(2, 0, 1, 3, 4)
    key_idx = jnp.arange(s)
    scale = QK_HEAD ** -0.5

    def attend(args):
        q_blk, i = args
        q_idx = i * Q_BLOCK + jnp.arange(Q_BLOCK)
        scores = jnp.einsum('bhqd,bhkd->bhqk', q_blk, k).astype(jnp.float32) * scale
        scores = jnp.where(key_idx[None, :] <= q_idx[:, None], scores, -jnp.inf)
        p = jax.nn.softmax(scores, axis=-1).astype(v.dtype)
        return jnp.einsum('bhqk,bhkd->bhqd', p, v)

    out = lax.map(attend, (q_blocks, jnp.arange(nb)))
    return out.transpose(1, 0, 3, 2, 4).reshape(b, s, MLA_HEADS * V_HEAD)


def setup_inputs(seed: int = 0) -> dict:
    key = jax.random.key(seed)
    ks = jax.random.split(key, 24)
    f32 = jnp.float32

    def nrm(k, shape, fan_in):
        return jax.random.normal(k, shape, f32) * (fan_in ** -0.5)

    def gain(k, shape):
        return 1.0 + 0.02 * jax.random.normal(k, shape, f32)

    L = DEPTH
    x = jax.random.normal(ks[0], (BATCH, SEQ, D_MODEL), f32)
    c = jax.random.normal(ks[1], (BATCH, D_MODEL), f32)
    positions = (jnp.arange(SEQ, dtype=jnp.int32)[None, :]
                 + jax.random.randint(ks[2], (BATCH, 1), 0, 4096, dtype=jnp.int32))
    return {
        'x': x,
        'c': c,
        'positions': positions,
        'w_ada': nrm(ks[3], (L, D_MODEL, N_MOD * D_MODEL), D_MODEL),
        'b_ada': 0.02 * jax.random.normal(ks[4], (L, N_MOD * D_MODEL), f32),
        'g_norm1': gain(ks[5], (L, D_MODEL)),
        'w_in': nrm(ks[6], (L, D_MODEL, IN_COLS), D_MODEL),
        'g_v': gain(ks[7], (L, GM_WIDTH)),
        'w_s': nrm(ks[8], (L, GM_GROUPS, CHUNK, CHUNK), CHUNK),
        'b_s': 1.0 + 0.1 * jax.random.normal(ks[9], (L, GM_GROUPS, CHUNK), f32),
        'g_q_lat': gain(ks[10], (L, Q_LORA)),
        'g_kv_lat': gain(ks[11], (L, KV_LORA)),
        'w_uq': nrm(ks[12], (L, Q_LORA, MLA_HEADS * QK_HEAD), Q_LORA),
        'w_ukv': nrm(ks[13], (L, KV_LORA, MLA_HEADS * (QK_NOPE + V_HEAD)), KV_LORA),
        'g_qn': gain(ks[14], (L, QK_HEAD)),
        'g_kn': gain(ks[15], (L, QK_HEAD)),
        'w_branch_a': nrm(ks[16], (L, GM_WIDTH, D_MODEL), GM_WIDTH),
        'w_branch_b': nrm(ks[17], (L, MLA_HEADS * V_HEAD, D_MODEL), MLA_HEADS * V_HEAD),
        'w_out': nrm(ks[18], (L, D_MODEL, D_MODEL), D_MODEL),
        'g_norm2': gain(ks[19], (L, D_MODEL)),
        'w_ff1': nrm(ks[20], (L, D_MODEL, D_FF), D_MODEL),
        'w_ff2': nrm(ks[21], (L, D_FF, D_MODEL), D_FF),
    }


def reference(x, c, positions, w_ada, b_ada, g_norm1, w_in, g_v, w_s, b_s, g_q_lat, g_kv_lat,
              w_uq, w_ukv, g_qn, g_kn, w_branch_a, w_branch_b, w_out, g_norm2, w_ff1, w_ff2):
    cos, sin = rope_tables(positions, x.dtype)
    cond = jax.nn.silu(c)
    for l in range(DEPTH):
        mod = cond @ w_ada[l] + b_ada[l]
        sh1, sc1, ga1, sh2, sc2, ga2 = jnp.split(mod, N_MOD, axis=-1)
        h = modulate(rms_norm(x, g_norm1[l]), sh1, sc1)
        proj = h @ w_in[l]
        y_a = gmlp_branch(proj[..., :OFF_Q], g_v[l], w_s[l], b_s[l]) @ w_branch_a[l]
        y_b = mla_branch(proj[..., OFF_Q:OFF_KV], proj[..., OFF_KV:OFF_KPE], proj[..., OFF_KPE:OFF_GATE],
                         cos, sin, g_q_lat[l], g_kv_lat[l], w_uq[l], w_ukv[l], g_qn[l], g_kn[l]) @ w_branch_b[l]
        gate_a = jax.nn.sigmoid(proj[..., OFF_GATE:OFF_GATE + D_MODEL])
        gate_b = jax.nn.sigmoid(proj[..., OFF_GATE + D_MODEL:])
        mixed = gate_a * y_a + gate_b * y_b
        x = x + ga1[:, None, :] * (mixed @ w_out[l])
        h = modulate(rms_norm(x, g_norm2[l]), sh2, sc2)
        ff = jnp.square(jax.nn.relu(h @ w_ff1[l])) @ w_ff2[l]
        x = x + ga2[:, None, :] * ff
    return x
```

```python
import functools
import math

import jax
import jax.numpy as jnp
from jax import lax
from jax.experimental import pallas as pl
from jax.experimental.pallas import tpu as pltpu

D_MODEL = 4096
SEQ = 2048
GM_WIDTH = 2048
GM_GROUPS = 8
GM_GROUP_W = GM_WIDTH // GM_GROUPS
CHUNK = 128
MLA_HEADS = 32
QK_NOPE = 128
QK_ROPE = 64
QK_HEAD = QK_NOPE + QK_ROPE
V_HEAD = 128
Q_LORA = 1024
KV_LORA = 512
ROPE_THETA = 10000.0
D_FF = 4 * D_MODEL
N_MOD = 6
EPS = 1e-6
OFF_Q = 2 * GM_WIDTH
OFF_KV = OFF_Q + Q_LORA
OFF_KPE = OFF_KV + KV_LORA
OFF_GATE = OFF_KPE + QK_ROPE

LANES = 128
QK_PAD = 2 * LANES
LAT_COLS = Q_LORA + KV_LORA + LANES
VMEM_LIMIT = 56 * 1024 * 1024

F32 = jnp.float32
BF16 = jnp.bfloat16


def _params(*sem):
    return pltpu.CompilerParams(dimension_semantics=sem, vmem_limit_bytes=VMEM_LIMIT)


def _ada_kernel(c_ref, w_ref, b_ref, o_ref):
    c = c_ref[...]
    cond = c * jax.nn.sigmoid(c)
    acc = jnp.dot(cond.astype(BF16), w_ref[...].astype(BF16), preferred_element_type=F32)
    o_ref[...] = acc + b_ref[...]


def _ada(c8, w_ada, b_ada, tn=1024):
    n = w_ada.shape[1]
    return pl.pallas_call(
        _ada_kernel,
        out_shape=jax.ShapeDtypeStruct((8, n), F32),
        grid=(n // tn,),
        in_specs=[
            pl.BlockSpec((8, D_MODEL), lambda j: (0, 0)),
            pl.BlockSpec((D_MODEL, tn), lambda j: (0, j)),
            pl.BlockSpec((1, tn), lambda j: (0, j)),
        ],
        out_specs=pl.BlockSpec((8, tn), lambda j: (0, j)),
        compiler_params=_params("parallel"),
        name="ada",
    )(c8, w_ada, b_ada)


def _normmod_kernel(x_ref, g_ref, sh_ref, sc_ref, o_ref):
    x = x_ref[...]
    y = x * lax.rsqrt(jnp.mean(x * x, axis=-1, keepdims=True) + EPS)
    y = y * g_ref[...]
    o_ref[...] = (y * (1.0 + sc_ref[0]) + sh_ref[0]).astype(o_ref.dtype)


def _mod_spec(idx, tm, width, ncol_arg=False):
    per_batch = SEQ // tm
    if ncol_arg:
        return pl.BlockSpec((1, 1, width), lambda m, n, *_: ((m // per_batch) * N_MOD + idx, 0, n))
    return pl.BlockSpec((1, 1, width), lambda m, *_: ((m // per_batch) * N_MOD + idx, 0, 0))


def _normmod(x2, g, mod3, shift_idx, scale_idx, tm=512):
    t, d = x2.shape
    return pl.pallas_call(
        _normmod_kernel,
        out_shape=jax.ShapeDtypeStruct((t, d), BF16),
        grid=(t // tm,),
        in_specs=[
            pl.BlockSpec((tm, d), lambda m: (m, 0)),
            pl.BlockSpec((1, d), lambda m: (0, 0)),
            _mod_spec(shift_idx, tm, d),
            _mod_spec(scale_idx, tm, d),
        ],
        out_specs=pl.BlockSpec((tm, d), lambda m: (m, 0)),
        compiler_params=_params("parallel"),
        name="normmod",
    )(x2, g, mod3, mod3)


def _gelu(v):
    return 0.5 * v * (1.0 + lax.erf(v * math.sqrt(0.5)))


def _relu2(v):
    r = jnp.maximum(v, 0.0)
    return r * r


_ACTS = {"gelu": _gelu, "sigmoid": jax.nn.sigmoid, "relu2": _relu2, "none": lambda v: v}


def _mm_act_kernel(x_ref, w_ref, o_ref, *, act):
    acc = jnp.dot(x_ref[...], w_ref[...], preferred_element_type=F32)
    o_ref[...] = _ACTS[act](acc).astype(o_ref.dtype)


def _mm_act(x, w, act, out_dtype, tm, tn, name):
    t, k = x.shape
    n = w.shape[1]
    return pl.pallas_call(
        functools.partial(_mm_act_kernel, act=act),
        out_shape=jax.ShapeDtypeStruct((t, n), out_dtype),
        grid=(t // tm, n // tn),
        in_specs=[
            pl.BlockSpec((tm, k), lambda m, j: (m, 0)),
            pl.BlockSpec((k, tn), lambda m, j: (0, j)),
        ],
        out_specs=pl.BlockSpec((tm, tn), lambda m, j: (m, j)),
        compiler_params=_params("parallel", "parallel"),
        name=name,
    )(x, w)


def _gmlp_kernel(u_ref, v_ref, gv_ref, ws_ref, bias_ref, o_ref, *, n_chunks):
    row = lax.broadcasted_iota(jnp.int32, (CHUNK, CHUNK), 0)
    col = lax.broadcasted_iota(jnp.int32, (CHUNK, CHUNK), 1)
    causal = col <= row
    gv = gv_ref[...]
    for ci in range(n_chunks):
        rows = pl.ds(ci * CHUNK, CHUNK)
        v = v_ref[rows, :].astype(F32)
        vn = v * lax.rsqrt(jnp.mean(v * v, axis=-1, keepdims=True) + EPS) * gv
        vn = vn.astype(BF16)
        for g in range(GM_GROUPS):
            cols = slice(g * GM_GROUP_W, (g + 1) * GM_GROUP_W)
            w = jnp.where(causal, ws_ref[g], 0.0).astype(BF16)
            mixed = jnp.dot(w, vn[:, cols], preferred_element_type=F32) + bias_ref[:, cols]
            o_ref[rows, cols] = (u_ref[rows, cols].astype(F32) * mixed).astype(o_ref.dtype)


def _gmlp(z, g_v, w_s, bias_full, tm=512):
    t = z.shape[0]
    return pl.pallas_call(
        functools.partial(_gmlp_kernel, n_chunks=tm // CHUNK),
        out_shape=jax.ShapeDtypeStruct((t, GM_WIDTH), BF16),
        grid=(t // tm,),
        in_specs=[
            pl.BlockSpec((tm, GM_WIDTH), lambda m: (m, 0)),
            pl.BlockSpec((tm, GM_WIDTH), lambda m: (m, 1)),
            pl.BlockSpec((1, GM_WIDTH), lambda m: (0, 0)),
            pl.BlockSpec((GM_GROUPS, CHUNK, CHUNK), lambda m: (0, 0, 0)),
            pl.BlockSpec((CHUNK, GM_WIDTH), lambda m: (0, 0)),
        ],
        out_specs=pl.BlockSpec((tm, GM_WIDTH), lambda m: (m, 0)),
        compiler_params=_params("parallel"),
        name="gmlp",
    )(z, z, g_v, w_s, bias_full)


def _rope_tab_kernel(pos_ref, freq_ref, c_ref, s1_ref, s2_ref):
    ang = pos_ref[...].astype(F32) * freq_ref[...]
    lane = lax.broadcasted_iota(jnp.int32, ang.shape, 1)
    cos = jnp.cos(ang)
    sin = jnp.sin(ang)
    half = QK_ROPE // 2
    c_ref[...] = jnp.where(lane < QK_ROPE, cos, 0.0)
    s1_ref[...] = jnp.where((lane >= half) & (lane < QK_ROPE), sin, 0.0)
    s2_ref[...] = jnp.where(lane < half, -sin, 0.0)


def _rope_tables(pos_col, freq_row, tm=1024):
    t = pos_col.shape[0]
    shp = jax.ShapeDtypeStruct((t, LANES), F32)
    return pl.pallas_call(
        _rope_tab_kernel,
        out_shape=(shp, shp, shp),
        grid=(t // tm,),
        in_specs=[
            pl.BlockSpec((tm, 1), lambda m: (m, 0)),
            pl.BlockSpec((1, LANES), lambda m: (0, 0)),
        ],
        out_specs=tuple(pl.BlockSpec((tm, LANES), lambda m: (m, 0)) for _ in range(3)),
        compiler_params=_params("parallel"),
        name="rope_tables",
    )(pos_col, freq_row)


def _rope(v, c, s1, s2):
    return v * c + pltpu.roll(v, QK_ROPE // 2, 1) * s1 + pltpu.roll(v, LANES - QK_ROPE // 2, 1) * s2


def _rms_rows(x, g):
    return x * lax.rsqrt(jnp.mean(x * x, axis=-1, keepdims=True) + EPS) * g


def _qprep_kernel(lat_ref, glat_ref, w_ref, g1_ref, g2_ref, c_ref, s1_ref, s2_ref, o_ref, *, heads):
    xn = _rms_rows(lat_ref[...], glat_ref[...]).astype(BF16)
    acc = jnp.dot(xn, w_ref[...], preferred_element_type=F32)
    c, s1, s2 = c_ref[...], s1_ref[...], s2_ref[...]
    g1, g2 = g1_ref[...], g2_ref[...]
    scale = QK_HEAD ** -0.5
    for h in range(heads):
        a = acc[:, h * QK_PAD:h * QK_PAD + LANES]
        b = acc[:, h * QK_PAD + LANES:(h + 1) * QK_PAD]
        ssq = jnp.sum(a * a + b * b, axis=-1, keepdims=True)
        r = lax.rsqrt(ssq * (1.0 / QK_HEAD) + EPS) * scale
        o_ref[:, h * QK_PAD:h * QK_PAD + LANES] = (a * r * g1).astype(o_ref.dtype)
        o_ref[:, h * QK_PAD + LANES:(h + 1) * QK_PAD] = _rope(b * r * g2, c, s1, s2).astype(o_ref.dtype)


def _qprep(lat, g_q_lat, wq, g1, g2, c, s1, s2, tm=512, heads=4):
    t = lat.shape[0]
    tn = heads * QK_PAD
    row = lambda m, j: (m, 0)
    const = lambda m, j: (0, 0)
    return pl.pallas_call(
        functools.partial(_qprep_kernel, heads=heads),
        out_shape=jax.ShapeDtypeStruct((t, MLA_HEADS * QK_PAD), BF16),
        grid=(t // tm, MLA_HEADS // heads),
        in_specs=[
            pl.BlockSpec((tm, Q_LORA), row),
            pl.BlockSpec((1, Q_LORA), const),
            pl.BlockSpec((Q_LORA, tn), lambda m, j: (0, j)),
            pl.BlockSpec((1, LANES), const),
            pl.BlockSpec((1, LANES), const),
            pl.BlockSpec((tm, LANES), row),
            pl.BlockSpec((tm, LANES), row),
            pl.BlockSpec((tm, LANES), row),
        ],
        out_specs=pl.BlockSpec((tm, tn), lambda m, j: (m, j)),
        compiler_params=_params("parallel", "parallel"),
        name="q_prep",
    )(lat, g_q_lat, wq, g1, g2, c, s1, s2)


def _kvprep_kernel(lat_ref, kpe_ref, glat_ref, w_ref, g1_ref, g2_ref, c_ref, s1_ref, s2_ref,
                   k_ref, v_ref, *, heads):
    xn = _rms_rows(lat_ref[...], glat_ref[...]).astype(BF16)
    acc = jnp.dot(xn, w_ref[...], preferred_element_type=F32)
    kpe = kpe_ref[...]
    ssq_pe = jnp.sum(kpe * kpe, axis=-1, keepdims=True)
    kr = _rope(kpe * g2_ref[...], c_ref[...], s1_ref[...], s2_ref[...])
    g1 = g1_ref[...]
    width = QK_NOPE + V_HEAD
    for h in range(heads):
        a = acc[:, h * width:h * width + QK_NOPE]
        ssq = jnp.sum(a * a, axis=-1, keepdims=True) + ssq_pe
        r = lax.rsqrt(ssq * (1.0 / QK_HEAD) + EPS)
        k_ref[:, h * QK_PAD:h * QK_PAD + LANES] = (a * r * g1).astype(k_ref.dtype)
        k_ref[:, h * QK_PAD + LANES:(h + 1) * QK_PAD] = (kr * r).astype(k_ref.dtype)
        v_ref[:, h * V_HEAD:(h + 1) * V_HEAD] = acc[:, h * width + QK_NOPE:(h + 1) * width].astype(v_ref.dtype)


def _kvprep(lat, g_kv_lat, wkv, g1, g2, c, s1, s2, tm=512, heads=4):
    t = lat.shape[0]
    row = lambda m, j: (m, 0)
    const = lambda m, j: (0, 0)
    return pl.pallas_call(
        functools.partial(_kvprep_kernel, heads=heads),
        out_shape=(jax.ShapeDtypeStruct((t, MLA_HEADS * QK_PAD), BF16),
                   jax.ShapeDtypeStruct((t, MLA_HEADS * V_HEAD), BF16)),
        grid=(t // tm, MLA_HEADS // heads),
        in_specs=[
            pl.BlockSpec((tm, KV_LORA), lambda m, j: (m, Q_LORA // KV_LORA)),
            pl.BlockSpec((tm, LANES), lambda m, j: (m, (Q_LORA + KV_LORA) // LANES)),
            pl.BlockSpec((1, KV_LORA), const),
            pl.BlockSpec((KV_LORA, heads * (QK_NOPE + V_HEAD)), lambda m, j: (0, j)),
            pl.BlockSpec((1, LANES), const),
            pl.BlockSpec((1, LANES), const),
            pl.BlockSpec((tm, LANES), row),
            pl.BlockSpec((tm, LANES), row),
            pl.BlockSpec((tm, LANES), row),
        ],
        out_specs=(pl.BlockSpec((tm, heads * QK_PAD), lambda m, j: (m, j)),
                   pl.BlockSpec((tm, heads * V_HEAD), lambda m, j: (m, j))),
        compiler_params=_params("parallel", "parallel"),
        name="kv_prep",
    )(lat, lat, g_kv_lat, wkv, g1, g2, c, s1, s2)


def _attn_kernel(q_ref, k_ref, v_ref, o_ref, *, tq):
    i = pl.program_id(2)
    q = q_ref[...]

    def scores(j):
        k = k_ref[pl.ds(pl.multiple_of(j * tq, tq), tq), :]
        return lax.dot_general(q, k, (((1,), (1,)), ((), ())), preferred_element_type=F32)

    def update(j, s, carry):
        m, l, acc = carry
        m_new = jnp.maximum(m, jnp.max(s, axis=-1, keepdims=True))
        alpha = jnp.exp(m - m_new)
        p = jnp.exp(s - m_new)
        l = alpha * l + jnp.sum(p, axis=-1, keepdims=True)
        v = v_ref[pl.ds(pl.multiple_of(j * tq, tq), tq), :]
        acc = alpha * acc + jnp.dot(p.astype(BF16), v, preferred_element_type=F32)
        return m_new, l, acc

    def body(j, carry):
        return update(j, scores(j), carry)

    init = (jnp.full((tq, 1), -jnp.inf, F32), jnp.zeros((tq, 1), F32), jnp.zeros((tq, V_HEAD), F32))
    carry = lax.fori_loop(0, i, body, init)
    row = lax.broadcasted_iota(jnp.int32, (tq, tq), 0)
    col = lax.broadcasted_iota(jnp.int32, (tq, tq), 1)
    s = jnp.where(col <= row, scores(i), -jnp.inf)
    _, l, acc = update(i, s, carry)
    o_ref[...] = (acc / l).astype(o_ref.dtype)


def _attention(q, k, v, batch, tq=256):
    t = q.shape[0]
    nq = SEQ // tq
    return pl.pallas_call(
        functools.partial(_attn_kernel, tq=tq),
        out_shape=jax.ShapeDtypeStruct((t, MLA_HEADS * V_HEAD), BF16),
        grid=(batch, MLA_HEADS, nq),
        in_specs=[
            pl.BlockSpec((tq, QK_PAD), lambda b, h, i: (b * nq + i, h)),
            pl.BlockSpec((SEQ, QK_PAD), lambda b, h, i: (b, h)),
            pl.BlockSpec((SEQ, V_HEAD), lambda b, h, i: (b, h)),
        ],
        out_specs=pl.BlockSpec((tq, V_HEAD), lambda b, h, i: (b * nq + i, h)),
        compiler_params=_params("parallel", "parallel", "arbitrary"),
        name="attention",
    )(q, k, v)


def _branch_kernel(a_ref, wa_ref, b_ref, wb_ref, ga_ref, gb_ref, o_ref):
    ya = jnp.dot(a_ref[...], wa_ref[...], preferred_element_type=F32)
    yb = jnp.dot(b_ref[...], wb_ref[...], preferred_element_type=F32)
    o_ref[...] = (ga_ref[...].astype(F32) * ya + gb_ref[...].astype(F32) * yb).astype(o_ref.dtype)


def _branch(a, wa, b, wb, gates, tm=1024, tn=512):
    t = a.shape[0]
    nb = D_MODEL // tn
    return pl.pallas_call(
        _branch_kernel,
        out_shape=jax.ShapeDtypeStruct((t, D_MODEL), BF16),
        grid=(t // tm, nb),
        in_specs=[
            pl.BlockSpec((tm, a.shape[1]), lambda m, j: (m, 0)),
            pl.BlockSpec((wa.shape[0], tn), lambda m, j: (0, j)),
            pl.BlockSpec((tm, b.shape[1]), lambda m, j: (m, 0)),
            pl.BlockSpec((wb.shape[0], tn), lambda m, j: (0, j)),
            pl.BlockSpec((tm, tn), lambda m, j: (m, j)),
            pl.BlockSpec((tm, tn), lambda m, j: (m, nb + j)),
        ],
        out_specs=pl.BlockSpec((tm, tn), lambda m, j: (m, j)),
        compiler_params=_params("parallel", "parallel"),
        name="branch_merge",
    )(a, wa, b, wb, gates, gates)


def _mm_resid_kernel(x_ref, w_ref, res_ref, gate_ref, o_ref, acc_ref, *, nk):
    kk = pl.program_id(2)
    part = jnp.dot(x_ref[...], w_ref[...], preferred_element_type=F32)
    if nk == 1:
        o_ref[...] = res_ref[...] + gate_ref[0] * part
        return

    @pl.when(kk == 0)
    def _():
        acc_ref[...] = part

    @pl.when((kk > 0) & (kk < nk - 1))
    def _():
        acc_ref[...] += part

    @pl.when(kk == nk - 1)
    def _():
        o_ref[...] = res_ref[...] + gate_ref[0] * (acc_ref[...] + part)


def _mm_resid(x, w, res, mod3, gate_idx, tm, tn, tk, name):
    t, k = x.shape
    n = w.shape[1]
    nk = k // tk
    return pl.pallas_call(
        functools.partial(_mm_resid_kernel, nk=nk),
        out_shape=jax.ShapeDtypeStruct((t, n), F32),
        grid=(t // tm, n // tn, nk),
        in_specs=[
            pl.BlockSpec((tm, tk), lambda m, j, kk: (m, kk)),
            pl.BlockSpec((tk, tn), lambda m, j, kk: (kk, j)),
            pl.BlockSpec((tm, tn), lambda m, j, kk: (m, j)),
            _mod_spec(gate_idx, tm, tn, ncol_arg=True),
        ],
        out_specs=pl.BlockSpec((tm, tn), lambda m, j, kk: (m, j)),
        scratch_shapes=[pltpu.VMEM((tm, tn), F32)],
        compiler_params=_params("parallel", "parallel", "arbitrary"),
        name=name,
    )(x, w, res, mod3)


def _pad_cols(a, width):
    return jnp.pad(a, ((0, 0), (0, width - a.shape[1])))


def _layer(x2, batch, mod3, c_tab, s1_tab, s2_tab, g_norm1, w_in, g_v, w_s, b_s, g_q_lat, g_kv_lat,
           w_uq, w_ukv, g_qn, g_kn, w_branch_a, w_branch_b, w_out, g_norm2, w_ff1, w_ff2):
    w_uv = w_in[:, :OFF_Q].astype(BF16)
    w_lat = _pad_cols(w_in[:, OFF_Q:OFF_GATE], LAT_COLS).astype(BF16)
    w_gate = w_in[:, OFF_GATE:].astype(BF16)
    wq = _pad_cols(w_uq.reshape(Q_LORA * MLA_HEADS, QK_HEAD), QK_PAD)
    wq = wq.reshape(Q_LORA, MLA_HEADS * QK_PAD).astype(BF16)
    wkv = w_ukv.astype(BF16)
    bias_full = jnp.repeat(b_s.T, GM_GROUP_W, axis=1)
    row = lambda v: v.reshape(1, -1)
    qg1, qg2 = row(g_qn[:QK_NOPE]), _pad_cols(row(g_qn[QK_NOPE:]), LANES)
    kg1, kg2 = row(g_kn[:QK_NOPE]), _pad_cols(row(g_kn[QK_NOPE:]), LANES)

    h1 = _normmod(x2, row(g_norm1), mod3, 0, 1)
    z = _mm_act(h1, w_uv, "gelu", BF16, 1024, 1024, "in_proj_uv")
    lat = _mm_act(h1, w_lat, "none", F32, 512, LAT_COLS, "in_proj_lat")
    gates = _mm_act(h1, w_gate, "sigmoid", BF16, 1024, 1024, "in_proj_gate")
    gm = _gmlp(z, row(g_v), w_s, bias_full)
    q = _qprep(lat, row(g_q_lat), wq, qg1, qg2, c_tab, s1_tab, s2_tab)
    k, v = _kvprep(lat, row(g_kv_lat), wkv, kg1, kg2, c_tab, s1_tab, s2_tab)
    att = _attention(q, k, v, batch)
    mixed = _branch(gm, w_branch_a.astype(BF16), att, w_branch_b.astype(BF16), gates)
    x2 = _mm_resid(mixed, w_out.astype(BF16), x2, mod3, 2, 1024, 512, D_MODEL, "out_proj")

    h2 = _normmod(x2, row(g_norm2), mod3, 3, 4)
    hid = _mm_act(h2, w_ff1.astype(BF16), "relu2", BF16, 1024, 1024, "ff1")
    return _mm_resid(hid, w_ff2.astype(BF16), x2, mod3, 5, 1024, 1024, 2048, "ff2")


def kernel(x, c, positions, w_ada, b_ada, g_norm1, w_in, g_v, w_s, b_s, g_q_lat, g_kv_lat, w_uq, w_ukv,
           g_qn, g_kn, w_branch_a, w_branch_b, w_out, g_norm2, w_ff1, w_ff2):
    batch, seq, d = x.shape
    assert (seq, d) == (SEQ, D_MODEL)
    depth = w_ada.shape[0]
    t = batch * seq
    x2 = x.reshape(t, d)

    inv_freq = 1.0 / (ROPE_THETA ** (jnp.arange(0, QK_ROPE, 2, dtype=F32) / QK_ROPE))
    freq_row = _pad_cols(jnp.concatenate([inv_freq, inv_freq]).reshape(1, QK_ROPE), LANES)
    c_tab, s1_tab, s2_tab = _rope_tables(positions.reshape(t, 1), freq_row)

    c8 = jnp.pad(c, ((0, 8 - batch), (0, 0)))
    for l in range(depth):
        mod = _ada(c8, w_ada[l], b_ada[l].reshape(1, -1))[:batch]
        mod3 = mod.reshape(batch * N_MOD, 1, d)
        x2 = _layer(x2, batch, mod3, c_tab, s1_tab, s2_tab, g_norm1[l], w_in[l], g_v[l], w_s[l], b_s[l],
                    g_q_lat[l], g_kv_lat[l], w_uq[l], w_ukv[l], g_qn[l], g_kn[l], w_branch_a[l],
                    w_branch_b[l], w_out[l], g_norm2[l], w_ff1[l], w_ff2[l])
    return x2.reshape(batch, seq, d)
```

```python
import functools
import math

import jax
import jax.numpy as jnp
from jax import lax
from jax.experimental import pallas as pl
from jax.experimental.pallas import tpu as pltpu

D_MODEL = 4096
SEQ = 2048
GM_WIDTH = 2048
GM_GROUPS = 8
GM_GROUP_W = GM_WIDTH // GM_GROUPS
CHUNK = 128
MLA_HEADS = 32
QK_NOPE = 128
QK_ROPE = 64
QK_HEAD = QK_NOPE + QK_ROPE
V_HEAD = 128
Q_LORA = 1024
KV_LORA = 512
ROPE_THETA = 10000.0
D_FF = 4 * D_MODEL
N_MOD = 6
EPS = 1e-6
OFF_Q = 2 * GM_WIDTH
OFF_KV = OFF_Q + Q_LORA
OFF_KPE = OFF_KV + KV_LORA
OFF_GATE = OFF_KPE + QK_ROPE

LANES = 128
QK_PAD = 2 * LANES
LAT_COLS = Q_LORA + KV_LORA + LANES
VMEM_LIMIT = 56 * 1024 * 1024

F32 = jnp.float32
BF16 = jnp.bfloat16


def _params(*sem):
    return pltpu.CompilerParams(dimension_semantics=sem, vmem_limit_bytes=VMEM_LIMIT)


def _ada_kernel(c_ref, w_ref, b_ref, o_ref):
    c = c_ref[...]
    cond = c * jax.nn.sigmoid(c)
    acc = jnp.dot(cond.astype(BF16), w_ref[...].astype(BF16), preferred_element_type=F32)
    o_ref[...] = acc + b_ref[...]


def _ada(c8, w_ada, b_ada, tn=1024):
    n = w_ada.shape[1]
    return pl.pallas_call(
        _ada_kernel,
        out_shape=jax.ShapeDtypeStruct((8, n), F32),
        grid=(n // tn,),
        in_specs=[
            pl.BlockSpec((8, D_MODEL), lambda j: (0, 0)),
            pl.BlockSpec((D_MODEL, tn), lambda j: (0, j)),
            pl.BlockSpec((1, tn), lambda j: (0, j)),
        ],
        out_specs=pl.BlockSpec((8, tn), lambda j: (0, j)),
        compiler_params=_params("parallel"),
        name="ada",
    )(c8, w_ada, b_ada)


def _normmod_kernel(x_ref, g_ref, sh_ref, sc_ref, o_ref):
    x = x_ref[...]
    y = x * lax.rsqrt(jnp.mean(x * x, axis=-1, keepdims=True) + EPS)
    y = y * g_ref[...]
    o_ref[...] = (y * (1.0 + sc_ref[0]) + sh_ref[0]).astype(o_ref.dtype)


def _mod_spec(idx, tm, width, ncol_arg=False):
    per_batch = SEQ // tm
    if ncol_arg:
        return pl.BlockSpec((1, 1, width), lambda m, n, *_: ((m // per_batch) * N_MOD + idx, 0, n))
    return pl.BlockSpec((1, 1, width), lambda m, *_: ((m // per_batch) * N_MOD + idx, 0, 0))


def _normmod(x2, g, mod3, shift_idx, scale_idx, tm=512):
    t, d = x2.shape
    return pl.pallas_call(
        _normmod_kernel,
        out_shape=jax.ShapeDtypeStruct((t, d), BF16),
        grid=(t // tm,),
        in_specs=[
            pl.BlockSpec((tm, d), lambda m: (m, 0)),
            pl.BlockSpec((1, d), lambda m: (0, 0)),
            _mod_spec(shift_idx, tm, d),
            _mod_spec(scale_idx, tm, d),
        ],
        out_specs=pl.BlockSpec((tm, d), lambda m: (m, 0)),
        compiler_params=_params("parallel"),
        name="normmod",
    )(x2, g, mod3, mod3)


def _gelu(v):
    return 0.5 * v * (1.0 + lax.erf(v * math.sqrt(0.5)))


def _relu2(v):
    r = jnp.maximum(v, 0.0)
    return r * r


_ACTS = {"gelu": _gelu, "sigmoid": jax.nn.sigmoid, "relu2": _relu2, "none": lambda v: v}


def _cast_specs(casts, n_steps, step_of):
    specs, shapes = [], []
    for a in casts:
        rows, cols = a.shape
        block = (rows // n_steps, cols)
        assert rows % (n_steps * 16) == 0
        specs.append(pl.BlockSpec(block, lambda *ids: (step_of(*ids), 0)))
        shapes.append(jax.ShapeDtypeStruct(a.shape, BF16))
    return specs, shapes


def _do_casts(refs, n):
    for src, dst in zip(refs[:n], refs[len(refs) - n:]):
        dst[...] = src[...].astype(BF16)


def _mm_act_kernel(x_ref, w_ref, *refs, act, n_cast):
    o_ref = refs[n_cast]
    acc = jnp.dot(x_ref[...], w_ref[...], preferred_element_type=F32)
    o_ref[...] = _ACTS[act](acc).astype(o_ref.dtype)
    _do_casts(refs, n_cast)


def _mm_act(x, w, act, out_dtype, tm, tn, name, casts=()):
    t, k = x.shape
    n = w.shape[1]
    gm, gn = t // tm, n // tn
    cast_specs, cast_shapes = _cast_specs(casts, gm * gn, lambda m, j: m * gn + j)
    out = pl.pallas_call(
        functools.partial(_mm_act_kernel, act=act, n_cast=len(casts)),
        out_shape=[jax.ShapeDtypeStruct((t, n), out_dtype)] + cast_shapes,
        grid=(gm, gn),
        in_specs=[
            pl.BlockSpec((tm, k), lambda m, j: (m, 0)),
            pl.BlockSpec((k, tn), lambda m, j: (0, j)),
        ] + cast_specs,
        out_specs=[pl.BlockSpec((tm, tn), lambda m, j: (m, j))] + cast_specs,
        compiler_params=_params("parallel", "parallel"),
        name=name,
    )(x, w, *casts)
    return out if casts else out[0]


def _gmlp_kernel(u_ref, v_ref, gv_ref, ws_ref, bias_ref, o_ref, *, n_chunks):
    row = lax.broadcasted_iota(jnp.int32, (CHUNK, CHUNK), 0)
    col = lax.broadcasted_iota(jnp.int32, (CHUNK, CHUNK), 1)
    causal = col <= row
    gv = gv_ref[...]
    for ci in range(n_chunks):
        rows = pl.ds(ci * CHUNK, CHUNK)
        v = v_ref[rows, :].astype(F32)
        vn = v * lax.rsqrt(jnp.mean(v * v, axis=-1, keepdims=True) + EPS) * gv
        vn = vn.astype(BF16)
        for g in range(GM_GROUPS):
            cols = slice(g * GM_GROUP_W, (g + 1) * GM_GROUP_W)
            w = jnp.where(causal, ws_ref[g], 0.0).astype(BF16)
            mixed = jnp.dot(w, vn[:, cols], preferred_element_type=F32) + bias_ref[:, cols]
            o_ref[rows, cols] = (u_ref[rows, cols].astype(F32) * mixed).astype(o_ref.dtype)


def _gmlp(z, g_v, w_s, bias_full, tm=512):
    t = z.shape[0]
    return pl.pallas_call(
        functools.partial(_gmlp_kernel, n_chunks=tm // CHUNK),
        out_shape=jax.ShapeDtypeStruct((t, GM_WIDTH), BF16),
        grid=(t // tm,),
        in_specs=[
            pl.BlockSpec((tm, GM_WIDTH), lambda m: (m, 0)),
            pl.BlockSpec((tm, GM_WIDTH), lambda m: (m, 1)),
            pl.BlockSpec((1, GM_WIDTH), lambda m: (0, 0)),
            pl.BlockSpec((GM_GROUPS, CHUNK, CHUNK), lambda m: (0, 0, 0)),
            pl.BlockSpec((CHUNK, GM_WIDTH), lambda m: (0, 0)),
        ],
        out_specs=pl.BlockSpec((tm, GM_WIDTH), lambda m: (m, 0)),
        compiler_params=_params("parallel"),
        name="gmlp",
    )(z, z, g_v, w_s, bias_full)


def _rope_tab_kernel(pos_ref, freq_ref, c_ref, s_ref):
    ang = pos_ref[...].astype(F32) * freq_ref[...]
    lane = lax.broadcasted_iota(jnp.int32, ang.shape, 1)
    cos = jnp.cos(ang)
    sin = jnp.sin(ang)
    c_ref[...] = jnp.where(lane < QK_ROPE, cos, 0.0)
    s_ref[...] = jnp.where(lane < QK_ROPE // 2, -sin, jnp.where(lane < QK_ROPE, sin, 0.0))


def _rope_tables(pos_col, freq_row, tm=1024):
    t = pos_col.shape[0]
    shp = jax.ShapeDtypeStruct((t, LANES), F32)
    return pl.pallas_call(
        _rope_tab_kernel,
        out_shape=(shp, shp),
        grid=(t // tm,),
        in_specs=[
            pl.BlockSpec((tm, 1), lambda m: (m, 0)),
            pl.BlockSpec((1, LANES), lambda m: (0, 0)),
        ],
        out_specs=tuple(pl.BlockSpec((tm, LANES), lambda m: (m, 0)) for _ in range(2)),
        compiler_params=_params("parallel"),
        name="rope_tables",
    )(pos_col, freq_row)


def _rope(v, c, s):
    return v * c + pltpu.roll(v, QK_ROPE // 2, 1) * s


def _rms_rows(x, g):
    return x * lax.rsqrt(jnp.mean(x * x, axis=-1, keepdims=True) + EPS) * g


def _qprep_kernel(lat_ref, glat_ref, w_ref, g1_ref, g2_ref, c_ref, s_ref, o_ref, xn_ref, *, heads):
    @pl.when(pl.program_id(1) == 0)
    def _():
        xn_ref[...] = _rms_rows(lat_ref[...], glat_ref[...]).astype(BF16)

    xn = xn_ref[...]
    c, s = c_ref[...], s_ref[...]
    g1, g2 = g1_ref[...], g2_ref[...]
    scale = QK_HEAD ** -0.5 * math.log2(math.e)
    for h in range(heads):
        acc = jnp.dot(xn, w_ref[:, h * QK_PAD:(h + 1) * QK_PAD], preferred_element_type=F32)
        a = acc[:, :LANES]
        b = acc[:, LANES:]
        ssq = jnp.sum(a * a + 0.5 * (b * b), axis=-1, keepdims=True)
        r = lax.rsqrt(ssq * (1.0 / QK_HEAD) + EPS) * scale
        o_ref[:, h * QK_PAD:h * QK_PAD + LANES] = (a * r * g1).astype(o_ref.dtype)
        o_ref[:, h * QK_PAD + LANES:(h + 1) * QK_PAD] = _rope(b * r * g2, c, s).astype(o_ref.dtype)


def _qprep(lat, g_q_lat, wq, g1, g2, c, s, tm=1024, heads=8):
    t = lat.shape[0]
    tn = heads * QK_PAD
    row = lambda m, j: (m, 0)
    const = lambda m, j: (0, 0)
    return pl.pallas_call(
        functools.partial(_qprep_kernel, heads=heads),
        out_shape=jax.ShapeDtypeStruct((t, MLA_HEADS * QK_PAD), BF16),
        grid=(t // tm, MLA_HEADS // heads),
        in_specs=[
            pl.BlockSpec((tm, Q_LORA), row),
            pl.BlockSpec((1, Q_LORA), const),
            pl.BlockSpec((Q_LORA, tn), lambda m, j: (0, j)),
            pl.BlockSpec((1, LANES), const),
            pl.BlockSpec((1, LANES), const),
            pl.BlockSpec((tm, LANES), row),
            pl.BlockSpec((tm, LANES), row),
        ],
        out_specs=pl.BlockSpec((tm, tn), lambda m, j: (m, j)),
        scratch_shapes=[pltpu.VMEM((tm, Q_LORA), BF16)],
        compiler_params=_params("parallel", "arbitrary"),
        name="q_prep",
    )(lat, g_q_lat, wq, g1, g2, c, s)


def _kvprep_kernel(lat_ref, kpe_ref, glat_ref, w_ref, g1_ref, g2_ref, c_ref, s_ref,
                   k_ref, v_ref, xn_ref, *, heads):
    @pl.when(pl.program_id(1) == 0)
    def _():
        xn_ref[...] = _rms_rows(lat_ref[...], glat_ref[...]).astype(BF16)

    xn = xn_ref[...]
    kpe = kpe_ref[...]
    ssq_pe = 0.5 * jnp.sum(kpe * kpe, axis=-1, keepdims=True)
    kr = _rope(kpe * g2_ref[...], c_ref[...], s_ref[...])
    g1 = g1_ref[...]
    width = QK_NOPE + V_HEAD
    for h in range(heads):
        acc = jnp.dot(xn, w_ref[:, h * width:(h + 1) * width], preferred_element_type=F32)
        a = acc[:, :QK_NOPE]
        ssq = jnp.sum(a * a, axis=-1, keepdims=True) + ssq_pe
        r = lax.rsqrt(ssq * (1.0 / QK_HEAD) + EPS)
        k_ref[:, h * QK_PAD:h * QK_PAD + LANES] = (a * r * g1).astype(k_ref.dtype)
        k_ref[:, h * QK_PAD + LANES:(h + 1) * QK_PAD] = (kr * r).astype(k_ref.dtype)
        v_ref[:, h * V_HEAD:(h + 1) * V_HEAD] = acc[:, QK_NOPE:].astype(v_ref.dtype)


def _kvprep(lat, g_kv_lat, wkv, g1, g2, c, s, tm=1024, heads=8):
    t = lat.shape[0]
    row = lambda m, j: (m, 0)
    const = lambda m, j: (0, 0)
    return pl.pallas_call(
        functools.partial(_kvprep_kernel, heads=heads),
        out_shape=(jax.ShapeDtypeStruct((t, MLA_HEADS * QK_PAD), BF16),
                   jax.ShapeDtypeStruct((t, MLA_HEADS * V_HEAD), BF16)),
        grid=(t // tm, MLA_HEADS // heads),
        in_specs=[
            pl.BlockSpec((tm, KV_LORA), lambda m, j: (m, Q_LORA // KV_LORA)),
            pl.BlockSpec((tm, LANES), lambda m, j: (m, (Q_LORA + KV_LORA) // LANES)),
            pl.BlockSpec((1, KV_LORA), const),
            pl.BlockSpec((KV_LORA, heads * (QK_NOPE + V_HEAD)), lambda m, j: (0, j)),
            pl.BlockSpec((1, LANES), const),
            pl.BlockSpec((1, LANES), const),
            pl.BlockSpec((tm, LANES), row),
            pl.BlockSpec((tm, LANES), row),
        ],
        out_specs=(pl.BlockSpec((tm, heads * QK_PAD), lambda m, j: (m, j)),
                   pl.BlockSpec((tm, heads * V_HEAD), lambda m, j: (m, j))),
        scratch_shapes=[pltpu.VMEM((tm, KV_LORA), BF16)],
        compiler_params=_params("parallel", "arbitrary"),
        name="kv_prep",
    )(lat, lat, g_kv_lat, wkv, g1, g2, c, s)


def _attn_kernel(q_ref, k_ref, v_ref, o_ref, m_ref, l_ref, acc_ref, *, tq, heads):
    i = pl.program_id(2)
    nc = tq // LANES
    m_ref[...] = jnp.full(m_ref.shape, -jnp.inf, F32)
    l_ref[...] = jnp.zeros(l_ref.shape, F32)
    acc_ref[...] = jnp.zeros(acc_ref.shape, F32)

    def block(j, masked):
        rows = pl.ds(pl.multiple_of(j * tq, tq), tq)
        for h in range(heads):
            q = q_ref[:, h * QK_PAD:(h + 1) * QK_PAD]
            k = k_ref[rows, h * QK_PAD:(h + 1) * QK_PAD]
            s = lax.dot_general(q, k, (((1,), (1,)), ((), ())), preferred_element_type=F32)
            if masked:
                row = lax.broadcasted_iota(jnp.int32, (tq, tq), 0)
                col = lax.broadcasted_iota(jnp.int32, (tq, tq), 1)
                s = jnp.where(col <= row, s, -jnp.inf)
            chunks = [s[:, c * LANES:(c + 1) * LANES] for c in range(nc)]
            m_part = functools.reduce(jnp.maximum, chunks)
            m_prev = m_ref[h]
            m_cur = jnp.broadcast_to(jnp.max(m_part, axis=1, keepdims=True), m_part.shape)
            m_new = jnp.maximum(m_prev, m_cur)
            alpha = jnp.exp2(m_prev - m_new)
            ps = [jnp.exp2(ch - m_new) for ch in chunks]
            l_ref[h] = alpha * l_ref[h] + functools.reduce(jnp.add, ps)
            m_ref[h] = m_new
            p = jnp.concatenate(ps, axis=1).astype(BF16)
            v = v_ref[rows, h * V_HEAD:(h + 1) * V_HEAD]
            acc_ref[h] = alpha * acc_ref[h] + jnp.dot(p, v, preferred_element_type=F32)

    def body(j, carry):
        block(j, False)
        return carry

    lax.fori_loop(0, i, body, 0)
    block(i, True)
    for h in range(heads):
        l = jnp.sum(l_ref[h], axis=1, keepdims=True)
        o_ref[:, h * V_HEAD:(h + 1) * V_HEAD] = (acc_ref[h] / l).astype(o_ref.dtype)


def _attention(q, k, v, batch, tq=512, heads=4):
    t = q.shape[0]
    nq = SEQ // tq
    stat = pltpu.VMEM((heads, tq, LANES), F32)
    return pl.pallas_call(
        functools.partial(_attn_kernel, tq=tq, heads=heads),
        out_shape=jax.ShapeDtypeStruct((t, MLA_HEADS * V_HEAD), BF16),
        grid=(batch, MLA_HEADS // heads, nq),
        in_specs=[
            pl.BlockSpec((tq, heads * QK_PAD), lambda b, h, i: (b * nq + i, h)),
            pl.BlockSpec((SEQ, heads * QK_PAD), lambda b, h, i: (b, h)),
            pl.BlockSpec((SEQ, heads * V_HEAD), lambda b, h, i: (b, h)),
        ],
        out_specs=pl.BlockSpec((tq, heads * V_HEAD), lambda b, h, i: (b * nq + i, h)),
        scratch_shapes=[stat, stat, pltpu.VMEM((heads, tq, V_HEAD), F32)],
        compiler_params=_params("parallel", "parallel", "arbitrary"),
        name="attention",
    )(q, k, v)


def _branch_kernel(a_ref, wa_ref, b_ref, wb_ref, ga_ref, gb_ref, o_ref):
    ya = jnp.dot(a_ref[...], wa_ref[...], preferred_element_type=F32)
    yb = jnp.dot(b_ref[...], wb_ref[...], preferred_element_type=F32)
    o_ref[...] = (ga_ref[...].astype(F32) * ya + gb_ref[...].astype(F32) * yb).astype(o_ref.dtype)


def _branch(a, wa, b, wb, gates, tm=1024, tn=512):
    t = a.shape[0]
    nb = D_MODEL // tn
    return pl.pallas_call(
        _branch_kernel,
        out_shape=jax.ShapeDtypeStruct((t, D_MODEL), BF16),
        grid=(t // tm, nb),
        in_specs=[
            pl.BlockSpec((tm, a.shape[1]), lambda m, j: (m, 0)),
            pl.BlockSpec((wa.shape[0], tn), lambda m, j: (0, j)),
            pl.BlockSpec((tm, b.shape[1]), lambda m, j: (m, 0)),
            pl.BlockSpec((wb.shape[0], tn), lambda m, j: (0, j)),
            pl.BlockSpec((tm, tn), lambda m, j: (m, j)),
            pl.BlockSpec((tm, tn), lambda m, j: (m, nb + j)),
        ],
        out_specs=pl.BlockSpec((tm, tn), lambda m, j: (m, j)),
        compiler_params=_params("parallel", "parallel"),
        name="branch_merge",
    )(a, wa, b, wb, gates, gates)


def _mm_resid_kernel(x_ref, w_ref, res_ref, gate_ref, *refs, nk, n_cast):
    o_ref, acc_ref = refs[n_cast], refs[-1]
    _do_casts(refs[:-1], n_cast)
    kk = pl.program_id(2)
    if nk == 1:
        part = jnp.dot(x_ref[...], w_ref[...], preferred_element_type=F32)
        o_ref[...] = res_ref[...] + gate_ref[0] * part
        return

    @pl.when(kk == 0)
    def _():
        acc_ref[...] = jnp.zeros(acc_ref.shape, F32)

    acc_ref[...] += jnp.dot(x_ref[...], w_ref[...], preferred_element_type=F32)

    @pl.when(kk == nk - 1)
    def _():
        o_ref[...] = res_ref[...] + gate_ref[0] * acc_ref[...]


def _mm_resid(x, w, res, mod3, gate_idx, tm, tn, tk, name, casts=()):
    t, k = x.shape
    n = w.shape[1]
    gm, gn, nk = t // tm, n // tn, k // tk
    cast_specs, cast_shapes = _cast_specs(casts, gm * gn * nk, lambda m, j, kk: (m * gn + j) * nk + kk)
    out = pl.pallas_call(
        functools.partial(_mm_resid_kernel, nk=nk, n_cast=len(casts)),
        out_shape=[jax.ShapeDtypeStruct((t, n), F32)] + cast_shapes,
        grid=(gm, gn, nk),
        in_specs=[
            pl.BlockSpec((tm, tk), lambda m, j, kk: (m, kk)),
            pl.BlockSpec((tk, tn), lambda m, j, kk: (kk, j)),
            pl.BlockSpec((tm, tn), lambda m, j, kk: (m, j)),
            _mod_spec(gate_idx, tm, tn, ncol_arg=True),
        ] + cast_specs,
        out_specs=[pl.BlockSpec((tm, tn), lambda m, j, kk: (m, j))] + cast_specs,
        scratch_shapes=[pltpu.VMEM((tm, tn), F32)],
        compiler_params=_params("parallel", "parallel", "arbitrary"),
        name=name,
    )(x, w, res, mod3, *casts)
    return out if casts else out[0]


def _pad_cols(a, width):
    return jnp.pad(a, ((0, 0), (0, width - a.shape[1])))


def _layer(x2, batch, mod3, c_tab, s_tab, g_norm1, w_in, g_v, w_s, b_s, g_q_lat, g_kv_lat,
           w_uq, w_ukv, g_qn, g_kn, w_branch_a, w_branch_b, w_out, g_norm2, w_ff1, w_ff2):
    w_uv = w_in[:, :OFF_Q].astype(BF16)
    w_lat = jnp.concatenate([w_in[:, OFF_Q:OFF_GATE], w_in[:, OFF_KPE:OFF_GATE]], axis=1).astype(BF16)
    w_gate = w_in[:, OFF_GATE:].astype(BF16)
    w3 = w_uq.reshape(Q_LORA, MLA_HEADS, QK_HEAD)
    wq = jnp.concatenate([w3, w3[:, :, QK_NOPE:]], axis=2).reshape(Q_LORA, MLA_HEADS * QK_PAD).astype(BF16)
    wkv = w_ukv.astype(BF16)
    bias_full = jnp.repeat(b_s.T, GM_GROUP_W, axis=1)
    row = lambda v: v.reshape(1, -1)
    twice = lambda v: jnp.concatenate([v, v]).reshape(1, -1)
    qg1, qg2 = row(g_qn[:QK_NOPE]), twice(g_qn[QK_NOPE:])
    kg1, kg2 = row(g_kn[:QK_NOPE]), twice(g_kn[QK_NOPE:])

    h1 = _normmod(x2, row(g_norm1), mod3, 0, 1)
    z = _mm_act(h1, w_uv, "gelu", BF16, 1024, 1024, "in_proj_uv")
    lat = _mm_act(h1, w_lat, "none", F32, 512, LAT_COLS, "in_proj_lat")
    gates, wa, wb, wo = _mm_act(h1, w_gate, "sigmoid", BF16, 1024, 1024, "in_proj_gate",
                                casts=(w_branch_a, w_branch_b, w_out))
    gm = _gmlp(z, row(g_v), w_s, bias_full)
    q = _qprep(lat, row(g_q_lat), wq, qg1, qg2, c_tab, s_tab)
    k, v = _kvprep(lat, row(g_kv_lat), wkv, kg1, kg2, c_tab, s_tab)
    att = _attention(q, k, v, batch)
    mixed = _branch(gm, wa, att, wb, gates)
    x2, w1 = _mm_resid(mixed, wo, x2, mod3, 2, 1024, 512, D_MODEL, "out_proj", casts=(w_ff1,))

    h2 = _normmod(x2, row(g_norm2), mod3, 3, 4)
    hid, w2 = _mm_act(h2, w1, "relu2", BF16, 1024, 1024, "ff1", casts=(w_ff2,))
    return _mm_resid(hid, w2, x2, mod3, 5, 1024, 1024, 2048, "ff2")


def kernel(x, c, positions, w_ada, b_ada, g_norm1, w_in, g_v, w_s, b_s, g_q_lat, g_kv_lat, w_uq, w_ukv,
           g_qn, g_kn, w_branch_a, w_branch_b, w_out, g_norm2, w_ff1, w_ff2):
    batch, seq, d = x.shape
    assert (seq, d) == (SEQ, D_MODEL)
    depth = w_ada.shape[0]
    t = batch * seq
    x2 = x.reshape(t, d)

    inv_freq = 1.0 / (ROPE_THETA ** (jnp.arange(0, QK_ROPE, 2, dtype=F32) / QK_ROPE))
    freq_row = _pad_cols(jnp.concatenate([inv_freq, inv_freq]).reshape(1, QK_ROPE), LANES)
    c_tab, s_tab = _rope_tables(positions.reshape(t, 1), freq_row)

    c8 = jnp.pad(c, ((0, 8 - batch), (0, 0)))
    for l in range(depth):
        mod = _ada(c8, w_ada[l], b_ada[l].reshape(1, -1))[:batch]
        mod3 = mod.reshape(batch * N_MOD, 1, d)
        x2 = _layer(x2, batch, mod3, c_tab, s_tab, g_norm1[l], w_in[l], g_v[l], w_s[l], b_s[l],
                    g_q_lat[l], g_kv_lat[l], w_uq[l], w_ukv[l], g_qn[l], g_kn[l], w_branch_a[l],
                    w_branch_b[l], w_out[l], g_norm2[l], w_ff1[l], w_ff2[l])
    return x2.reshape(batch, seq, d)
```

```python
import functools
import math

import jax
import jax.numpy as jnp
from jax import lax
from jax.experimental import pallas as pl
from jax.experimental.pallas import tpu as pltpu

D_MODEL = 4096
SEQ = 2048
GM_WIDTH = 2048
GM_GROUPS = 8
GM_GROUP_W = GM_WIDTH // GM_GROUPS
CHUNK = 128
MLA_HEADS = 32
QK_NOPE = 128
QK_ROPE = 64
QK_HEAD = QK_NOPE + QK_ROPE
V_HEAD = 128
Q_LORA = 1024
KV_LORA = 512
ROPE_THETA = 10000.0
D_FF = 4 * D_MODEL
N_MOD = 6
EPS = 1e-6
OFF_Q = 2 * GM_WIDTH
OFF_KV = OFF_Q + Q_LORA
OFF_KPE = OFF_KV + KV_LORA
OFF_GATE = OFF_KPE + QK_ROPE

LANES = 128
QK_PAD = 2 * LANES
LAT_COLS = Q_LORA + KV_LORA + LANES
VMEM_LIMIT = 56 * 1024 * 1024

F32 = jnp.float32
BF16 = jnp.bfloat16


def _params(*sem):
    return pltpu.CompilerParams(dimension_semantics=sem, vmem_limit_bytes=VMEM_LIMIT)


def _ada_kernel(c_ref, w_ref, b_ref, o_ref):
    c = c_ref[...]
    cond = c * jax.nn.sigmoid(c)
    acc = jnp.dot(cond.astype(BF16), w_ref[...].astype(BF16), preferred_element_type=F32)
    o_ref[...] = acc + b_ref[...]


def _ada(c8, w_ada, b_ada, tn=1024):
    n = w_ada.shape[1]
    return pl.pallas_call(
        _ada_kernel,
        out_shape=jax.ShapeDtypeStruct((8, n), F32),
        grid=(n // tn,),
        in_specs=[
            pl.BlockSpec((8, D_MODEL), lambda j: (0, 0)),
            pl.BlockSpec((D_MODEL, tn), lambda j: (0, j)),
            pl.BlockSpec((1, tn), lambda j: (0, j)),
        ],
        out_specs=pl.BlockSpec((8, tn), lambda j: (0, j)),
        compiler_params=_params("parallel"),
        name="ada",
    )(c8, w_ada, b_ada)


def _normmod_kernel(x_ref, g_ref, sh_ref, sc_ref, o_ref):
    x = x_ref[...]
    y = x * lax.rsqrt(jnp.mean(x * x, axis=-1, keepdims=True) + EPS)
    y = y * g_ref[...]
    o_ref[...] = (y * (1.0 + sc_ref[0]) + sh_ref[0]).astype(o_ref.dtype)


def _mod_spec(idx, tm, width, ncol_arg=False):
    per_batch = SEQ // tm
    if ncol_arg:
        return pl.BlockSpec((1, 1, width), lambda m, n, *_: ((m // per_batch) * N_MOD + idx, 0, n))
    return pl.BlockSpec((1, 1, width), lambda m, *_: ((m // per_batch) * N_MOD + idx, 0, 0))


def _normmod(x2, g, mod3, shift_idx, scale_idx, tm=512):
    t, d = x2.shape
    return pl.pallas_call(
        _normmod_kernel,
        out_shape=jax.ShapeDtypeStruct((t, d), BF16),
        grid=(t // tm,),
        in_specs=[
            pl.BlockSpec((tm, d), lambda m: (m, 0)),
            pl.BlockSpec((1, d), lambda m: (0, 0)),
            _mod_spec(shift_idx, tm, d),
            _mod_spec(scale_idx, tm, d),
        ],
        out_specs=pl.BlockSpec((tm, d), lambda m: (m, 0)),
        compiler_params=_params("parallel"),
        name="normmod",
    )(x2, g, mod3, mod3)


def _gelu(v):
    return 0.5 * v * (1.0 + lax.erf(v * math.sqrt(0.5)))


def _relu2(v):
    r = jnp.maximum(v, 0.0)
    return r * r


_ACTS = {"gelu": _gelu, "sigmoid": jax.nn.sigmoid, "relu2": _relu2, "none": lambda v: v}


def _cast_specs(casts, n_steps, step_of):
    specs, shapes = [], []
    for a in casts:
        rows, cols = a.shape
        block = (rows // n_steps, cols)
        assert rows % (n_steps * 16) == 0
        specs.append(pl.BlockSpec(block, lambda *ids: (step_of(*ids), 0)))
        shapes.append(jax.ShapeDtypeStruct(a.shape, BF16))
    return specs, shapes


def _do_casts(refs, n):
    for src, dst in zip(refs[:n], refs[len(refs) - n:]):
        dst[...] = src[...].astype(BF16)


def _mm_act_kernel(x_ref, w_ref, *refs, act, n_cast):
    o_ref = refs[n_cast]
    acc = jnp.dot(x_ref[...], w_ref[...], preferred_element_type=F32)
    o_ref[...] = _ACTS[act](acc).astype(o_ref.dtype)
    _do_casts(refs, n_cast)


def _mm_act(x, w, act, out_dtype, tm, tn, name, casts=()):
    t, k = x.shape
    n = w.shape[1]
    gm, gn = t // tm, n // tn
    cast_specs, cast_shapes = _cast_specs(casts, gm * gn, lambda m, j: m * gn + j)
    out = pl.pallas_call(
        functools.partial(_mm_act_kernel, act=act, n_cast=len(casts)),
        out_shape=[jax.ShapeDtypeStruct((t, n), out_dtype)] + cast_shapes,
        grid=(gm, gn),
        in_specs=[
            pl.BlockSpec((tm, k), lambda m, j: (m, 0)),
            pl.BlockSpec((k, tn), lambda m, j: (0, j)),
        ] + cast_specs,
        out_specs=[pl.BlockSpec((tm, tn), lambda m, j: (m, j))] + cast_specs,
        compiler_params=_params("parallel", "parallel"),
        name=name,
    )(x, w, *casts)
    return out if casts else out[0]


def _gmlp_kernel(u_ref, v_ref, gv_ref, ws_ref, bias_ref, o_ref, *, n_chunks):
    row = lax.broadcasted_iota(jnp.int32, (CHUNK, CHUNK), 0)
    col = lax.broadcasted_iota(jnp.int32, (CHUNK, CHUNK), 1)
    causal = col <= row
    gv = gv_ref[...]
    for ci in range(n_chunks):
        rows = pl.ds(ci * CHUNK, CHUNK)
        v = v_ref[rows, :].astype(F32)
        vn = v * lax.rsqrt(jnp.mean(v * v, axis=-1, keepdims=True) + EPS) * gv
        vn = vn.astype(BF16)
        for g in range(GM_GROUPS):
            cols = slice(g * GM_GROUP_W, (g + 1) * GM_GROUP_W)
            w = jnp.where(causal, ws_ref[g], 0.0).astype(BF16)
            mixed = jnp.dot(w, vn[:, cols], preferred_element_type=F32) + bias_ref[:, cols]
            o_ref[rows, cols] = (u_ref[rows, cols].astype(F32) * mixed).astype(o_ref.dtype)


def _gmlp(z, g_v, w_s, bias_full, tm=512):
    t = z.shape[0]
    return pl.pallas_call(
        functools.partial(_gmlp_kernel, n_chunks=tm // CHUNK),
        out_shape=jax.ShapeDtypeStruct((t, GM_WIDTH), BF16),
        grid=(t // tm,),
        in_specs=[
            pl.BlockSpec((tm, GM_WIDTH), lambda m: (m, 0)),
            pl.BlockSpec((tm, GM_WIDTH), lambda m: (m, 1)),
            pl.BlockSpec((1, GM_WIDTH), lambda m: (0, 0)),
            pl.BlockSpec((GM_GROUPS, CHUNK, CHUNK), lambda m: (0, 0, 0)),
            pl.BlockSpec((CHUNK, GM_WIDTH), lambda m: (0, 0)),
        ],
        out_specs=pl.BlockSpec((tm, GM_WIDTH), lambda m: (m, 0)),
        compiler_params=_params("parallel"),
        name="gmlp",
    )(z, z, g_v, w_s, bias_full)


def _rope_tab_kernel(pos_ref, freq_ref, c_ref, s_ref):
    ang = pos_ref[...].astype(F32) * freq_ref[...]
    lane = lax.broadcasted_iota(jnp.int32, ang.shape, 1)
    cos = jnp.cos(ang)
    sin = jnp.sin(ang)
    c_ref[...] = jnp.where(lane < QK_ROPE, cos, 0.0)
    s_ref[...] = jnp.where(lane < QK_ROPE // 2, -sin, jnp.where(lane < QK_ROPE, sin, 0.0))


def _rope_tables(pos_col, freq_row, tm=1024):
    t = pos_col.shape[0]
    shp = jax.ShapeDtypeStruct((t, LANES), F32)
    return pl.pallas_call(
        _rope_tab_kernel,
        out_shape=(shp, shp),
        grid=(t // tm,),
        in_specs=[
            pl.BlockSpec((tm, 1), lambda m: (m, 0)),
            pl.BlockSpec((1, LANES), lambda m: (0, 0)),
        ],
        out_specs=tuple(pl.BlockSpec((tm, LANES), lambda m: (m, 0)) for _ in range(2)),
        compiler_params=_params("parallel"),
        name="rope_tables",
    )(pos_col, freq_row)


def _rope(v, c, s):
    return v * c + pltpu.roll(v, QK_ROPE // 2, 1) * s


def _rms_rows(x, g):
    return x * lax.rsqrt(jnp.mean(x * x, axis=-1, keepdims=True) + EPS) * g


def _qprep_kernel(lat_ref, glat_ref, w_ref, g1_ref, g2_ref, c_ref, s_ref, o_ref, xn_ref, *, heads):
    @pl.when(pl.program_id(1) == 0)
    def _():
        xn_ref[...] = _rms_rows(lat_ref[...], glat_ref[...]).astype(BF16)

    xn = xn_ref[...]
    c, s = c_ref[...], s_ref[...]
    g1, g2 = g1_ref[...], g2_ref[...]
    scale = QK_HEAD ** -0.5 * math.log2(math.e)
    for h in range(heads):
        acc = jnp.dot(xn, w_ref[:, h * QK_PAD:(h + 1) * QK_PAD], preferred_element_type=F32)
        a = acc[:, :LANES]
        b = acc[:, LANES:]
        ssq = jnp.sum(a * a + 0.5 * (b * b), axis=-1, keepdims=True)
        r = lax.rsqrt(ssq * (1.0 / QK_HEAD) + EPS) * scale
        o_ref[:, h * QK_PAD:h * QK_PAD + LANES] = (a * r * g1).astype(o_ref.dtype)
        o_ref[:, h * QK_PAD + LANES:(h + 1) * QK_PAD] = _rope(b * r * g2, c, s).astype(o_ref.dtype)


def _qprep(lat, g_q_lat, wq, g1, g2, c, s, tm=1024, heads=8):
    t = lat.shape[0]
    tn = heads * QK_PAD
    row = lambda m, j: (m, 0)
    const = lambda m, j: (0, 0)
    return pl.pallas_call(
        functools.partial(_qprep_kernel, heads=heads),
        out_shape=jax.ShapeDtypeStruct((t, MLA_HEADS * QK_PAD), BF16),
        grid=(t // tm, MLA_HEADS // heads),
        in_specs=[
            pl.BlockSpec((tm, Q_LORA), row),
            pl.BlockSpec((1, Q_LORA), const),
            pl.BlockSpec((Q_LORA, tn), lambda m, j: (0, j)),
            pl.BlockSpec((1, LANES), const),
            pl.BlockSpec((1, LANES), const),
            pl.BlockSpec((tm, LANES), row),
            pl.BlockSpec((tm, LANES), row),
        ],
        out_specs=pl.BlockSpec((tm, tn), lambda m, j: (m, j)),
        scratch_shapes=[pltpu.VMEM((tm, Q_LORA), BF16)],
        compiler_params=_params("parallel", "arbitrary"),
        name="q_prep",
    )(lat, g_q_lat, wq, g1, g2, c, s)


def _kvprep_kernel(lat_ref, kpe_ref, glat_ref, w_ref, g1_ref, g2_ref, c_ref, s_ref,
                   k_ref, v_ref, xn_ref, *, heads):
    @pl.when(pl.program_id(1) == 0)
    def _():
        xn_ref[...] = _rms_rows(lat_ref[...], glat_ref[...]).astype(BF16)

    xn = xn_ref[...]
    kpe = kpe_ref[...]
    ssq_pe = 0.5 * jnp.sum(kpe * kpe, axis=-1, keepdims=True)
    kr = _rope(kpe * g2_ref[...], c_ref[...], s_ref[...])
    g1 = g1_ref[...]
    width = QK_NOPE + V_HEAD
    for h in range(heads):
        acc = jnp.dot(xn, w_ref[:, h * width:(h + 1) * width], preferred_element_type=F32)
        a = acc[:, :QK_NOPE]
        ssq = jnp.sum(a * a, axis=-1, keepdims=True) + ssq_pe
        r = lax.rsqrt(ssq * (1.0 / QK_HEAD) + EPS)
        k_ref[:, h * QK_PAD:h * QK_PAD + LANES] = (a * r * g1).astype(k_ref.dtype)
        k_ref[:, h * QK_PAD + LANES:(h + 1) * QK_PAD] = (kr * r).astype(k_ref.dtype)
        v_ref[:, h * V_HEAD:(h + 1) * V_HEAD] = acc[:, QK_NOPE:].astype(v_ref.dtype)


def _kvprep(lat, g_kv_lat, wkv, g1, g2, c, s, tm=1024, heads=8):
    t = lat.shape[0]
    row = lambda m, j: (m, 0)
    const = lambda m, j: (0, 0)
    return pl.pallas_call(
        functools.partial(_kvprep_kernel, heads=heads),
        out_shape=(jax.ShapeDtypeStruct((t, MLA_HEADS * QK_PAD), BF16),
                   jax.ShapeDtypeStruct((t, MLA_HEADS * V_HEAD), BF16)),
        grid=(t // tm, MLA_HEADS // heads),
        in_specs=[
            pl.BlockSpec((tm, KV_LORA), lambda m, j: (m, Q_LORA // KV_LORA)),
            pl.BlockSpec((tm, LANES), lambda m, j: (m, (Q_LORA + KV_LORA) // LANES)),
            pl.BlockSpec((1, KV_LORA), const),
            pl.BlockSpec((KV_LORA, heads * (QK_NOPE + V_HEAD)), lambda m, j: (0, j)),
            pl.BlockSpec((1, LANES), const),
            pl.BlockSpec((1, LANES), const),
            pl.BlockSpec((tm, LANES), row),
            pl.BlockSpec((tm, LANES), row),
        ],
        out_specs=(pl.BlockSpec((tm, heads * QK_PAD), lambda m, j: (m, j)),
                   pl.BlockSpec((tm, heads * V_HEAD), lambda m, j: (m, j))),
        scratch_shapes=[pltpu.VMEM((tm, KV_LORA), BF16)],
        compiler_params=_params("parallel", "arbitrary"),
        name="kv_prep",
    )(lat, lat, g_kv_lat, wkv, g1, g2, c, s)


def _attn_kernel(q_ref, k_ref, v_ref, *refs, tq, heads, n_cast):
    o_ref, m_ref, acc_ref = refs[n_cast], refs[-2], refs[-1]
    _do_casts(refs[:-2], n_cast)
    i = pl.program_id(2)
    nc = tq // LANES
    m_ref[...] = jnp.full(m_ref.shape, -jnp.inf, F32)
    acc_ref[...] = jnp.zeros(acc_ref.shape, F32)

    def block(j, masked):
        rows = pl.ds(pl.multiple_of(j * tq, tq), tq)
        for h in range(heads):
            q = q_ref[:, h * QK_PAD:(h + 1) * QK_PAD]
            k = k_ref[rows, h * QK_PAD:(h + 1) * QK_PAD]
            s = lax.dot_general(q, k, (((1,), (1,)), ((), ())), preferred_element_type=F32)
            if masked:
                row = lax.broadcasted_iota(jnp.int32, (tq, tq), 0)
                col = lax.broadcasted_iota(jnp.int32, (tq, tq), 1)
                s = jnp.where(col <= row, s, -jnp.inf)
            chunks = [s[:, c * LANES:(c + 1) * LANES] for c in range(nc)]
            m_part = functools.reduce(jnp.maximum, chunks)
            m_prev = m_ref[h]
            m_cur = jnp.broadcast_to(jnp.max(m_part, axis=1, keepdims=True), m_part.shape)
            m_new = jnp.maximum(m_prev, m_cur)
            alpha = jnp.exp2(m_prev - m_new)
            m_ref[h] = m_new
            p = jnp.concatenate([jnp.exp2(ch - m_new) for ch in chunks], axis=1).astype(BF16)
            v = v_ref[rows, h * V_HEAD:(h + 1) * V_HEAD]
            v1 = jnp.concatenate([v, jnp.ones_like(v)], axis=1)
            alpha2 = jnp.concatenate([alpha, alpha], axis=1)
            acc_ref[h] = alpha2 * acc_ref[h] + jnp.dot(p, v1, preferred_element_type=F32)

    def body(j, carry):
        block(j, False)
        return carry

    lax.fori_loop(0, i, body, 0)
    block(i, True)
    for h in range(heads):
        acc = acc_ref[h]
        o_ref[:, h * V_HEAD:(h + 1) * V_HEAD] = (acc[:, :V_HEAD] / acc[:, V_HEAD:]).astype(o_ref.dtype)


def _attention(q, k, v, batch, tq=512, heads=4, casts=()):
    t = q.shape[0]
    nq = SEQ // tq
    gh = MLA_HEADS // heads
    cast_specs, cast_shapes = _cast_specs(casts, batch * gh * nq, lambda b, h, i: (b * gh + h) * nq + i)
    out = pl.pallas_call(
        functools.partial(_attn_kernel, tq=tq, heads=heads, n_cast=len(casts)),
        out_shape=[jax.ShapeDtypeStruct((t, MLA_HEADS * V_HEAD), BF16)] + cast_shapes,
        grid=(batch, gh, nq),
        in_specs=[
            pl.BlockSpec((tq, heads * QK_PAD), lambda b, h, i: (b * nq + i, h)),
            pl.BlockSpec((SEQ, heads * QK_PAD), lambda b, h, i: (b, h)),
            pl.BlockSpec((SEQ, heads * V_HEAD), lambda b, h, i: (b, h)),
        ] + cast_specs,
        out_specs=[pl.BlockSpec((tq, heads * V_HEAD), lambda b, h, i: (b * nq + i, h))] + cast_specs,
        scratch_shapes=[pltpu.VMEM((heads, tq, LANES), F32), pltpu.VMEM((heads, tq, 2 * V_HEAD), F32)],
        compiler_params=_params("parallel", "parallel", "arbitrary"),
        name="attention",
    )(q, k, v, *casts)
    return out if casts else out[0]


def _branch_kernel(a_ref, wa_ref, b_ref, wb_ref, ga_ref, gb_ref, o_ref):
    ya = jnp.dot(a_ref[...], wa_ref[...], preferred_element_type=F32)
    yb = jnp.dot(b_ref[...], wb_ref[...], preferred_element_type=F32)
    o_ref[...] = (ga_ref[...].astype(F32) * ya + gb_ref[...].astype(F32) * yb).astype(o_ref.dtype)


def _branch(a, wa, b, wb, gates, tm=1024, tn=512):
    t = a.shape[0]
    nb = D_MODEL // tn
    return pl.pallas_call(
        _branch_kernel,
        out_shape=jax.ShapeDtypeStruct((t, D_MODEL), BF16),
        grid=(t // tm, nb),
        in_specs=[
            pl.BlockSpec((tm, a.shape[1]), lambda m, j: (m, 0)),
            pl.BlockSpec((wa.shape[0], tn), lambda m, j: (0, j)),
            pl.BlockSpec((tm, b.shape[1]), lambda m, j: (m, 0)),
            pl.BlockSpec((wb.shape[0], tn), lambda m, j: (0, j)),
            pl.BlockSpec((tm, tn), lambda m, j: (m, j)),
            pl.BlockSpec((tm, tn), lambda m, j: (m, nb + j)),
        ],
        out_specs=pl.BlockSpec((tm, tn), lambda m, j: (m, j)),
        compiler_params=_params("parallel", "parallel"),
        name="branch_merge",
    )(a, wa, b, wb, gates, gates)


def _mm_resid_kernel(x_ref, w_ref, res_ref, gate_ref, *refs, nk, n_cast):
    o_ref, acc_ref = refs[n_cast], refs[-1]
    _do_casts(refs[:-1], n_cast)
    kk = pl.program_id(2)
    if nk == 1:
        part = jnp.dot(x_ref[...], w_ref[...], preferred_element_type=F32)
        o_ref[...] = res_ref[...] + gate_ref[0] * part
        return

    @pl.when(kk == 0)
    def _():
        acc_ref[...] = jnp.zeros(acc_ref.shape, F32)

    acc_ref[...] += jnp.dot(x_ref[...], w_ref[...], preferred_element_type=F32)

    @pl.when(kk == nk - 1)
    def _():
        o_ref[...] = res_ref[...] + gate_ref[0] * acc_ref[...]


def _mm_resid(x, w, res, mod3, gate_idx, tm, tn, tk, name, casts=()):
    t, k = x.shape
    n = w.shape[1]
    gm, gn, nk = t // tm, n // tn, k // tk
    cast_specs, cast_shapes = _cast_specs(casts, gm * gn * nk, lambda m, j, kk: (m * gn + j) * nk + kk)
    out = pl.pallas_call(
        functools.partial(_mm_resid_kernel, nk=nk, n_cast=len(casts)),
        out_shape=[jax.ShapeDtypeStruct((t, n), F32)] + cast_shapes,
        grid=(gm, gn, nk),
        in_specs=[
            pl.BlockSpec((tm, tk), lambda m, j, kk: (m, kk)),
            pl.BlockSpec((tk, tn), lambda m, j, kk: (kk, j)),
            pl.BlockSpec((tm, tn), lambda m, j, kk: (m, j)),
            _mod_spec(gate_idx, tm, tn, ncol_arg=True),
        ] + cast_specs,
        out_specs=[pl.BlockSpec((tm, tn), lambda m, j, kk: (m, j))] + cast_specs,
        scratch_shapes=[pltpu.VMEM((tm, tn), F32)],
        compiler_params=_params("parallel", "parallel", "arbitrary"),
        name=name,
    )(x, w, res, mod3, *casts)
    return out if casts else out[0]


def _repack_kernel(wt_ref, o_ref, *, dup_tail):
    blk = wt_ref[...]
    if dup_tail:
        last = pl.program_id(0) == pl.num_programs(0) - 1
        half = blk.shape[0] // 2
        blk = jnp.where(last, jnp.concatenate([blk[:half], blk[:half]], axis=0), blk)
    o_ref[...] = blk.T.astype(BF16)


def _repack_w_in(w_in_t, start, width, cols, name, dup_tail=False):
    d = w_in_t.shape[1]
    return pl.pallas_call(
        functools.partial(_repack_kernel, dup_tail=dup_tail),
        out_shape=jax.ShapeDtypeStruct((d, width), BF16),
        grid=(width // cols,),
        in_specs=[pl.BlockSpec((pl.Element(cols), pl.Element(d)),
                               lambda i: (pl.multiple_of(start + i * cols, math.gcd(start, cols)), 0))],
        out_specs=pl.BlockSpec((d, cols), lambda i: (0, i)),
        compiler_params=_params("parallel"),
        name=name,
    )(w_in_t)


def _repack_uq_kernel(w_ref, o_ref):
    w = w_ref[...].astype(BF16)
    for h in range(MLA_HEADS):
        src, dst = h * QK_HEAD, h * QK_PAD
        o_ref[:, dst:dst + QK_HEAD] = w[:, src:src + QK_HEAD]
        o_ref[:, dst + QK_HEAD:dst + QK_PAD] = w[:, src + QK_NOPE:src + QK_HEAD]


def _repack_w_uq(w_uq, rows=128):
    d, n = w_uq.shape
    return pl.pallas_call(
        _repack_uq_kernel,
        out_shape=jax.ShapeDtypeStruct((d, MLA_HEADS * QK_PAD), BF16),
        grid=(d // rows,),
        in_specs=[pl.BlockSpec((rows, n), lambda r: (r, 0))],
        out_specs=pl.BlockSpec((rows, MLA_HEADS * QK_PAD), lambda r: (r, 0)),
        compiler_params=_params("parallel"),
        name="repack_w_uq",
    )(w_uq)


def _pad_cols(a, width):
    return jnp.pad(a, ((0, 0), (0, width - a.shape[1])))


def _layer(x2, batch, mod3, c_tab, s_tab, g_norm1, w_in, g_v, w_s, b_s, g_q_lat, g_kv_lat,
           w_uq, w_ukv, g_qn, g_kn, w_branch_a, w_branch_b, w_out, g_norm2, w_ff1, w_ff2):
    w_in_t = w_in.T
    w_uv = _repack_w_in(w_in_t, 0, OFF_Q, 512, "repack_uv")
    w_lat = _repack_w_in(w_in_t, OFF_Q, LAT_COLS, LANES, "repack_lat", dup_tail=True)
    w_gate = _repack_w_in(w_in_t, OFF_GATE, w_in.shape[1] - OFF_GATE, 512, "repack_gate")
    wq = _repack_w_uq(w_uq)
    bias_full = jnp.repeat(b_s.T, GM_GROUP_W, axis=1)
    row = lambda v: v.reshape(1, -1)
    twice = lambda v: jnp.concatenate([v, v]).reshape(1, -1)
    qg1, qg2 = row(g_qn[:QK_NOPE]), twice(g_qn[QK_NOPE:])
    kg1, kg2 = row(g_kn[:QK_NOPE]), twice(g_kn[QK_NOPE:])

    h1 = _normmod(x2, row(g_norm1), mod3, 0, 1)
    z, wkv = _mm_act(h1, w_uv, "gelu", BF16, 1024, 1024, "in_proj_uv", casts=(w_ukv,))
    lat = _mm_act(h1, w_lat, "none", F32, 512, LAT_COLS, "in_proj_lat")
    gates, wa, wb, wo = _mm_act(h1, w_gate, "sigmoid", BF16, 1024, 1024, "in_proj_gate",
                                casts=(w_branch_a, w_branch_b, w_out))
    gm = _gmlp(z, row(g_v), w_s, bias_full)
    q = _qprep(lat, row(g_q_lat), wq, qg1, qg2, c_tab, s_tab)
    k, v = _kvprep(lat, row(g_kv_lat), wkv, kg1, kg2, c_tab, s_tab)
    att, w1 = _attention(q, k, v, batch, casts=(w_ff1,))
    mixed = _branch(gm, wa, att, wb, gates)
    x2 = _mm_resid(mixed, wo, x2, mod3, 2, 1024, 512, D_MODEL, "out_proj")

    h2 = _normmod(x2, row(g_norm2), mod3, 3, 4)
    hid, w2 = _mm_act(h2, w1, "relu2", BF16, 1024, 1024, "ff1", casts=(w_ff2,))
    return _mm_resid(hid, w2, x2, mod3, 5, 1024, 1024, 2048, "ff2")


def kernel(x, c, positions, w_ada, b_ada, g_norm1, w_in, g_v, w_s, b_s, g_q_lat, g_kv_lat, w_uq, w_ukv,
           g_qn, g_kn, w_branch_a, w_branch_b, w_out, g_norm2, w_ff1, w_ff2):
    batch, seq, d = x.shape
    assert (seq, d) == (SEQ, D_MODEL)
    depth = w_ada.shape[0]
    t = batch * seq
    x2 = x.reshape(t, d)

    inv_freq = 1.0 / (ROPE_THETA ** (jnp.arange(0, QK_ROPE, 2, dtype=F32) / QK_ROPE))
    freq_row = _pad_cols(jnp.concatenate([inv_freq, inv_freq]).reshape(1, QK_ROPE), LANES)
    c_tab, s_tab = _rope_tables(positions.reshape(t, 1), freq_row)

    c8 = jnp.pad(c, ((0, 8 - batch), (0, 0)))
    for l in range(depth):
        mod = _ada(c8, w_ada[l], b_ada[l].reshape(1, -1))[:batch]
        mod3 = mod.reshape(batch * N_MOD, 1, d)
        x2 = _layer(x2, batch, mod3, c_tab, s_tab, g_norm1[l], w_in[l], g_v[l], w_s[l], b_s[l],
                    g_q_lat[l], g_kv_lat[l], w_uq[l], w_ukv[l], g_qn[l], g_kn[l], w_branch_a[l],
                    w_branch_b[l], w_out[l], g_norm2[l], w_ff1[l], w_ff2[l])
    return x2.reshape(batch, seq, d)
```

```python
import functools
import math

import jax
import jax.numpy as jnp
from jax import lax
from jax.experimental import pallas as pl
from jax.experimental.pallas import tpu as pltpu

D_MODEL = 4096
SEQ = 2048
GM_WIDTH = 2048
GM_GROUPS = 8
GM_GROUP_W = GM_WIDTH // GM_GROUPS
CHUNK = 128
MLA_HEADS = 32
QK_NOPE = 128
QK_ROPE = 64
QK_HEAD = QK_NOPE + QK_ROPE
V_HEAD = 128
Q_LORA = 1024
KV_LORA = 512
ROPE_THETA = 10000.0
D_FF = 4 * D_MODEL
N_MOD = 6
EPS = 1e-6
OFF_Q = 2 * GM_WIDTH
OFF_KV = OFF_Q + Q_LORA
OFF_KPE = OFF_KV + KV_LORA
OFF_GATE = OFF_KPE + QK_ROPE

LANES = 128
QK_PAD = 2 * LANES
LAT_COLS = Q_LORA + KV_LORA + LANES
VMEM_LIMIT = 56 * 1024 * 1024

F32 = jnp.float32
BF16 = jnp.bfloat16


def _params(*sem):
    return pltpu.CompilerParams(dimension_semantics=sem, vmem_limit_bytes=VMEM_LIMIT)


def _ada_kernel(c_ref, w_ref, b_ref, o_ref):
    c = c_ref[...]
    cond = c * jax.nn.sigmoid(c)
    acc = jnp.dot(cond.astype(BF16), w_ref[...].astype(BF16), preferred_element_type=F32)
    o_ref[...] = acc + b_ref[...]


def _ada(c8, w_ada, b_ada, tn=1024):
    n = w_ada.shape[1]
    return pl.pallas_call(
        _ada_kernel,
        out_shape=jax.ShapeDtypeStruct((8, n), F32),
        grid=(n // tn,),
        in_specs=[
            pl.BlockSpec((8, D_MODEL), lambda j: (0, 0)),
            pl.BlockSpec((D_MODEL, tn), lambda j: (0, j)),
            pl.BlockSpec((1, tn), lambda j: (0, j)),
        ],
        out_specs=pl.BlockSpec((8, tn), lambda j: (0, j)),
        compiler_params=_params("parallel"),
        name="ada",
    )(c8, w_ada, b_ada)


def _normmod_kernel(x_ref, g_ref, sh_ref, sc_ref, o_ref):
    x = x_ref[...]
    y = x * lax.rsqrt(jnp.mean(x * x, axis=-1, keepdims=True) + EPS)
    y = y * g_ref[...]
    o_ref[...] = (y * (1.0 + sc_ref[0]) + sh_ref[0]).astype(o_ref.dtype)


def _mod_spec(idx, tm, width, ncol_arg=False):
    per_batch = SEQ // tm
    if ncol_arg:
        return pl.BlockSpec((1, 1, width), lambda m, n, *_: ((m // per_batch) * N_MOD + idx, 0, n))
    return pl.BlockSpec((1, 1, width), lambda m, *_: ((m // per_batch) * N_MOD + idx, 0, 0))


def _normmod(x2, g, mod3, shift_idx, scale_idx, tm=512):
    t, d = x2.shape
    return pl.pallas_call(
        _normmod_kernel,
        out_shape=jax.ShapeDtypeStruct((t, d), BF16),
        grid=(t // tm,),
        in_specs=[
            pl.BlockSpec((tm, d), lambda m: (m, 0)),
            pl.BlockSpec((1, d), lambda m: (0, 0)),
            _mod_spec(shift_idx, tm, d),
            _mod_spec(scale_idx, tm, d),
        ],
        out_specs=pl.BlockSpec((tm, d), lambda m: (m, 0)),
        compiler_params=_params("parallel"),
        name="normmod",
    )(x2, g, mod3, mod3)


def _gelu(v):
    return 0.5 * v * (1.0 + lax.erf(v * math.sqrt(0.5)))


def _relu2(v):
    r = jnp.maximum(v, 0.0)
    return r * r


_ACTS = {"gelu": _gelu, "sigmoid": jax.nn.sigmoid, "relu2": _relu2, "none": lambda v: v}


def _cast_specs(jobs, n_steps, step_of):
    in_specs, out_specs, shapes, operands = [], [], [], []

    def rows_job(a):
        rows, cols = a.shape
        assert rows % (n_steps * 16) == 0
        spec = pl.BlockSpec((rows // n_steps, cols), lambda *ids: (step_of(*ids), 0))
        return a, spec, spec, a.shape

    def transposed_job(w_t, start, width):
        d, cols = w_t.shape[1], width // n_steps
        assert width % (n_steps * LANES) == 0
        src = pl.BlockSpec(
            (pl.Element(cols), pl.Element(d)),
            lambda *ids: (pl.multiple_of(start + step_of(*ids) * cols, math.gcd(start, cols)), 0))
        return w_t, src, pl.BlockSpec((d, cols), lambda *ids: (0, step_of(*ids))), (d, width)

    for job in jobs:
        a, src, dst, shape = transposed_job(*job) if isinstance(job, tuple) else rows_job(job)
        operands.append(a)
        in_specs.append(src)
        out_specs.append(dst)
        shapes.append(jax.ShapeDtypeStruct(shape, BF16))
    return in_specs, out_specs, shapes, operands


def _do_casts(refs, n):
    for src, dst in zip(refs[:n], refs[len(refs) - n:]):
        blk = src[...]
        dst[...] = (blk if src.shape == dst.shape else blk.T).astype(BF16)


def _mm_act_kernel(x_ref, w_ref, *refs, act, n_cast):
    o_ref = refs[n_cast]
    acc = jnp.dot(x_ref[...], w_ref[...], preferred_element_type=F32)
    o_ref[...] = _ACTS[act](acc).astype(o_ref.dtype)
    _do_casts(refs, n_cast)


def _mm_act(x, w, act, out_dtype, tm, tn, name, casts=()):
    t, k = x.shape
    n = w.shape[1]
    gm, gn = t // tm, n // tn
    cast_in, cast_out, cast_shapes, cast_ops = _cast_specs(casts, gm * gn, lambda m, j: m * gn + j)
    out = pl.pallas_call(
        functools.partial(_mm_act_kernel, act=act, n_cast=len(casts)),
        out_shape=[jax.ShapeDtypeStruct((t, n), out_dtype)] + cast_shapes,
        grid=(gm, gn),
        in_specs=[
            pl.BlockSpec((tm, k), lambda m, j: (m, 0)),
            pl.BlockSpec((k, tn), lambda m, j: (0, j)),
        ] + cast_in,
        out_specs=[pl.BlockSpec((tm, tn), lambda m, j: (m, j))] + cast_out,
        compiler_params=_params("parallel", "parallel"),
        name=name,
    )(x, w, *cast_ops)
    return out if casts else out[0]


def _gmlp_kernel(u_ref, v_ref, gv_ref, ws_ref, bias_ref, o_ref, *, n_chunks):
    row = lax.broadcasted_iota(jnp.int32, (CHUNK, CHUNK), 0)
    col = lax.broadcasted_iota(jnp.int32, (CHUNK, CHUNK), 1)
    causal = col <= row
    gv = gv_ref[...]
    for ci in range(n_chunks):
        rows = pl.ds(ci * CHUNK, CHUNK)
        v = v_ref[rows, :].astype(F32)
        vn = v * lax.rsqrt(jnp.mean(v * v, axis=-1, keepdims=True) + EPS) * gv
        vn = vn.astype(BF16)
        for g in range(GM_GROUPS):
            cols = slice(g * GM_GROUP_W, (g + 1) * GM_GROUP_W)
            w = jnp.where(causal, ws_ref[g], 0.0).astype(BF16)
            mixed = jnp.dot(w, vn[:, cols], preferred_element_type=F32) + bias_ref[:, cols]
            o_ref[rows, cols] = (u_ref[rows, cols].astype(F32) * mixed).astype(o_ref.dtype)


def _gmlp(z, g_v, w_s, bias_full, tm=512):
    t = z.shape[0]
    return pl.pallas_call(
        functools.partial(_gmlp_kernel, n_chunks=tm // CHUNK),
        out_shape=jax.ShapeDtypeStruct((t, GM_WIDTH), BF16),
        grid=(t // tm,),
        in_specs=[
            pl.BlockSpec((tm, GM_WIDTH), lambda m: (m, 0)),
            pl.BlockSpec((tm, GM_WIDTH), lambda m: (m, 1)),
            pl.BlockSpec((1, GM_WIDTH), lambda m: (0, 0)),
            pl.BlockSpec((GM_GROUPS, CHUNK, CHUNK), lambda m: (0, 0, 0)),
            pl.BlockSpec((CHUNK, GM_WIDTH), lambda m: (0, 0)),
        ],
        out_specs=pl.BlockSpec((tm, GM_WIDTH), lambda m: (m, 0)),
        compiler_params=_params("parallel"),
        name="gmlp",
    )(z, z, g_v, w_s, bias_full)


def _rope_tab_kernel(pos_ref, freq_ref, c_ref, s_ref):
    ang = pos_ref[...].astype(F32) * freq_ref[...]
    lane = lax.broadcasted_iota(jnp.int32, ang.shape, 1)
    cos = jnp.cos(ang)
    sin = jnp.sin(ang)
    c_ref[...] = jnp.where(lane < QK_ROPE, cos, 0.0)
    s_ref[...] = jnp.where(lane < QK_ROPE // 2, -sin, jnp.where(lane < QK_ROPE, sin, 0.0))


def _rope_tables(pos_col, freq_row, tm=1024):
    t = pos_col.shape[0]
    shp = jax.ShapeDtypeStruct((t, LANES), F32)
    return pl.pallas_call(
        _rope_tab_kernel,
        out_shape=(shp, shp),
        grid=(t // tm,),
        in_specs=[
            pl.BlockSpec((tm, 1), lambda m: (m, 0)),
            pl.BlockSpec((1, LANES), lambda m: (0, 0)),
        ],
        out_specs=tuple(pl.BlockSpec((tm, LANES), lambda m: (m, 0)) for _ in range(2)),
        compiler_params=_params("parallel"),
        name="rope_tables",
    )(pos_col, freq_row)


def _rope(v, c, s):
    return v * c + pltpu.roll(v, QK_ROPE // 2, 1) * s


def _rms_rows(x, g):
    return x * lax.rsqrt(jnp.mean(x * x, axis=-1, keepdims=True) + EPS) * g


def _qprep_kernel(lat_ref, glat_ref, w_ref, g1_ref, g2_ref, c_ref, s_ref, o_ref, xn_ref, *, heads):
    @pl.when(pl.program_id(1) == 0)
    def _():
        xn_ref[...] = _rms_rows(lat_ref[...], glat_ref[...]).astype(BF16)

    xn = xn_ref[...]
    c, s = c_ref[...], s_ref[...]
    g1, g2 = g1_ref[...], g2_ref[...]
    scale = QK_HEAD ** -0.5 * math.log2(math.e)
    for h in range(heads):
        acc = jnp.dot(xn, w_ref[:, h * QK_PAD:(h + 1) * QK_PAD], preferred_element_type=F32)
        a = acc[:, :LANES]
        b = acc[:, LANES:]
        ssq = jnp.sum(a * a + 0.5 * (b * b), axis=-1, keepdims=True)
        r = lax.rsqrt(ssq * (1.0 / QK_HEAD) + EPS) * scale
        o_ref[:, h * QK_PAD:h * QK_PAD + LANES] = (a * r * g1).astype(o_ref.dtype)
        o_ref[:, h * QK_PAD + LANES:(h + 1) * QK_PAD] = _rope(b * r * g2, c, s).astype(o_ref.dtype)


def _qprep(lat, g_q_lat, wq, g1, g2, c, s, tm=1024, heads=8):
    t = lat.shape[0]
    tn = heads * QK_PAD
    row = lambda m, j: (m, 0)
    const = lambda m, j: (0, 0)
    return pl.pallas_call(
        functools.partial(_qprep_kernel, heads=heads),
        out_shape=jax.ShapeDtypeStruct((t, MLA_HEADS * QK_PAD), BF16),
        grid=(t // tm, MLA_HEADS // heads),
        in_specs=[
            pl.BlockSpec((tm, Q_LORA), row),
            pl.BlockSpec((1, Q_LORA), const),
            pl.BlockSpec((Q_LORA, tn), lambda m, j: (0, j)),
            pl.BlockSpec((1, LANES), const),
            pl.BlockSpec((1, LANES), const),
            pl.BlockSpec((tm, LANES), row),
            pl.BlockSpec((tm, LANES), row),
        ],
        out_specs=pl.BlockSpec((tm, tn), lambda m, j: (m, j)),
        scratch_shapes=[pltpu.VMEM((tm, Q_LORA), BF16)],
        compiler_params=_params("parallel", "arbitrary"),
        name="q_prep",
    )(lat, g_q_lat, wq, g1, g2, c, s)


def _kvprep_kernel(lat_ref, kpe_ref, glat_ref, w_ref, g1_ref, g2_ref, c_ref, s_ref,
                   k_ref, v_ref, xn_ref, *, heads):
    @pl.when(pl.program_id(1) == 0)
    def _():
        xn_ref[...] = _rms_rows(lat_ref[...], glat_ref[...]).astype(BF16)

    xn = xn_ref[...]
    kpe = kpe_ref[...]
    ssq_pe = 0.5 * jnp.sum(kpe * kpe, axis=-1, keepdims=True)
    kr = _rope(kpe * g2_ref[...], c_ref[...], s_ref[...])
    g1 = g1_ref[...]
    width = QK_NOPE + V_HEAD
    for h in range(heads):
        acc = jnp.dot(xn, w_ref[:, h * width:(h + 1) * width], preferred_element_type=F32)
        a = acc[:, :QK_NOPE]
        ssq = jnp.sum(a * a, axis=-1, keepdims=True) + ssq_pe
        r = lax.rsqrt(ssq * (1.0 / QK_HEAD) + EPS)
        k_ref[:, h * QK_PAD:h * QK_PAD + LANES] = (a * r * g1).astype(k_ref.dtype)
        k_ref[:, h * QK_PAD + LANES:(h + 1) * QK_PAD] = (kr * r).astype(k_ref.dtype)
        v_ref[:, h * V_HEAD:(h + 1) * V_HEAD] = acc[:, QK_NOPE:].astype(v_ref.dtype)


def _kvprep(lat, g_kv_lat, wkv, g1, g2, c, s, tm=1024, heads=8):
    t = lat.shape[0]
    row = lambda m, j: (m, 0)
    const = lambda m, j: (0, 0)
    return pl.pallas_call(
        functools.partial(_kvprep_kernel, heads=heads),
        out_shape=(jax.ShapeDtypeStruct((t, MLA_HEADS * QK_PAD), BF16),
                   jax.ShapeDtypeStruct((t, MLA_HEADS * V_HEAD), BF16)),
        grid=(t // tm, MLA_HEADS // heads),
        in_specs=[
            pl.BlockSpec((tm, KV_LORA), lambda m, j: (m, Q_LORA // KV_LORA)),
            pl.BlockSpec((tm, LANES), lambda m, j: (m, (Q_LORA + KV_LORA) // LANES)),
            pl.BlockSpec((1, KV_LORA), const),
            pl.BlockSpec((KV_LORA, heads * (QK_NOPE + V_HEAD)), lambda m, j: (0, j)),
            pl.BlockSpec((1, LANES), const),
            pl.BlockSpec((1, LANES), const),
            pl.BlockSpec((tm, LANES), row),
            pl.BlockSpec((tm, LANES), row),
        ],
        out_specs=(pl.BlockSpec((tm, heads * QK_PAD), lambda m, j: (m, j)),
                   pl.BlockSpec((tm, heads * V_HEAD), lambda m, j: (m, j))),
        scratch_shapes=[pltpu.VMEM((tm, KV_LORA), BF16)],
        compiler_params=_params("parallel", "arbitrary"),
        name="kv_prep",
    )(lat, lat, g_kv_lat, wkv, g1, g2, c, s)


def _attn_kernel(q_ref, k_ref, v_ref, *refs, tq, tk, heads, n_cast):
    o_ref, m_ref, acc_ref = refs[n_cast], refs[-2], refs[-1]
    _do_casts(refs[:-2], n_cast)
    i = pl.program_id(2)
    nc = tk // LANES
    ratio = tq // tk
    m_ref[...] = jnp.full(m_ref.shape, -jnp.inf, F32)
    acc_ref[...] = jnp.zeros(acc_ref.shape, F32)

    def block(j, r0, masked):
        keys = pl.ds(pl.multiple_of(j * tk, tk), tk)
        nr = tq - r0
        for h in range(heads):
            q = q_ref[r0:, h * QK_PAD:(h + 1) * QK_PAD]
            k = k_ref[keys, h * QK_PAD:(h + 1) * QK_PAD]
            s = lax.dot_general(q, k, (((1,), (1,)), ((), ())), preferred_element_type=F32)
            if masked:
                row = lax.broadcasted_iota(jnp.int32, (nr, tk), 0)
                col = lax.broadcasted_iota(jnp.int32, (nr, tk), 1)
                s = jnp.where(col <= row, s, -jnp.inf)
            chunks = [s[:, c * LANES:(c + 1) * LANES] for c in range(nc)]
            m_part = functools.reduce(jnp.maximum, chunks)
            m_prev = m_ref[h, r0:, :]
            m_cur = jnp.broadcast_to(jnp.max(m_part, axis=1, keepdims=True), m_part.shape)
            m_new = jnp.maximum(m_prev, m_cur)
            alpha = jnp.exp2(m_prev - m_new)
            m_ref[h, r0:, :] = m_new
            p = jnp.concatenate([jnp.exp2(ch - m_new) for ch in chunks], axis=1).astype(BF16)
            v = v_ref[keys, h * V_HEAD:(h + 1) * V_HEAD]
            v1 = jnp.concatenate([v, jnp.ones_like(v)], axis=1)
            alpha2 = jnp.concatenate([alpha, alpha], axis=1)
            acc_ref[h, r0:, :] = alpha2 * acc_ref[h, r0:, :] + jnp.dot(p, v1, preferred_element_type=F32)

    def body(j, carry):
        block(j, 0, False)
        return carry

    lax.fori_loop(0, ratio * i, body, 0)
    for d in range(ratio):
        block(ratio * i + d, d * tk, True)
    for h in range(heads):
        acc = acc_ref[h]
        o_ref[:, h * V_HEAD:(h + 1) * V_HEAD] = (acc[:, :V_HEAD] / acc[:, V_HEAD:]).astype(o_ref.dtype)


def _attention(q, k, v, batch, tq=1024, tk=512, heads=4, casts=()):
    t = q.shape[0]
    nq = SEQ // tq
    gh = MLA_HEADS // heads
    cast_in, cast_out, cast_shapes, cast_ops = _cast_specs(
        casts, batch * gh * nq, lambda b, h, i: (b * gh + h) * nq + i)
    out = pl.pallas_call(
        functools.partial(_attn_kernel, tq=tq, tk=tk, heads=heads, n_cast=len(casts)),
        out_shape=[jax.ShapeDtypeStruct((t, MLA_HEADS * V_HEAD), BF16)] + cast_shapes,
        grid=(batch, gh, nq),
        in_specs=[
            pl.BlockSpec((tq, heads * QK_PAD), lambda b, h, i: (b * nq + i, h)),
            pl.BlockSpec((SEQ, heads * QK_PAD), lambda b, h, i: (b, h)),
            pl.BlockSpec((SEQ, heads * V_HEAD), lambda b, h, i: (b, h)),
        ] + cast_in,
        out_specs=[pl.BlockSpec((tq, heads * V_HEAD), lambda b, h, i: (b * nq + i, h))] + cast_out,
        scratch_shapes=[pltpu.VMEM((heads, tq, LANES), F32), pltpu.VMEM((heads, tq, 2 * V_HEAD), F32)],
        compiler_params=_params("parallel", "parallel", "arbitrary"),
        name="attention",
    )(q, k, v, *cast_ops)
    return out if casts else out[0]


def _branch_kernel(a_ref, wa_ref, b_ref, wb_ref, ga_ref, gb_ref, o_ref):
    ya = jnp.dot(a_ref[...], wa_ref[...], preferred_element_type=F32)
    yb = jnp.dot(b_ref[...], wb_ref[...], preferred_element_type=F32)
    o_ref[...] = (ga_ref[...].astype(F32) * ya + gb_ref[...].astype(F32) * yb).astype(o_ref.dtype)


def _branch(a, wa, b, wb, gates, tm=1024, tn=512):
    t = a.shape[0]
    nb = D_MODEL // tn
    return pl.pallas_call(
        _branch_kernel,
        out_shape=jax.ShapeDtypeStruct((t, D_MODEL), BF16),
        grid=(t // tm, nb),
        in_specs=[
            pl.BlockSpec((tm, a.shape[1]), lambda m, j: (m, 0)),
            pl.BlockSpec((wa.shape[0], tn), lambda m, j: (0, j)),
            pl.BlockSpec((tm, b.shape[1]), lambda m, j: (m, 0)),
            pl.BlockSpec((wb.shape[0], tn), lambda m, j: (0, j)),
            pl.BlockSpec((tm, tn), lambda m, j: (m, j)),
            pl.BlockSpec((tm, tn), lambda m, j: (m, nb + j)),
        ],
        out_specs=pl.BlockSpec((tm, tn), lambda m, j: (m, j)),
        compiler_params=_params("parallel", "parallel"),
        name="branch_merge",
    )(a, wa, b, wb, gates, gates)


def _mm_resid_kernel(x_ref, w_ref, res_ref, gate_ref, *refs, nk, n_cast):
    o_ref = refs[n_cast]
    _do_casts(refs, n_cast)
    kk = pl.program_id(2)
    if nk == 1:
        part = jnp.dot(x_ref[...], w_ref[...], preferred_element_type=F32)
        o_ref[...] = res_ref[...] + gate_ref[0] * part
        return

    @pl.when(kk == 0)
    def _():
        o_ref[...] = jnp.zeros(o_ref.shape, F32)

    o_ref[...] += jnp.dot(x_ref[...], w_ref[...], preferred_element_type=F32)

    @pl.when(kk == nk - 1)
    def _():
        o_ref[...] = res_ref[...] + gate_ref[0] * o_ref[...]


def _mm_resid(x, w, res, mod3, gate_idx, tm, tn, tk, name, casts=()):
    t, k = x.shape
    n = w.shape[1]
    gm, gn, nk = t // tm, n // tn, k // tk
    cast_in, cast_out, cast_shapes, cast_ops = _cast_specs(
        casts, gm * gn * nk, lambda m, j, kk: (m * gn + j) * nk + kk)
    out = pl.pallas_call(
        functools.partial(_mm_resid_kernel, nk=nk, n_cast=len(casts)),
        out_shape=[jax.ShapeDtypeStruct((t, n), F32)] + cast_shapes,
        grid=(gm, gn, nk),
        in_specs=[
            pl.BlockSpec((tm, tk), lambda m, j, kk: (m, kk)),
            pl.BlockSpec((tk, tn), lambda m, j, kk: (kk, j)),
            pl.BlockSpec((tm, tn), lambda m, j, kk: (m, j)),
            _mod_spec(gate_idx, tm, tn, ncol_arg=True),
        ] + cast_in,
        out_specs=[pl.BlockSpec((tm, tn), lambda m, j, kk: (m, j))] + cast_out,
        compiler_params=_params("parallel", "parallel", "arbitrary"),
        name=name,
    )(x, w, res, mod3, *cast_ops)
    return out if casts else out[0]


def _repack_kernel(wt_ref, o_ref, *, dup_tail):
    blk = wt_ref[...]
    if dup_tail:
        last = pl.program_id(0) == pl.num_programs(0) - 1
        half = blk.shape[0] // 2
        blk = jnp.where(last, jnp.concatenate([blk[:half], blk[:half]], axis=0), blk)
    o_ref[...] = blk.T.astype(BF16)


def _repack_w_in(w_in_t, start, width, cols, name, dup_tail=False):
    d = w_in_t.shape[1]
    return pl.pallas_call(
        functools.partial(_repack_kernel, dup_tail=dup_tail),
        out_shape=jax.ShapeDtypeStruct((d, width), BF16),
        grid=(width // cols,),
        in_specs=[pl.BlockSpec((pl.Element(cols), pl.Element(d)),
                               lambda i: (pl.multiple_of(start + i * cols, math.gcd(start, cols)), 0))],
        out_specs=pl.BlockSpec((d, cols), lambda i: (0, i)),
        compiler_params=_params("parallel"),
        name=name,
    )(w_in_t)


def _repack_uq_kernel(w_ref, o_ref):
    w = w_ref[...].astype(BF16)
    for h in range(MLA_HEADS):
        src, dst = h * QK_HEAD, h * QK_PAD
        o_ref[:, dst:dst + QK_HEAD] = w[:, src:src + QK_HEAD]
        o_ref[:, dst + QK_HEAD:dst + QK_PAD] = w[:, src + QK_NOPE:src + QK_HEAD]


def _repack_w_uq(w_uq, rows=128):
    d, n = w_uq.shape
    return pl.pallas_call(
        _repack_uq_kernel,
        out_shape=jax.ShapeDtypeStruct((d, MLA_HEADS * QK_PAD), BF16),
        grid=(d // rows,),
        in_specs=[pl.BlockSpec((rows, n), lambda r: (r, 0))],
        out_specs=pl.BlockSpec((rows, MLA_HEADS * QK_PAD), lambda r: (r, 0)),
        compiler_params=_params("parallel"),
        name="repack_w_uq",
    )(w_uq)


def _pad_cols(a, width):
    return jnp.pad(a, ((0, 0), (0, width - a.shape[1])))


def _layer(x2, batch, mod3, c_tab, s_tab, g_norm1, w_in, g_v, w_s, b_s, g_q_lat, g_kv_lat,
           w_uq, w_ukv, g_qn, g_kn, w_branch_a, w_branch_b, w_out, g_norm2, w_ff1, w_ff2):
    w_in_t = w_in.T
    w_uv = _repack_w_in(w_in_t, 0, OFF_Q, 512, "repack_uv")
    w_lat = _repack_w_in(w_in_t, OFF_Q, LAT_COLS, LANES, "repack_lat", dup_tail=True)
    wq = _repack_w_uq(w_uq)
    bias_full = jnp.repeat(b_s.T, GM_GROUP_W, axis=1)
    row = lambda v: v.reshape(1, -1)
    twice = lambda v: jnp.concatenate([v, v]).reshape(1, -1)
    qg1, qg2 = row(g_qn[:QK_NOPE]), twice(g_qn[QK_NOPE:])
    kg1, kg2 = row(g_kn[:QK_NOPE]), twice(g_kn[QK_NOPE:])

    h1 = _normmod(x2, row(g_norm1), mod3, 0, 1)
    z, w_gate = _mm_act(h1, w_uv, "gelu", BF16, 1024, 512, "in_proj_uv",
                        casts=((w_in_t, OFF_GATE, w_in.shape[1] - OFF_GATE),))
    lat, wkv = _mm_act(h1, w_lat, "none", F32, 512, LAT_COLS, "in_proj_lat", casts=(w_ukv,))
    gates, wa, wb, wo = _mm_act(h1, w_gate, "sigmoid", BF16, 1024, 1024, "in_proj_gate",
                                casts=(w_branch_a, w_branch_b, w_out))
    gm = _gmlp(z, row(g_v), w_s, bias_full)
    q = _qprep(lat, row(g_q_lat), wq, qg1, qg2, c_tab, s_tab)
    k, v = _kvprep(lat, row(g_kv_lat), wkv, kg1, kg2, c_tab, s_tab)
    att, w1 = _attention(q, k, v, batch, casts=(w_ff1,))
    mixed = _branch(gm, wa, att, wb, gates)
    x2 = _mm_resid(mixed, wo, x2, mod3, 2, 1024, 1024, D_MODEL, "out_proj")

    h2 = _normmod(x2, row(g_norm2), mod3, 3, 4)
    hid, w2 = _mm_act(h2, w1, "relu2", BF16, 1024, 1024, "ff1", casts=(w_ff2,))
    return _mm_resid(hid, w2, x2, mod3, 5, 1024, 1024, 4096, "ff2")


def kernel(x, c, positions, w_ada, b_ada, g_norm1, w_in, g_v, w_s, b_s, g_q_lat, g_kv_lat, w_uq, w_ukv,
           g_qn, g_kn, w_branch_a, w_branch_b, w_out, g_norm2, w_ff1, w_ff2):
    batch, seq, d = x.shape
    assert (seq, d) == (SEQ, D_MODEL)
    depth = w_ada.shape[0]
    t = batch * seq
    x2 = x.reshape(t, d)

    inv_freq = 1.0 / (ROPE_THETA ** (jnp.arange(0, QK_ROPE, 2, dtype=F32) / QK_ROPE))
    freq_row = _pad_cols(jnp.concatenate([inv_freq, inv_freq]).reshape(1, QK_ROPE), LANES)
    c_tab, s_tab = _rope_tables(positions.reshape(t, 1), freq_row)

    c8 = jnp.pad(c, ((0, 8 - batch), (0, 0)))
    for l in range(depth):
        mod = _ada(c8, w_ada[l], b_ada[l].reshape(1, -1))[:batch]
        mod3 = mod.reshape(batch * N_MOD, 1, d)
        x2 = _layer(x2, batch, mod3, c_tab, s_tab, g_norm1[l], w_in[l], g_v[l], w_s[l], b_s[l],
                    g_q_lat[l], g_kv_lat[l], w_uq[l], w_ukv[l], g_qn[l], g_kn[l], w_branch_a[l],
                    w_branch_b[l], w_out[l], g_norm2[l], w_ff1[l], w_ff2[l])
    return x2.reshape(batch, seq, d)
```

```python
import functools
import math

import jax
import jax.numpy as jnp
from jax import lax
from jax.experimental import pallas as pl
from jax.experimental.pallas import tpu as pltpu

D_MODEL = 4096
SEQ = 2048
GM_WIDTH = 2048
GM_GROUPS = 8
GM_GROUP_W = GM_WIDTH // GM_GROUPS
CHUNK = 128
MLA_HEADS = 32
QK_NOPE = 128
QK_ROPE = 64
QK_HEAD = QK_NOPE + QK_ROPE
V_HEAD = 128
Q_LORA = 1024
KV_LORA = 512
ROPE_THETA = 10000.0
D_FF = 4 * D_MODEL
N_MOD = 6
EPS = 1e-6
OFF_Q = 2 * GM_WIDTH
OFF_KV = OFF_Q + Q_LORA
OFF_KPE = OFF_KV + KV_LORA
OFF_GATE = OFF_KPE + QK_ROPE

LANES = 128
QK_PAD = 2 * LANES
LAT_COLS = Q_LORA + KV_LORA + LANES
VMEM_LIMIT = 56 * 1024 * 1024

F32 = jnp.float32
BF16 = jnp.bfloat16


def _params(*sem):
    return pltpu.CompilerParams(dimension_semantics=sem, vmem_limit_bytes=VMEM_LIMIT)


def _ada_kernel(c_ref, w_ref, b_ref, o_ref):
    c = c_ref[...]
    cond = c * jax.nn.sigmoid(c)
    acc = jnp.dot(cond.astype(BF16), w_ref[...].astype(BF16), preferred_element_type=F32)
    o_ref[...] = acc + b_ref[...]


def _ada(c8, w_ada, b_ada, tn=1024):
    n = w_ada.shape[1]
    return pl.pallas_call(
        _ada_kernel,
        out_shape=jax.ShapeDtypeStruct((8, n), F32),
        grid=(n // tn,),
        in_specs=[
            pl.BlockSpec((8, D_MODEL), lambda j: (0, 0)),
            pl.BlockSpec((D_MODEL, tn), lambda j: (0, j)),
            pl.BlockSpec((1, tn), lambda j: (0, j)),
        ],
        out_specs=pl.BlockSpec((8, tn), lambda j: (0, j)),
        compiler_params=_params("parallel"),
        name="ada",
    )(c8, w_ada, b_ada)


def _normmod_kernel(x_ref, g_ref, sh_ref, sc_ref, o_ref):
    x = x_ref[...]
    y = x * lax.rsqrt(jnp.mean(x * x, axis=-1, keepdims=True) + EPS)
    y = y * g_ref[...]
    o_ref[...] = (y * (1.0 + sc_ref[0]) + sh_ref[0]).astype(o_ref.dtype)


def _mod_spec(idx, tm, width, ncol_arg=False):
    per_batch = SEQ // tm
    if ncol_arg:
        return pl.BlockSpec((1, 1, width), lambda m, n, *_: ((m // per_batch) * N_MOD + idx, 0, n))
    return pl.BlockSpec((1, 1, width), lambda m, *_: ((m // per_batch) * N_MOD + idx, 0, 0))


def _normmod(x2, g, mod3, shift_idx, scale_idx, tm=512):
    t, d = x2.shape
    return pl.pallas_call(
        _normmod_kernel,
        out_shape=jax.ShapeDtypeStruct((t, d), BF16),
        grid=(t // tm,),
        in_specs=[
            pl.BlockSpec((tm, d), lambda m: (m, 0)),
            pl.BlockSpec((1, d), lambda m: (0, 0)),
            _mod_spec(shift_idx, tm, d),
            _mod_spec(scale_idx, tm, d),
        ],
        out_specs=pl.BlockSpec((tm, d), lambda m: (m, 0)),
        compiler_params=_params("parallel"),
        name="normmod",
    )(x2, g, mod3, mod3)


def _gelu(v):
    return 0.5 * v * (1.0 + lax.erf(v * math.sqrt(0.5)))


def _relu2(v):
    r = jnp.maximum(v, 0.0)
    return r * r


_ACTS = {"gelu": _gelu, "sigmoid": jax.nn.sigmoid, "relu2": _relu2, "none": lambda v: v}


def _cast_specs(jobs, n_steps, step_of):
    in_specs, out_specs, shapes, operands = [], [], [], []

    def rows_job(a):
        rows, cols = a.shape
        assert rows % (n_steps * 16) == 0
        spec = pl.BlockSpec((rows // n_steps, cols), lambda *ids: (step_of(*ids), 0))
        return a, spec, spec, a.shape

    def transposed_job(w_t, start, width):
        d, cols = w_t.shape[1], width // n_steps
        assert width % (n_steps * LANES) == 0
        src = pl.BlockSpec(
            (pl.Element(cols), pl.Element(d)),
            lambda *ids: (pl.multiple_of(start + step_of(*ids) * cols, math.gcd(start, cols)), 0))
        return w_t, src, pl.BlockSpec((d, cols), lambda *ids: (0, step_of(*ids))), (d, width)

    for job in jobs:
        a, src, dst, shape = transposed_job(*job) if isinstance(job, tuple) else rows_job(job)
        operands.append(a)
        in_specs.append(src)
        out_specs.append(dst)
        shapes.append(jax.ShapeDtypeStruct(shape, BF16))
    return in_specs, out_specs, shapes, operands


def _do_casts(refs, n):
    for src, dst in zip(refs[:n], refs[len(refs) - n:]):
        blk = src[...]
        dst[...] = (blk if src.shape == dst.shape else blk.T).astype(BF16)


def _mm_act_kernel(x_ref, w_ref, *refs, act, n_cast):
    o_ref = refs[n_cast]
    acc = jnp.dot(x_ref[...], w_ref[...], preferred_element_type=F32)
    o_ref[...] = _ACTS[act](acc).astype(o_ref.dtype)
    _do_casts(refs, n_cast)


def _mm_act(x, w, act, out_dtype, tm, tn, name, casts=()):
    t, k = x.shape
    n = w.shape[1]
    gm, gn = t // tm, n // tn
    cast_in, cast_out, cast_shapes, cast_ops = _cast_specs(casts, gm * gn, lambda m, j: m * gn + j)
    out = pl.pallas_call(
        functools.partial(_mm_act_kernel, act=act, n_cast=len(casts)),
        out_shape=[jax.ShapeDtypeStruct((t, n), out_dtype)] + cast_shapes,
        grid=(gm, gn),
        in_specs=[
            pl.BlockSpec((tm, k), lambda m, j: (m, 0)),
            pl.BlockSpec((k, tn), lambda m, j: (0, j)),
        ] + cast_in,
        out_specs=[pl.BlockSpec((tm, tn), lambda m, j: (m, j))] + cast_out,
        compiler_params=_params("parallel", "parallel"),
        name=name,
    )(x, w, *cast_ops)
    return out if casts else out[0]


def _gmlp_kernel(u_ref, v_ref, gv_ref, ws_ref, bias_ref, o_ref, *, n_chunks):
    row = lax.broadcasted_iota(jnp.int32, (CHUNK, CHUNK), 0)
    col = lax.broadcasted_iota(jnp.int32, (CHUNK, CHUNK), 1)
    causal = col <= row
    gv = gv_ref[...]
    for ci in range(n_chunks):
        rows = pl.ds(ci * CHUNK, CHUNK)
        v = v_ref[rows, :].astype(F32)
        vn = v * lax.rsqrt(jnp.mean(v * v, axis=-1, keepdims=True) + EPS) * gv
        vn = vn.astype(BF16)
        for g in range(GM_GROUPS):
            cols = slice(g * GM_GROUP_W, (g + 1) * GM_GROUP_W)
            w = jnp.where(causal, ws_ref[g], 0.0).astype(BF16)
            mixed = jnp.dot(w, vn[:, cols], preferred_element_type=F32) + bias_ref[:, cols]
            o_ref[rows, cols] = (u_ref[rows, cols].astype(F32) * mixed).astype(o_ref.dtype)


def _gmlp(z, g_v, w_s, bias_full, tm=512):
    t = z.shape[0]
    return pl.pallas_call(
        functools.partial(_gmlp_kernel, n_chunks=tm // CHUNK),
        out_shape=jax.ShapeDtypeStruct((t, GM_WIDTH), BF16),
        grid=(t // tm,),
        in_specs=[
            pl.BlockSpec((tm, GM_WIDTH), lambda m: (m, 0)),
            pl.BlockSpec((tm, GM_WIDTH), lambda m: (m, 1)),
            pl.BlockSpec((1, GM_WIDTH), lambda m: (0, 0)),
            pl.BlockSpec((GM_GROUPS, CHUNK, CHUNK), lambda m: (0, 0, 0)),
            pl.BlockSpec((CHUNK, GM_WIDTH), lambda m: (0, 0)),
        ],
        out_specs=pl.BlockSpec((tm, GM_WIDTH), lambda m: (m, 0)),
        compiler_params=_params("parallel"),
        name="gmlp",
    )(z, z, g_v, w_s, bias_full)


def _rope_tab_kernel(pos_ref, freq_ref, c_ref, s_ref):
    ang = pos_ref[...].astype(F32) * freq_ref[...]
    lane = lax.broadcasted_iota(jnp.int32, ang.shape, 1)
    cos = jnp.cos(ang)
    sin = jnp.sin(ang)
    c_ref[...] = jnp.where(lane < QK_ROPE, cos, 0.0)
    s_ref[...] = jnp.where(lane < QK_ROPE // 2, -sin, jnp.where(lane < QK_ROPE, sin, 0.0))


def _rope_tables(pos_col, freq_row, tm=1024):
    t = pos_col.shape[0]
    shp = jax.ShapeDtypeStruct((t, LANES), F32)
    return pl.pallas_call(
        _rope_tab_kernel,
        out_shape=(shp, shp),
        grid=(t // tm,),
        in_specs=[
            pl.BlockSpec((tm, 1), lambda m: (m, 0)),
            pl.BlockSpec((1, LANES), lambda m: (0, 0)),
        ],
        out_specs=tuple(pl.BlockSpec((tm, LANES), lambda m: (m, 0)) for _ in range(2)),
        compiler_params=_params("parallel"),
        name="rope_tables",
    )(pos_col, freq_row)


def _rope(v, c, s):
    return v * c + pltpu.roll(v, QK_ROPE // 2, 1) * s


def _rms_rows(x, g):
    return x * lax.rsqrt(jnp.mean(x * x, axis=-1, keepdims=True) + EPS) * g


def _qprep_kernel(lat_ref, glat_ref, w_ref, g1_ref, g2_ref, c_ref, s_ref, o_ref, xn_ref, *, heads):
    @pl.when(pl.program_id(1) == 0)
    def _():
        xn_ref[...] = _rms_rows(lat_ref[...], glat_ref[...]).astype(BF16)

    xn = xn_ref[...]
    c, s = c_ref[...], s_ref[...]
    g1, g2 = g1_ref[...], g2_ref[...]
    scale = QK_HEAD ** -0.5 * math.log2(math.e)
    for h in range(heads):
        acc = jnp.dot(xn, w_ref[:, h * QK_PAD:(h + 1) * QK_PAD], preferred_element_type=F32)
        a = acc[:, :LANES]
        b = acc[:, LANES:]
        ssq = jnp.sum(a * a + 0.5 * (b * b), axis=-1, keepdims=True)
        r = lax.rsqrt(ssq * (1.0 / QK_HEAD) + EPS) * scale
        o_ref[:, h * QK_PAD:h * QK_PAD + LANES] = (a * r * g1).astype(o_ref.dtype)
        o_ref[:, h * QK_PAD + LANES:(h + 1) * QK_PAD] = _rope(b * r * g2, c, s).astype(o_ref.dtype)


def _qprep(lat, g_q_lat, wq, g1, g2, c, s, tm=1024, heads=16):
    t = lat.shape[0]
    tn = heads * QK_PAD
    row = lambda m, j: (m, 0)
    const = lambda m, j: (0, 0)
    return pl.pallas_call(
        functools.partial(_qprep_kernel, heads=heads),
        out_shape=jax.ShapeDtypeStruct((t, MLA_HEADS * QK_PAD), BF16),
        grid=(t // tm, MLA_HEADS // heads),
        in_specs=[
            pl.BlockSpec((tm, Q_LORA), row),
            pl.BlockSpec((1, Q_LORA), const),
            pl.BlockSpec((Q_LORA, tn), lambda m, j: (0, j)),
            pl.BlockSpec((1, LANES), const),
            pl.BlockSpec((1, LANES), const),
            pl.BlockSpec((tm, LANES), row),
            pl.BlockSpec((tm, LANES), row),
        ],
        out_specs=pl.BlockSpec((tm, tn), lambda m, j: (m, j)),
        scratch_shapes=[pltpu.VMEM((tm, Q_LORA), BF16)],
        compiler_params=_params("parallel", "arbitrary"),
        name="q_prep",
    )(lat, g_q_lat, wq, g1, g2, c, s)


def _kvprep_kernel(lat_ref, kpe_ref, glat_ref, w_ref, g1_ref, g2_ref, c_ref, s_ref,
                   k_ref, v_ref, xn_ref, *, heads):
    @pl.when(pl.program_id(1) == 0)
    def _():
        xn_ref[...] = _rms_rows(lat_ref[...], glat_ref[...]).astype(BF16)

    xn = xn_ref[...]
    kpe = kpe_ref[...]
    ssq_pe = 0.5 * jnp.sum(kpe * kpe, axis=-1, keepdims=True)
    kr = _rope(kpe * g2_ref[...], c_ref[...], s_ref[...])
    g1 = g1_ref[...]
    width = QK_NOPE + V_HEAD
    for h in range(heads):
        acc = jnp.dot(xn, w_ref[:, h * width:(h + 1) * width], preferred_element_type=F32)
        a = acc[:, :QK_NOPE]
        ssq = jnp.sum(a * a, axis=-1, keepdims=True) + ssq_pe
        r = lax.rsqrt(ssq * (1.0 / QK_HEAD) + EPS)
        k_ref[:, h * QK_PAD:h * QK_PAD + LANES] = (a * r * g1).astype(k_ref.dtype)
        k_ref[:, h * QK_PAD + LANES:(h + 1) * QK_PAD] = (kr * r).astype(k_ref.dtype)
        v_ref[:, h * V_HEAD:(h + 1) * V_HEAD] = acc[:, QK_NOPE:].astype(v_ref.dtype)


def _kvprep(lat, g_kv_lat, wkv, g1, g2, c, s, tm=1024, heads=16):
    t = lat.shape[0]
    row = lambda m, j: (m, 0)
    const = lambda m, j: (0, 0)
    return pl.pallas_call(
        functools.partial(_kvprep_kernel, heads=heads),
        out_shape=(jax.ShapeDtypeStruct((t, MLA_HEADS * QK_PAD), BF16),
                   jax.ShapeDtypeStruct((t, MLA_HEADS * V_HEAD), BF16)),
        grid=(t // tm, MLA_HEADS // heads),
        in_specs=[
            pl.BlockSpec((tm, KV_LORA), lambda m, j: (m, Q_LORA // KV_LORA)),
            pl.BlockSpec((tm, LANES), lambda m, j: (m, (Q_LORA + KV_LORA) // LANES)),
            pl.BlockSpec((1, KV_LORA), const),
            pl.BlockSpec((KV_LORA, heads * (QK_NOPE + V_HEAD)), lambda m, j: (0, j)),
            pl.BlockSpec((1, LANES), const),
            pl.BlockSpec((1, LANES), const),
            pl.BlockSpec((tm, LANES), row),
            pl.BlockSpec((tm, LANES), row),
        ],
        out_specs=(pl.BlockSpec((tm, heads * QK_PAD), lambda m, j: (m, j)),
                   pl.BlockSpec((tm, heads * V_HEAD), lambda m, j: (m, j))),
        scratch_shapes=[pltpu.VMEM((tm, KV_LORA), BF16)],
        compiler_params=_params("parallel", "arbitrary"),
        name="kv_prep",
    )(lat, lat, g_kv_lat, wkv, g1, g2, c, s)


def _attn_kernel(q_ref, k_ref, v_ref, *refs, tq, tk, heads, n_cast):
    o_ref, m_ref, acc_ref = refs[n_cast], refs[-2], refs[-1]
    _do_casts(refs[:-2], n_cast)
    i = pl.program_id(2)
    nc = tk // LANES
    ratio = tq // tk
    m_ref[...] = jnp.full(m_ref.shape, -jnp.inf, F32)
    acc_ref[...] = jnp.zeros(acc_ref.shape, F32)

    def block(j, r0, masked):
        keys = pl.ds(pl.multiple_of(j * tk, tk), tk)
        nr = tq - r0
        for h in range(heads):
            q = q_ref[r0:, h * QK_PAD:(h + 1) * QK_PAD]
            k = k_ref[keys, h * QK_PAD:(h + 1) * QK_PAD]
            s = lax.dot_general(q, k, (((1,), (1,)), ((), ())), preferred_element_type=F32)
            if masked:
                row = lax.broadcasted_iota(jnp.int32, (nr, tk), 0)
                col = lax.broadcasted_iota(jnp.int32, (nr, tk), 1)
                s = jnp.where(col <= row, s, -jnp.inf)
            chunks = [s[:, c * LANES:(c + 1) * LANES] for c in range(nc)]
            m_part = functools.reduce(jnp.maximum, chunks)
            m_prev = m_ref[h, r0:, :]
            m_cur = jnp.broadcast_to(jnp.max(m_part, axis=1, keepdims=True), m_part.shape)
            m_new = jnp.maximum(m_prev, m_cur)
            alpha = jnp.exp2(m_prev - m_new)
            m_ref[h, r0:, :] = m_new
            p = jnp.concatenate([jnp.exp2(ch - m_new) for ch in chunks], axis=1).astype(BF16)
            v = v_ref[keys, h * V_HEAD:(h + 1) * V_HEAD]
            v1 = jnp.concatenate([v, jnp.ones_like(v)], axis=1)
            alpha2 = jnp.concatenate([alpha, alpha], axis=1)
            acc_ref[h, r0:, :] = alpha2 * acc_ref[h, r0:, :] + jnp.dot(p, v1, preferred_element_type=F32)

    def body(j, carry):
        block(j, 0, False)
        return carry

    if tq < SEQ:
        lax.fori_loop(0, ratio * i, body, 0)
    for d in range(ratio):
        block(ratio * i + d, d * tk, True)
    for h in range(heads):
        acc = acc_ref[h]
        o_ref[:, h * V_HEAD:(h + 1) * V_HEAD] = (acc[:, :V_HEAD] / acc[:, V_HEAD:]).astype(o_ref.dtype)


def _attention(q, k, v, batch, tq=2048, tk=512, heads=2, casts=()):
    t = q.shape[0]
    nq = SEQ // tq
    gh = MLA_HEADS // heads
    cast_in, cast_out, cast_shapes, cast_ops = _cast_specs(
        casts, batch * gh * nq, lambda b, h, i: (b * gh + h) * nq + i)
    out = pl.pallas_call(
        functools.partial(_attn_kernel, tq=tq, tk=tk, heads=heads, n_cast=len(casts)),
        out_shape=[jax.ShapeDtypeStruct((t, MLA_HEADS * V_HEAD), BF16)] + cast_shapes,
        grid=(batch, gh, nq),
        in_specs=[
            pl.BlockSpec((tq, heads * QK_PAD), lambda b, h, i: (b * nq + i, h)),
            pl.BlockSpec((SEQ, heads * QK_PAD), lambda b, h, i: (b, h)),
            pl.BlockSpec((SEQ, heads * V_HEAD), lambda b, h, i: (b, h)),
        ] + cast_in,
        out_specs=[pl.BlockSpec((tq, heads * V_HEAD), lambda b, h, i: (b * nq + i, h))] + cast_out,
        scratch_shapes=[pltpu.VMEM((heads, tq, LANES), F32), pltpu.VMEM((heads, tq, 2 * V_HEAD), F32)],
        compiler_params=_params("parallel", "parallel", "arbitrary"),
        name="attention",
    )(q, k, v, *cast_ops)
    return out if casts else out[0]


def _branch_kernel(a_ref, wa_ref, b_ref, wb_ref, ga_ref, gb_ref, o_ref):
    ya = jnp.dot(a_ref[...], wa_ref[...], preferred_element_type=F32)
    yb = jnp.dot(b_ref[...], wb_ref[...], preferred_element_type=F32)
    o_ref[...] = (ga_ref[...].astype(F32) * ya + gb_ref[...].astype(F32) * yb).astype(o_ref.dtype)


def _branch(a, wa, b, wb, gates, tm=1024, tn=512):
    t = a.shape[0]
    nb = D_MODEL // tn
    return pl.pallas_call(
        _branch_kernel,
        out_shape=jax.ShapeDtypeStruct((t, D_MODEL), BF16),
        grid=(t // tm, nb),
        in_specs=[
            pl.BlockSpec((tm, a.shape[1]), lambda m, j: (m, 0)),
            pl.BlockSpec((wa.shape[0], tn), lambda m, j: (0, j)),
            pl.BlockSpec((tm, b.shape[1]), lambda m, j: (m, 0)),
            pl.BlockSpec((wb.shape[0], tn), lambda m, j: (0, j)),
            pl.BlockSpec((tm, tn), lambda m, j: (m, j)),
            pl.BlockSpec((tm, tn), lambda m, j: (m, nb + j)),
        ],
        out_specs=pl.BlockSpec((tm, tn), lambda m, j: (m, j)),
        compiler_params=_params("parallel", "parallel"),
        name="branch_merge",
    )(a, wa, b, wb, gates, gates)


def _mm_resid_kernel(x_ref, w_ref, res_ref, gate_ref, *refs, nk, n_cast):
    o_ref = refs[n_cast]
    _do_casts(refs, n_cast)
    kk = pl.program_id(2)
    if nk == 1:
        part = jnp.dot(x_ref[...], w_ref[...], preferred_element_type=F32)
        o_ref[...] = res_ref[...] + gate_ref[0] * part
        return

    @pl.when(kk == 0)
    def _():
        o_ref[...] = jnp.zeros(o_ref.shape, F32)

    o_ref[...] += jnp.dot(x_ref[...], w_ref[...], preferred_element_type=F32)

    @pl.when(kk == nk - 1)
    def _():
        o_ref[...] = res_ref[...] + gate_ref[0] * o_ref[...]


def _mm_resid(x, w, res, mod3, gate_idx, tm, tn, tk, name, casts=()):
    t, k = x.shape
    n = w.shape[1]
    gm, gn, nk = t // tm, n // tn, k // tk
    cast_in, cast_out, cast_shapes, cast_ops = _cast_specs(
        casts, gm * gn * nk, lambda m, j, kk: (m * gn + j) * nk + kk)
    out = pl.pallas_call(
        functools.partial(_mm_resid_kernel, nk=nk, n_cast=len(casts)),
        out_shape=[jax.ShapeDtypeStruct((t, n), F32)] + cast_shapes,
        grid=(gm, gn, nk),
        in_specs=[
            pl.BlockSpec((tm, tk), lambda m, j, kk: (m, kk)),
            pl.BlockSpec((tk, tn), lambda m, j, kk: (kk, j)),
            pl.BlockSpec((tm, tn), lambda m, j, kk: (m, j)),
            _mod_spec(gate_idx, tm, tn, ncol_arg=True),
        ] + cast_in,
        out_specs=[pl.BlockSpec((tm, tn), lambda m, j, kk: (m, j))] + cast_out,
        compiler_params=_params("parallel", "parallel", "arbitrary"),
        name=name,
    )(x, w, res, mod3, *cast_ops)
    return out if casts else out[0]


def _repack_kernel(wt_ref, o_ref, *, dup_tail):
    blk = wt_ref[...]
    if dup_tail:
        last = pl.program_id(0) == pl.num_programs(0) - 1
        half = blk.shape[0] // 2
        blk = jnp.where(last, jnp.concatenate([blk[:half], blk[:half]], axis=0), blk)
    o_ref[...] = blk.T.astype(BF16)


def _repack_w_in(w_in_t, start, width, cols, name, dup_tail=False):
    d = w_in_t.shape[1]
    return pl.pallas_call(
        functools.partial(_repack_kernel, dup_tail=dup_tail),
        out_shape=jax.ShapeDtypeStruct((d, width), BF16),
        grid=(width // cols,),
        in_specs=[pl.BlockSpec((pl.Element(cols), pl.Element(d)),
                               lambda i: (pl.multiple_of(start + i * cols, math.gcd(start, cols)), 0))],
        out_specs=pl.BlockSpec((d, cols), lambda i: (0, i)),
        compiler_params=_params("parallel"),
        name=name,
    )(w_in_t)


def _repack_uq_kernel(w_ref, o_ref):
    w = w_ref[...].astype(BF16)
    for h in range(MLA_HEADS):
        src, dst = h * QK_HEAD, h * QK_PAD
        o_ref[:, dst:dst + QK_HEAD] = w[:, src:src + QK_HEAD]
        o_ref[:, dst + QK_HEAD:dst + QK_PAD] = w[:, src + QK_NOPE:src + QK_HEAD]


def _repack_w_uq(w_uq, rows=128):
    d, n = w_uq.shape
    return pl.pallas_call(
        _repack_uq_kernel,
        out_shape=jax.ShapeDtypeStruct((d, MLA_HEADS * QK_PAD), BF16),
        grid=(d // rows,),
        in_specs=[pl.BlockSpec((rows, n), lambda r: (r, 0))],
        out_specs=pl.BlockSpec((rows, MLA_HEADS * QK_PAD), lambda r: (r, 0)),
        compiler_params=_params("parallel"),
        name="repack_w_uq",
    )(w_uq)


def _pad_cols(a, width):
    return jnp.pad(a, ((0, 0), (0, width - a.shape[1])))


def _layer(x2, batch, mod3, c_tab, s_tab, g_norm1, w_in, g_v, w_s, b_s, g_q_lat, g_kv_lat,
           w_uq, w_ukv, g_qn, g_kn, w_branch_a, w_branch_b, w_out, g_norm2, w_ff1, w_ff2):
    w_in_t = w_in.T
    w_uv = _repack_w_in(w_in_t, 0, OFF_Q, 512, "repack_uv")
    w_lat = _repack_w_in(w_in_t, OFF_Q, LAT_COLS, LANES, "repack_lat", dup_tail=True)
    wq = _repack_w_uq(w_uq)
    bias_full = jnp.repeat(b_s.T, GM_GROUP_W, axis=1)
    row = lambda v: v.reshape(1, -1)
    twice = lambda v: jnp.concatenate([v, v]).reshape(1, -1)
    qg1, qg2 = row(g_qn[:QK_NOPE]), twice(g_qn[QK_NOPE:])
    kg1, kg2 = row(g_kn[:QK_NOPE]), twice(g_kn[QK_NOPE:])

    h1 = _normmod(x2, row(g_norm1), mod3, 0, 1)
    z, w_gate = _mm_act(h1, w_uv, "gelu", BF16, 1024, 512, "in_proj_uv",
                        casts=((w_in_t, OFF_GATE, w_in.shape[1] - OFF_GATE),))
    lat, wkv = _mm_act(h1, w_lat, "none", F32, 512, LAT_COLS, "in_proj_lat", casts=(w_ukv,))
    gates, wa, wb, wo = _mm_act(h1, w_gate, "sigmoid", BF16, 1024, 1024, "in_proj_gate",
                                casts=(w_branch_a, w_branch_b, w_out))
    gm = _gmlp(z, row(g_v), w_s, bias_full)
    q = _qprep(lat, row(g_q_lat), wq, qg1, qg2, c_tab, s_tab)
    k, v = _kvprep(lat, row(g_kv_lat), wkv, kg1, kg2, c_tab, s_tab)
    att, w1 = _attention(q, k, v, batch, casts=(w_ff1,))
    mixed = _branch(gm, wa, att, wb, gates)
    x2 = _mm_resid(mixed, wo, x2, mod3, 2, 1024, 1024, D_MODEL, "out_proj")

    h2 = _normmod(x2, row(g_norm2), mod3, 3, 4)
    hid, w2 = _mm_act(h2, w1, "relu2", BF16, 1024, 1024, "ff1", casts=(w_ff2,))
    return _mm_resid(hid, w2, x2, mod3, 5, 1024, 1024, 4096, "ff2")


def kernel(x, c, positions, w_ada, b_ada, g_norm1, w_in, g_v, w_s, b_s, g_q_lat, g_kv_lat, w_uq, w_ukv,
           g_qn, g_kn, w_branch_a, w_branch_b, w_out, g_norm2, w_ff1, w_ff2):
    batch, seq, d = x.shape
    assert (seq, d) == (SEQ, D_MODEL)
    depth = w_ada.shape[0]
    t = batch * seq
    x2 = x.reshape(t, d)

    inv_freq = 1.0 / (ROPE_THETA ** (jnp.arange(0, QK_ROPE, 2, dtype=F32) / QK_ROPE))
    freq_row = _pad_cols(jnp.concatenate([inv_freq, inv_freq]).reshape(1, QK_ROPE), LANES)
    c_tab, s_tab = _rope_tables(positions.reshape(t, 1), freq_row)

    c8 = jnp.pad(c, ((0, 8 - batch), (0, 0)))
    for l in range(depth):
        mod = _ada(c8, w_ada[l], b_ada[l].reshape(1, -1))[:batch]
        mod3 = mod.reshape(batch * N_MOD, 1, d)
        x2 = _layer(x2, batch, mod3, c_tab, s_tab, g_norm1[l], w_in[l], g_v[l], w_s[l], b_s[l],
                    g_q_lat[l], g_kv_lat[l], w_uq[l], w_ukv[l], g_qn[l], g_kn[l], w_branch_a[l],
                    w_branch_b[l], w_out[l], g_norm2[l], w_ff1[l], w_ff2[l])
    return x2.reshape(batch, seq, d)
```

```python
import functools
import math

import jax
import jax.numpy as jnp
from jax import lax
from jax.experimental import pallas as pl
from jax.experimental.pallas import tpu as pltpu

D_MODEL = 4096
SEQ = 2048
GM_WIDTH = 2048
GM_GROUPS = 8
GM_GROUP_W = GM_WIDTH // GM_GROUPS
CHUNK = 128
MLA_HEADS = 32
QK_NOPE = 128
QK_ROPE = 64
QK_HEAD = QK_NOPE + QK_ROPE
V_HEAD = 128
Q_LORA = 1024
KV_LORA = 512
ROPE_THETA = 10000.0
D_FF = 4 * D_MODEL
N_MOD = 6
EPS = 1e-6
OFF_Q = 2 * GM_WIDTH
OFF_KV = OFF_Q + Q_LORA
OFF_KPE = OFF_KV + KV_LORA
OFF_GATE = OFF_KPE + QK_ROPE

LANES = 128
QK_PAD = 2 * LANES
LAT_COLS = Q_LORA + KV_LORA + LANES
VMEM_LIMIT = 56 * 1024 * 1024

F32 = jnp.float32
BF16 = jnp.bfloat16


def _params(*sem):
    return pltpu.CompilerParams(dimension_semantics=sem, vmem_limit_bytes=VMEM_LIMIT)


def _ada_kernel(c_ref, w_ref, b_ref, o_ref):
    c = c_ref[...]
    cond = c * jax.nn.sigmoid(c)
    acc = jnp.dot(cond.astype(BF16), w_ref[...].astype(BF16), preferred_element_type=F32)
    o_ref[...] = acc + b_ref[...]


def _ada(c8, w_ada, b_ada, tn=1024):
    n = w_ada.shape[1]
    return pl.pallas_call(
        _ada_kernel,
        out_shape=jax.ShapeDtypeStruct((8, n), F32),
        grid=(n // tn,),
        in_specs=[
            pl.BlockSpec((8, D_MODEL), lambda j: (0, 0)),
            pl.BlockSpec((D_MODEL, tn), lambda j: (0, j)),
            pl.BlockSpec((1, tn), lambda j: (0, j)),
        ],
        out_specs=pl.BlockSpec((8, tn), lambda j: (0, j)),
        compiler_params=_params("parallel"),
        name="ada",
    )(c8, w_ada, b_ada)


def _normmod_kernel(x_ref, g_ref, sh_ref, sc_ref, *refs):
    o_ref = refs[-1] if len(refs) == 1 else refs[2]
    x = x_ref[...]
    y = x * lax.rsqrt(jnp.mean(x * x, axis=-1, keepdims=True) + EPS)
    y = y * g_ref[...]
    o_ref[...] = (y * (1.0 + sc_ref[0]) + sh_ref[0]).astype(o_ref.dtype)
    if len(refs) > 1:
        pos_ref, freq_ref, _, c_ref, s_ref = refs
        _rope_tab_kernel(pos_ref, freq_ref, c_ref, s_ref)


def _mod_spec(idx, tm, width, ncol_arg=False):
    per_batch = SEQ // tm
    if ncol_arg:
        return pl.BlockSpec((1, 1, width), lambda m, n, *_: ((m // per_batch) * N_MOD + idx, 0, n))
    return pl.BlockSpec((1, 1, width), lambda m, *_: ((m // per_batch) * N_MOD + idx, 0, 0))


def _normmod(x2, g, mod3, shift_idx, scale_idx, tm=512, rope_in=None):
    t, d = x2.shape
    rows = lambda width: pl.BlockSpec((tm, width), lambda m: (m, 0))
    in_specs = [rows(d), pl.BlockSpec((1, d), lambda m: (0, 0)),
                _mod_spec(shift_idx, tm, d), _mod_spec(scale_idx, tm, d)]
    out_shape, out_specs, extra = [jax.ShapeDtypeStruct((t, d), BF16)], [rows(d)], ()
    if rope_in is not None:
        extra = rope_in
        in_specs += [rows(1), pl.BlockSpec((1, LANES), lambda m: (0, 0))]
        out_shape += [jax.ShapeDtypeStruct((t, LANES), F32)] * 2
        out_specs += [rows(LANES)] * 2
    out = pl.pallas_call(
        _normmod_kernel,
        out_shape=out_shape,
        grid=(t // tm,),
        in_specs=in_specs,
        out_specs=out_specs,
        compiler_params=_params("parallel"),
        name="normmod",
    )(x2, g, mod3, mod3, *extra)
    return out if rope_in is not None else out[0]


def _gelu(v):
    return 0.5 * v * (1.0 + lax.erf(v * math.sqrt(0.5)))


def _relu2(v):
    r = jnp.maximum(v, 0.0)
    return r * r


def _sigmoid(v):
    return 0.5 + 0.5 * jnp.tanh(0.5 * v)


_ACTS = {"gelu": _gelu, "sigmoid": _sigmoid, "relu2": _relu2, "none": lambda v: v}


def _cast_specs(jobs, n_steps, step_of):
    in_specs, out_specs, shapes, operands = [], [], [], []

    def rows_job(a):
        rows, cols = a.shape
        assert rows % (n_steps * 16) == 0
        spec = pl.BlockSpec((rows // n_steps, cols), lambda *ids: (step_of(*ids), 0))
        return a, spec, spec, a.shape

    def transposed_job(w_t, start, width):
        d, cols = w_t.shape[1], width // n_steps
        assert width % (n_steps * LANES) == 0
        src = pl.BlockSpec(
            (pl.Element(cols), pl.Element(d)),
            lambda *ids: (pl.multiple_of(start + step_of(*ids) * cols, math.gcd(start, cols)), 0))
        return w_t, src, pl.BlockSpec((d, cols), lambda *ids: (0, step_of(*ids))), (d, width)

    def widen_heads_job(w_uq):
        rows = w_uq.shape[0] // n_steps
        assert w_uq.shape[0] % (n_steps * 16) == 0
        index = lambda *ids: (step_of(*ids), 0)
        wide = MLA_HEADS * QK_PAD
        return (w_uq, pl.BlockSpec((rows, w_uq.shape[1]), index), pl.BlockSpec((rows, wide), index),
                (w_uq.shape[0], wide))

    for job in jobs:
        if not isinstance(job, tuple):
            a, src, dst, shape = rows_job(job)
        elif isinstance(job[0], str):
            a, src, dst, shape = widen_heads_job(job[1])
        else:
            a, src, dst, shape = transposed_job(*job)
        operands.append(a)
        in_specs.append(src)
        out_specs.append(dst)
        shapes.append(jax.ShapeDtypeStruct(shape, BF16))
    return in_specs, out_specs, shapes, operands


def _widen_heads(w, o_ref):
    for h in range(MLA_HEADS):
        src, dst = h * QK_HEAD, h * QK_PAD
        o_ref[:, dst:dst + QK_HEAD] = w[:, src:src + QK_HEAD]
        o_ref[:, dst + QK_HEAD:dst + QK_PAD] = w[:, src + QK_NOPE:src + QK_HEAD]


def _do_casts(refs, n):
    for src, dst in zip(refs[:n], refs[len(refs) - n:]):
        blk = src[...]
        if src.shape == dst.shape:
            dst[...] = blk.astype(BF16)
        elif src.shape[0] == dst.shape[0]:
            _widen_heads(blk.astype(BF16), dst)
        else:
            dst[...] = blk.T.astype(BF16)


def _mm_act_kernel(x_ref, w_ref, *refs, act, n_cast):
    o_ref = refs[n_cast]
    acc = jnp.dot(x_ref[...], w_ref[...], preferred_element_type=F32)
    o_ref[...] = _ACTS[act](acc).astype(o_ref.dtype)
    _do_casts(refs, n_cast)


def _mm_act(x, w, act, out_dtype, tm, tn, name, casts=()):
    t, k = x.shape
    n = w.shape[1]
    gm, gn = t // tm, n // tn
    cast_in, cast_out, cast_shapes, cast_ops = _cast_specs(casts, gm * gn, lambda m, j: m * gn + j)
    out = pl.pallas_call(
        functools.partial(_mm_act_kernel, act=act, n_cast=len(casts)),
        out_shape=[jax.ShapeDtypeStruct((t, n), out_dtype)] + cast_shapes,
        grid=(gm, gn),
        in_specs=[
            pl.BlockSpec((tm, k), lambda m, j: (m, 0)),
            pl.BlockSpec((k, tn), lambda m, j: (0, j)),
        ] + cast_in,
        out_specs=[pl.BlockSpec((tm, tn), lambda m, j: (m, j))] + cast_out,
        compiler_params=_params("parallel", "parallel"),
        name=name,
    )(x, w, *cast_ops)
    return out if casts else out[0]


def _gmlp_kernel(u_ref, v_ref, gv_ref, ws_ref, bias_ref, o_ref, *, n_chunks):
    row = lax.broadcasted_iota(jnp.int32, (CHUNK, CHUNK), 0)
    col = lax.broadcasted_iota(jnp.int32, (CHUNK, CHUNK), 1)
    causal = col <= row
    gv = gv_ref[...]
    for ci in range(n_chunks):
        rows = pl.ds(ci * CHUNK, CHUNK)
        v = v_ref[rows, :].astype(F32)
        vn = v * lax.rsqrt(jnp.mean(v * v, axis=-1, keepdims=True) + EPS) * gv
        vn = vn.astype(BF16)
        for g in range(GM_GROUPS):
            cols = slice(g * GM_GROUP_W, (g + 1) * GM_GROUP_W)
            w = jnp.where(causal, ws_ref[g], 0.0).astype(BF16)
            mixed = jnp.dot(w, vn[:, cols], preferred_element_type=F32) + bias_ref[:, cols]
            o_ref[rows, cols] = (u_ref[rows, cols].astype(F32) * mixed).astype(o_ref.dtype)


def _gmlp(z, g_v, w_s, bias_full, tm=512):
    t = z.shape[0]
    return pl.pallas_call(
        functools.partial(_gmlp_kernel, n_chunks=tm // CHUNK),
        out_shape=jax.ShapeDtypeStruct((t, GM_WIDTH), BF16),
        grid=(t // tm,),
        in_specs=[
            pl.BlockSpec((tm, GM_WIDTH), lambda m: (m, 0)),
            pl.BlockSpec((tm, GM_WIDTH), lambda m: (m, 1)),
            pl.BlockSpec((1, GM_WIDTH), lambda m: (0, 0)),
            pl.BlockSpec((GM_GROUPS, CHUNK, CHUNK), lambda m: (0, 0, 0)),
            pl.BlockSpec((CHUNK, GM_WIDTH), lambda m: (0, 0)),
        ],
        out_specs=pl.BlockSpec((tm, GM_WIDTH), lambda m: (m, 0)),
        compiler_params=_params("parallel"),
        name="gmlp",
    )(z, z, g_v, w_s, bias_full)


def _rope_tab_kernel(pos_ref, freq_ref, c_ref, s_ref):
    ang = pos_ref[...].astype(F32) * freq_ref[...]
    lane = lax.broadcasted_iota(jnp.int32, ang.shape, 1)
    cos = jnp.cos(ang)
    sin = jnp.sin(ang)
    c_ref[...] = jnp.where(lane < QK_ROPE, cos, 0.0)
    s_ref[...] = jnp.where(lane < QK_ROPE // 2, -sin, jnp.where(lane < QK_ROPE, sin, 0.0))


def _rope(v, c, s):
    return v * c + pltpu.roll(v, QK_ROPE // 2, 1) * s


def _rms_rows(x, g):
    return x * lax.rsqrt(jnp.mean(x * x, axis=-1, keepdims=True) + EPS) * g


def _qprep_kernel(lat_ref, glat_ref, w_ref, g1_ref, g2_ref, c_ref, s_ref, o_ref, xn_ref, *, heads):
    @pl.when(pl.program_id(1) == 0)
    def _():
        xn_ref[...] = _rms_rows(lat_ref[...], glat_ref[...]).astype(BF16)

    xn = xn_ref[...]
    c, s = c_ref[...], s_ref[...]
    g1, g2 = g1_ref[...], g2_ref[...]
    scale = QK_HEAD ** -0.5 * math.log2(math.e)
    for h in range(heads):
        acc = jnp.dot(xn, w_ref[:, h * QK_PAD:(h + 1) * QK_PAD], preferred_element_type=F32)
        a = acc[:, :LANES]
        b = acc[:, LANES:]
        ssq = jnp.sum(a * a + 0.5 * (b * b), axis=-1, keepdims=True)
        r = lax.rsqrt(ssq * (1.0 / QK_HEAD) + EPS) * scale
        o_ref[:, h * QK_PAD:h * QK_PAD + LANES] = (a * r * g1).astype(o_ref.dtype)
        o_ref[:, h * QK_PAD + LANES:(h + 1) * QK_PAD] = _rope(b * r * g2, c, s).astype(o_ref.dtype)


def _qprep(lat, g_q_lat, wq, g1, g2, c, s, tm=1024, heads=16):
    t = lat.shape[0]
    tn = heads * QK_PAD
    row = lambda m, j: (m, 0)
    const = lambda m, j: (0, 0)
    return pl.pallas_call(
        functools.partial(_qprep_kernel, heads=heads),
        out_shape=jax.ShapeDtypeStruct((t, MLA_HEADS * QK_PAD), BF16),
        grid=(t // tm, MLA_HEADS // heads),
        in_specs=[
            pl.BlockSpec((tm, Q_LORA), row),
            pl.BlockSpec((1, Q_LORA), const),
            pl.BlockSpec((Q_LORA, tn), lambda m, j: (0, j)),
            pl.BlockSpec((1, LANES), const),
            pl.BlockSpec((1, LANES), const),
            pl.BlockSpec((tm, LANES), row),
            pl.BlockSpec((tm, LANES), row),
        ],
        out_specs=pl.BlockSpec((tm, tn), lambda m, j: (m, j)),
        scratch_shapes=[pltpu.VMEM((tm, Q_LORA), BF16)],
        compiler_params=_params("parallel", "arbitrary"),
        name="q_prep",
    )(lat, g_q_lat, wq, g1, g2, c, s)


def _kvprep_kernel(lat_ref, kpe_ref, glat_ref, w_ref, g1_ref, g2_ref, c_ref, s_ref,
                   k_ref, v_ref, xn_ref, *, heads):
    @pl.when(pl.program_id(1) == 0)
    def _():
        xn_ref[...] = _rms_rows(lat_ref[...], glat_ref[...]).astype(BF16)

    xn = xn_ref[...]
    kpe = kpe_ref[...]
    ssq_pe = 0.5 * jnp.sum(kpe * kpe, axis=-1, keepdims=True)
    kr = _rope(kpe * g2_ref[...], c_ref[...], s_ref[...])
    g1 = g1_ref[...]
    width = QK_NOPE + V_HEAD
    for h in range(heads):
        acc = jnp.dot(xn, w_ref[:, h * width:(h + 1) * width], preferred_element_type=F32)
        a = acc[:, :QK_NOPE]
        ssq = jnp.sum(a * a, axis=-1, keepdims=True) + ssq_pe
        r = lax.rsqrt(ssq * (1.0 / QK_HEAD) + EPS)
        k_ref[:, h * QK_PAD:h * QK_PAD + LANES] = (a * r * g1).astype(k_ref.dtype)
        k_ref[:, h * QK_PAD + LANES:(h + 1) * QK_PAD] = (kr * r).astype(k_ref.dtype)
        v_ref[:, h * V_HEAD:(h + 1) * V_HEAD] = acc[:, QK_NOPE:].astype(v_ref.dtype)


def _kvprep(lat, g_kv_lat, wkv, g1, g2, c, s, tm=1024, heads=16):
    t = lat.shape[0]
    row = lambda m, j: (m, 0)
    const = lambda m, j: (0, 0)
    return pl.pallas_call(
        functools.partial(_kvprep_kernel, heads=heads),
        out_shape=(jax.ShapeDtypeStruct((t, MLA_HEADS * QK_PAD), BF16),
                   jax.ShapeDtypeStruct((t, MLA_HEADS * V_HEAD), BF16)),
        grid=(t // tm, MLA_HEADS // heads),
        in_specs=[
            pl.BlockSpec((tm, KV_LORA), lambda m, j: (m, Q_LORA // KV_LORA)),
            pl.BlockSpec((tm, LANES), lambda m, j: (m, (Q_LORA + KV_LORA) // LANES)),
            pl.BlockSpec((1, KV_LORA), const),
            pl.BlockSpec((KV_LORA, heads * (QK_NOPE + V_HEAD)), lambda m, j: (0, j)),
            pl.BlockSpec((1, LANES), const),
            pl.BlockSpec((1, LANES), const),
            pl.BlockSpec((tm, LANES), row),
            pl.BlockSpec((tm, LANES), row),
        ],
        out_specs=(pl.BlockSpec((tm, heads * QK_PAD), lambda m, j: (m, j)),
                   pl.BlockSpec((tm, heads * V_HEAD), lambda m, j: (m, j))),
        scratch_shapes=[pltpu.VMEM((tm, KV_LORA), BF16)],
        compiler_params=_params("parallel", "arbitrary"),
        name="kv_prep",
    )(lat, lat, g_kv_lat, wkv, g1, g2, c, s)


def _attn_kernel(q_ref, k_ref, v_ref, *refs, tq, tk, heads, n_cast):
    o_ref, m_ref, acc_ref = refs[n_cast], refs[-2], refs[-1]
    _do_casts(refs[:-2], n_cast)
    i = pl.program_id(2)
    nc = tk // LANES
    ratio = tq // tk
    m_ref[...] = jnp.full(m_ref.shape, -jnp.inf, F32)
    acc_ref[...] = jnp.zeros(acc_ref.shape, F32)

    def block(j, r0, masked):
        keys = pl.ds(pl.multiple_of(j * tk, tk), tk)
        nr = tq - r0
        for h in range(heads):
            q = q_ref[r0:, h * QK_PAD:(h + 1) * QK_PAD]
            k = k_ref[keys, h * QK_PAD:(h + 1) * QK_PAD]
            s = lax.dot_general(q, k, (((1,), (1,)), ((), ())), preferred_element_type=F32)
            if masked:
                row = lax.broadcasted_iota(jnp.int32, (nr, tk), 0)
                col = lax.broadcasted_iota(jnp.int32, (nr, tk), 1)
                s = jnp.where(col <= row, s, -jnp.inf)
            chunks = [s[:, c * LANES:(c + 1) * LANES] for c in range(nc)]
            m_part = functools.reduce(jnp.maximum, chunks)
            m_prev = m_ref[h, r0:, :]
            m_cur = jnp.broadcast_to(jnp.max(m_part, axis=1, keepdims=True), m_part.shape)
            m_new = jnp.maximum(m_prev, m_cur)
            alpha = jnp.exp2(m_prev - m_new)
            m_ref[h, r0:, :] = m_new
            p = jnp.concatenate([jnp.exp2(ch - m_new) for ch in chunks], axis=1).astype(BF16)
            v = v_ref[keys, h * V_HEAD:(h + 1) * V_HEAD]
            v1 = jnp.concatenate([v, jnp.ones_like(v)], axis=1)
            alpha2 = jnp.concatenate([alpha, alpha], axis=1)
            acc_ref[h, r0:, :] = alpha2 * acc_ref[h, r0:, :] + jnp.dot(p, v1, preferred_element_type=F32)

    def body(j, carry):
        block(j, 0, False)
        return carry

    if tq < SEQ:
        lax.fori_loop(0, ratio * i, body, 0)
    for d in range(ratio):
        block(ratio * i + d, d * tk, True)
    for h in range(heads):
        acc = acc_ref[h]
        o_ref[:, h * V_HEAD:(h + 1) * V_HEAD] = (acc[:, :V_HEAD] / acc[:, V_HEAD:]).astype(o_ref.dtype)


def _attention(q, k, v, batch, tq=2048, tk=512, heads=2, casts=()):
    t = q.shape[0]
    nq = SEQ // tq
    gh = MLA_HEADS // heads
    cast_in, cast_out, cast_shapes, cast_ops = _cast_specs(
        casts, batch * gh * nq, lambda b, h, i: (b * gh + h) * nq + i)
    out = pl.pallas_call(
        functools.partial(_attn_kernel, tq=tq, tk=tk, heads=heads, n_cast=len(casts)),
        out_shape=[jax.ShapeDtypeStruct((t, MLA_HEADS * V_HEAD), BF16)] + cast_shapes,
        grid=(batch, gh, nq),
        in_specs=[
            pl.BlockSpec((tq, heads * QK_PAD), lambda b, h, i: (b * nq + i, h)),
            pl.BlockSpec((SEQ, heads * QK_PAD), lambda b, h, i: (b, h)),
            pl.BlockSpec((SEQ, heads * V_HEAD), lambda b, h, i: (b, h)),
        ] + cast_in,
        out_specs=[pl.BlockSpec((tq, heads * V_HEAD), lambda b, h, i: (b * nq + i, h))] + cast_out,
        scratch_shapes=[pltpu.VMEM((heads, tq, LANES), F32), pltpu.VMEM((heads, tq, 2 * V_HEAD), F32)],
        compiler_params=_params("parallel", "parallel", "arbitrary"),
        name="attention",
    )(q, k, v, *cast_ops)
    return out if casts else out[0]


def _branch_kernel(a_ref, wa_ref, b_ref, wb_ref, ga_ref, gb_ref, o_ref):
    ya = jnp.dot(a_ref[...], wa_ref[...], preferred_element_type=F32)
    yb = jnp.dot(b_ref[...], wb_ref[...], preferred_element_type=F32)
    o_ref[...] = (ga_ref[...].astype(F32) * ya + gb_ref[...].astype(F32) * yb).astype(o_ref.dtype)


def _branch(a, wa, b, wb, gates, tm=1024, tn=512):
    t = a.shape[0]
    nb = D_MODEL // tn
    return pl.pallas_call(
        _branch_kernel,
        out_shape=jax.ShapeDtypeStruct((t, D_MODEL), BF16),
        grid=(t // tm, nb),
        in_specs=[
            pl.BlockSpec((tm, a.shape[1]), lambda m, j: (m, 0)),
            pl.BlockSpec((wa.shape[0], tn), lambda m, j: (0, j)),
            pl.BlockSpec((tm, b.shape[1]), lambda m, j: (m, 0)),
            pl.BlockSpec((wb.shape[0], tn), lambda m, j: (0, j)),
            pl.BlockSpec((tm, tn), lambda m, j: (m, j)),
            pl.BlockSpec((tm, tn), lambda m, j: (m, nb + j)),
        ],
        out_specs=pl.BlockSpec((tm, tn), lambda m, j: (m, j)),
        compiler_params=_params("parallel", "parallel"),
        name="branch_merge",
    )(a, wa, b, wb, gates, gates)


def _mm_resid_kernel(x_ref, w_ref, res_ref, gate_ref, *refs, nk, n_cast):
    o_ref = refs[n_cast]
    _do_casts(refs, n_cast)
    kk = pl.program_id(2)
    if nk == 1:
        part = jnp.dot(x_ref[...], w_ref[...], preferred_element_type=F32)
        o_ref[...] = res_ref[...] + gate_ref[0] * part
        return

    @pl.when(kk == 0)
    def _():
        o_ref[...] = jnp.zeros(o_ref.shape, F32)

    o_ref[...] += jnp.dot(x_ref[...], w_ref[...], preferred_element_type=F32)

    @pl.when(kk == nk - 1)
    def _():
        o_ref[...] = res_ref[...] + gate_ref[0] * o_ref[...]


def _mm_resid(x, w, res, mod3, gate_idx, tm, tn, tk, name, casts=()):
    t, k = x.shape
    n = w.shape[1]
    gm, gn, nk = t // tm, n // tn, k // tk
    cast_in, cast_out, cast_shapes, cast_ops = _cast_specs(
        casts, gm * gn * nk, lambda m, j, kk: (m * gn + j) * nk + kk)
    out = pl.pallas_call(
        functools.partial(_mm_resid_kernel, nk=nk, n_cast=len(casts)),
        out_shape=[jax.ShapeDtypeStruct((t, n), F32)] + cast_shapes,
        grid=(gm, gn, nk),
        in_specs=[
            pl.BlockSpec((tm, tk), lambda m, j, kk: (m, kk)),
            pl.BlockSpec((tk, tn), lambda m, j, kk: (kk, j)),
            pl.BlockSpec((tm, tn), lambda m, j, kk: (m, j)),
            _mod_spec(gate_idx, tm, tn, ncol_arg=True),
        ] + cast_in,
        out_specs=[pl.BlockSpec((tm, tn), lambda m, j, kk: (m, j))] + cast_out,
        compiler_params=_params("parallel", "parallel", "arbitrary"),
        name=name,
    )(x, w, res, mod3, *cast_ops)
    return out if casts else out[0]


def _repack_kernel(wt_ref, o_ref, *, dup_tail):
    blk = wt_ref[...]
    if dup_tail:
        last = pl.program_id(0) == pl.num_programs(0) - 1
        half = blk.shape[0] // 2
        blk = jnp.where(last, jnp.concatenate([blk[:half], blk[:half]], axis=0), blk)
    o_ref[...] = blk.T.astype(BF16)


def _repack_w_in(w_in_t, start, width, cols, name, dup_tail=False):
    d = w_in_t.shape[1]
    return pl.pallas_call(
        functools.partial(_repack_kernel, dup_tail=dup_tail),
        out_shape=jax.ShapeDtypeStruct((d, width), BF16),
        grid=(width // cols,),
        in_specs=[pl.BlockSpec((pl.Element(cols), pl.Element(d)),
                               lambda i: (pl.multiple_of(start + i * cols, math.gcd(start, cols)), 0))],
        out_specs=pl.BlockSpec((d, cols), lambda i: (0, i)),
        compiler_params=_params("parallel"),
        name=name,
    )(w_in_t)


def _pad_cols(a, width):
    return jnp.pad(a, ((0, 0), (0, width - a.shape[1])))


def _layer(x2, batch, mod3, rope_in, tables, g_norm1, w_in, g_v, w_s, b_s, g_q_lat, g_kv_lat,
           w_uq, w_ukv, g_qn, g_kn, w_branch_a, w_branch_b, w_out, g_norm2, w_ff1, w_ff2):
    w_in_t = w_in.T
    w_uv = _repack_w_in(w_in_t, 0, OFF_Q, 512, "repack_uv")
    w_lat = _repack_w_in(w_in_t, OFF_Q, LAT_COLS, LANES, "repack_lat", dup_tail=True)
    bias_full = jnp.repeat(b_s.T, GM_GROUP_W, axis=1)
    row = lambda v: v.reshape(1, -1)
    twice = lambda v: jnp.concatenate([v, v]).reshape(1, -1)
    qg1, qg2 = row(g_qn[:QK_NOPE]), twice(g_qn[QK_NOPE:])
    kg1, kg2 = row(g_kn[:QK_NOPE]), twice(g_kn[QK_NOPE:])

    if tables is None:
        h1, *tables = _normmod(x2, row(g_norm1), mod3, 0, 1, rope_in=rope_in)
    else:
        h1 = _normmod(x2, row(g_norm1), mod3, 0, 1)
    c_tab, s_tab = tables
    z, w_gate, wq = _mm_act(h1, w_uv, "gelu", BF16, 1024, 512, "in_proj_uv",
                            casts=((w_in_t, OFF_GATE, w_in.shape[1] - OFF_GATE), ("widen_heads", w_uq)))
    lat, wkv = _mm_act(h1, w_lat, "none", F32, 512, LAT_COLS, "in_proj_lat", casts=(w_ukv,))
    gates, wa, wb, wo = _mm_act(h1, w_gate, "sigmoid", BF16, 1024, 1024, "in_proj_gate",
                                casts=(w_branch_a, w_branch_b, w_out))
    gm = _gmlp(z, row(g_v), w_s, bias_full)
    q = _qprep(lat, row(g_q_lat), wq, qg1, qg2, c_tab, s_tab)
    k, v = _kvprep(lat, row(g_kv_lat), wkv, kg1, kg2, c_tab, s_tab)
    att, w1 = _attention(q, k, v, batch, casts=(w_ff1,))
    mixed = _branch(gm, wa, att, wb, gates)
    x2 = _mm_resid(mixed, wo, x2, mod3, 2, 1024, 1024, D_MODEL, "out_proj")

    h2 = _normmod(x2, row(g_norm2), mod3, 3, 4)
    hid, w2 = _mm_act(h2, w1, "relu2", BF16, 1024, 1024, "ff1", casts=(w_ff2,))
    return _mm_resid(hid, w2, x2, mod3, 5, 1024, 1024, 4096, "ff2"), tables


def kernel(x, c, positions, w_ada, b_ada, g_norm1, w_in, g_v, w_s, b_s, g_q_lat, g_kv_lat, w_uq, w_ukv,
           g_qn, g_kn, w_branch_a, w_branch_b, w_out, g_norm2, w_ff1, w_ff2):
    batch, seq, d = x.shape
    assert (seq, d) == (SEQ, D_MODEL)
    depth = w_ada.shape[0]
    t = batch * seq
    x2 = x.reshape(t, d)

    inv_freq = 1.0 / (ROPE_THETA ** (jnp.arange(0, QK_ROPE, 2, dtype=F32) / QK_ROPE))
    freq_row = _pad_cols(jnp.concatenate([inv_freq, inv_freq]).reshape(1, QK_ROPE), LANES)
    rope_in = (positions.reshape(t, 1), freq_row)

    c8 = jnp.pad(c, ((0, 8 - batch), (0, 0)))
    tables = None
    for l in range(depth):
        mod = _ada(c8, w_ada[l], b_ada[l].reshape(1, -1))[:batch]
        mod3 = mod.reshape(batch * N_MOD, 1, d)
        x2, tables = _layer(x2, batch, mod3, rope_in, tables, g_norm1[l], w_in[l], g_v[l], w_s[l], b_s[l],
                            g_q_lat[l], g_kv_lat[l], w_uq[l], w_ukv[l], g_qn[l], g_kn[l], w_branch_a[l],
                            w_branch_b[l], w_out[l], g_norm2[l], w_ff1[l], w_ff2[l])
    return x2.reshape(batch, seq, d)
```

```python
import functools
import math

import jax
import jax.numpy as jnp
from jax import lax
from jax.experimental import pallas as pl
from jax.experimental.pallas import tpu as pltpu

D_MODEL = 4096
SEQ = 2048
GM_WIDTH = 2048
GM_GROUPS = 8
GM_GROUP_W = GM_WIDTH // GM_GROUPS
CHUNK = 128
MLA_HEADS = 32
QK_NOPE = 128
QK_ROPE = 64
QK_HEAD = QK_NOPE + QK_ROPE
V_HEAD = 128
Q_LORA = 1024
KV_LORA = 512
ROPE_THETA = 10000.0
D_FF = 4 * D_MODEL
N_MOD = 6
EPS = 1e-6
OFF_Q = 2 * GM_WIDTH
OFF_KV = OFF_Q + Q_LORA
OFF_KPE = OFF_KV + KV_LORA
OFF_GATE = OFF_KPE + QK_ROPE

LANES = 128
QK_PAD = 2 * LANES
LAT_COLS = Q_LORA + KV_LORA + LANES
VMEM_LIMIT = 56 * 1024 * 1024

F32 = jnp.float32
BF16 = jnp.bfloat16


def _params(*sem):
    return pltpu.CompilerParams(dimension_semantics=sem, vmem_limit_bytes=VMEM_LIMIT)


def _ada_kernel(c_ref, w_ref, b_ref, o_ref):
    c = c_ref[...]
    cond = c * jax.nn.sigmoid(c)
    acc = jnp.dot(cond.astype(BF16), w_ref[...].astype(BF16), preferred_element_type=F32)
    o_ref[...] = acc + b_ref[...]


def _ada(c8, w_ada, b_ada, tn=1024):
    n = w_ada.shape[1]
    return pl.pallas_call(
        _ada_kernel,
        out_shape=jax.ShapeDtypeStruct((8, n), F32),
        grid=(n // tn,),
        in_specs=[
            pl.BlockSpec((8, D_MODEL), lambda j: (0, 0)),
            pl.BlockSpec((D_MODEL, tn), lambda j: (0, j)),
            pl.BlockSpec((1, tn), lambda j: (0, j)),
        ],
        out_specs=pl.BlockSpec((8, tn), lambda j: (0, j)),
        compiler_params=_params("parallel"),
        name="ada",
    )(c8, w_ada, b_ada)


def _normmod_kernel(x_ref, g_ref, sh_ref, sc_ref, *refs):
    o_ref = refs[-1] if len(refs) == 1 else refs[2]
    x = x_ref[...]
    y = x * lax.rsqrt(jnp.mean(x * x, axis=-1, keepdims=True) + EPS)
    y = y * g_ref[...]
    o_ref[...] = (y * (1.0 + sc_ref[0]) + sh_ref[0]).astype(o_ref.dtype)
    if len(refs) > 1:
        pos_ref, freq_ref, _, c_ref, s_ref = refs
        _rope_tab_kernel(pos_ref, freq_ref, c_ref, s_ref)


def _mod_spec(idx, tm, width, ncol_arg=False):
    per_batch = SEQ // tm
    if ncol_arg:
        return pl.BlockSpec((1, 1, width), lambda m, n, *_: ((m // per_batch) * N_MOD + idx, 0, n))
    return pl.BlockSpec((1, 1, width), lambda m, *_: ((m // per_batch) * N_MOD + idx, 0, 0))


def _normmod(x2, g, mod3, shift_idx, scale_idx, tm=512, rope_in=None):
    t, d = x2.shape
    rows = lambda width: pl.BlockSpec((tm, width), lambda m: (m, 0))
    in_specs = [rows(d), pl.BlockSpec((1, d), lambda m: (0, 0)),
                _mod_spec(shift_idx, tm, d), _mod_spec(scale_idx, tm, d)]
    out_shape, out_specs, extra = [jax.ShapeDtypeStruct((t, d), BF16)], [rows(d)], ()
    if rope_in is not None:
        extra = rope_in
        in_specs += [rows(1), pl.BlockSpec((1, LANES), lambda m: (0, 0))]
        out_shape += [jax.ShapeDtypeStruct((t, LANES), F32)] * 2
        out_specs += [rows(LANES)] * 2
    out = pl.pallas_call(
        _normmod_kernel,
        out_shape=out_shape,
        grid=(t // tm,),
        in_specs=in_specs,
        out_specs=out_specs,
        compiler_params=_params("parallel"),
        name="normmod",
    )(x2, g, mod3, mod3, *extra)
    return out if rope_in is not None else out[0]


def _gelu(v):
    return 0.5 * v * (1.0 + lax.erf(v * math.sqrt(0.5)))


def _relu2(v):
    r = jnp.maximum(v, 0.0)
    return r * r


def _sigmoid(v):
    return 0.5 + 0.5 * jnp.tanh(0.5 * v)


_ACTS = {"gelu": _gelu, "sigmoid": _sigmoid, "relu2": _relu2, "none": lambda v: v}


def _cast_specs(jobs, n_steps, step_of):
    in_specs, out_specs, shapes, operands = [], [], [], []

    def rows_job(a):
        rows, cols = a.shape
        assert rows % (n_steps * 16) == 0
        spec = pl.BlockSpec((rows // n_steps, cols), lambda *ids: (step_of(*ids), 0))
        return a, spec, spec, a.shape

    def transposed_job(w_t, start, width):
        d, cols = w_t.shape[1], width // n_steps
        assert width % (n_steps * LANES) == 0
        src = pl.BlockSpec(
            (pl.Element(cols), pl.Element(d)),
            lambda *ids: (pl.multiple_of(start + step_of(*ids) * cols, math.gcd(start, cols)), 0))
        return w_t, src, pl.BlockSpec((d, cols), lambda *ids: (0, step_of(*ids))), (d, width)

    def widen_heads_job(w_uq):
        rows = w_uq.shape[0] // n_steps
        assert w_uq.shape[0] % (n_steps * 16) == 0
        index = lambda *ids: (step_of(*ids), 0)
        wide = MLA_HEADS * QK_PAD
        return (w_uq, pl.BlockSpec((rows, w_uq.shape[1]), index), pl.BlockSpec((rows, wide), index),
                (w_uq.shape[0], wide))

    for job in jobs:
        if not isinstance(job, tuple):
            a, src, dst, shape = rows_job(job)
        elif isinstance(job[0], str):
            a, src, dst, shape = widen_heads_job(job[1])
        else:
            a, src, dst, shape = transposed_job(*job)
        operands.append(a)
        in_specs.append(src)
        out_specs.append(dst)
        shapes.append(jax.ShapeDtypeStruct(shape, BF16))
    return in_specs, out_specs, shapes, operands


def _widen_heads(w, o_ref):
    for h in range(MLA_HEADS):
        src, dst = h * QK_HEAD, h * QK_PAD
        o_ref[:, dst:dst + QK_HEAD] = w[:, src:src + QK_HEAD]
        o_ref[:, dst + QK_HEAD:dst + QK_PAD] = w[:, src + QK_NOPE:src + QK_HEAD]


def _do_casts(refs, n):
    for src, dst in zip(refs[:n], refs[len(refs) - n:]):
        blk = src[...]
        if src.shape == dst.shape:
            dst[...] = blk.astype(BF16)
        elif src.shape[0] == dst.shape[0]:
            _widen_heads(blk.astype(BF16), dst)
        else:
            dst[...] = blk.T.astype(BF16)


def _mm_act_kernel(x_ref, w_ref, *refs, act, n_cast):
    o_ref = refs[n_cast]
    acc = jnp.dot(x_ref[...], w_ref[...], preferred_element_type=F32)
    o_ref[...] = _ACTS[act](acc).astype(o_ref.dtype)
    _do_casts(refs, n_cast)


def _mm_act(x, w, act, out_dtype, tm, tn, name, casts=()):
    t, k = x.shape
    n = w.shape[1]
    gm, gn = t // tm, n // tn
    cast_in, cast_out, cast_shapes, cast_ops = _cast_specs(casts, gm * gn, lambda m, j: m * gn + j)
    out = pl.pallas_call(
        functools.partial(_mm_act_kernel, act=act, n_cast=len(casts)),
        out_shape=[jax.ShapeDtypeStruct((t, n), out_dtype)] + cast_shapes,
        grid=(gm, gn),
        in_specs=[
            pl.BlockSpec((tm, k), lambda m, j: (m, 0)),
            pl.BlockSpec((k, tn), lambda m, j: (0, j)),
        ] + cast_in,
        out_specs=[pl.BlockSpec((tm, tn), lambda m, j: (m, j))] + cast_out,
        compiler_params=_params("parallel", "parallel"),
        name=name,
    )(x, w, *cast_ops)
    return out if casts else out[0]


def _gmlp_kernel(u_ref, v_ref, gv_ref, ws_ref, bias_ref, o_ref, *, n_chunks):
    row = lax.broadcasted_iota(jnp.int32, (CHUNK, CHUNK), 0)
    col = lax.broadcasted_iota(jnp.int32, (CHUNK, CHUNK), 1)
    causal = col <= row
    gv = gv_ref[...]
    for ci in range(n_chunks):
        rows = pl.ds(ci * CHUNK, CHUNK)
        v = v_ref[rows, :].astype(F32)
        vn = v * lax.rsqrt(jnp.mean(v * v, axis=-1, keepdims=True) + EPS) * gv
        vn = vn.astype(BF16)
        for g in range(GM_GROUPS):
            cols = slice(g * GM_GROUP_W, (g + 1) * GM_GROUP_W)
            w = jnp.where(causal, ws_ref[g], 0.0).astype(BF16)
            mixed = jnp.dot(w, vn[:, cols], preferred_element_type=F32) + bias_ref[:, cols]
            o_ref[rows, cols] = (u_ref[rows, cols].astype(F32) * mixed).astype(o_ref.dtype)


def _gmlp(z, g_v, w_s, bias_full, tm=512):
    t = z.shape[0]
    return pl.pallas_call(
        functools.partial(_gmlp_kernel, n_chunks=tm // CHUNK),
        out_shape=jax.ShapeDtypeStruct((t, GM_WIDTH), BF16),
        grid=(t // tm,),
        in_specs=[
            pl.BlockSpec((tm, GM_WIDTH), lambda m: (m, 0)),
            pl.BlockSpec((tm, GM_WIDTH), lambda m: (m, 1)),
            pl.BlockSpec((1, GM_WIDTH), lambda m: (0, 0)),
            pl.BlockSpec((GM_GROUPS, CHUNK, CHUNK), lambda m: (0, 0, 0)),
            pl.BlockSpec((CHUNK, GM_WIDTH), lambda m: (0, 0)),
        ],
        out_specs=pl.BlockSpec((tm, GM_WIDTH), lambda m: (m, 0)),
        compiler_params=_params("parallel"),
        name="gmlp",
    )(z, z, g_v, w_s, bias_full)


def _rope_tab_kernel(pos_ref, freq_ref, c_ref, s_ref):
    ang = pos_ref[...].astype(F32) * freq_ref[...]
    lane = lax.broadcasted_iota(jnp.int32, ang.shape, 1)
    cos = jnp.cos(ang)
    sin = jnp.sin(ang)
    c_ref[...] = jnp.where(lane < QK_ROPE, cos, 0.0)
    s_ref[...] = jnp.where(lane < QK_ROPE // 2, -sin, jnp.where(lane < QK_ROPE, sin, 0.0))


def _rope(v, c, s):
    return v * c + pltpu.roll(v, QK_ROPE // 2, 1) * s


def _rms_rows(x, g):
    return x * lax.rsqrt(jnp.mean(x * x, axis=-1, keepdims=True) + EPS) * g


def _qprep_kernel(lat_ref, glat_ref, w_ref, g1_ref, g2_ref, c_ref, s_ref, o_ref):
    xn = _rms_rows(lat_ref[...], glat_ref[...]).astype(BF16)
    c, s = c_ref[...], s_ref[...]
    g1, g2 = g1_ref[...], g2_ref[...]
    scale = QK_HEAD ** -0.5 * math.log2(math.e)
    for h in range(MLA_HEADS):
        acc = jnp.dot(xn, w_ref[:, h * QK_PAD:(h + 1) * QK_PAD], preferred_element_type=F32)
        a = acc[:, :LANES]
        b = acc[:, LANES:]
        ssq = jnp.sum(a * a + 0.5 * (b * b), axis=-1, keepdims=True)
        r = lax.rsqrt(ssq * (1.0 / QK_HEAD) + EPS) * scale
        o_ref[:, h * QK_PAD:h * QK_PAD + LANES] = (a * r * g1).astype(o_ref.dtype)
        o_ref[:, h * QK_PAD + LANES:(h + 1) * QK_PAD] = _rope(b * r * g2, c, s).astype(o_ref.dtype)


def _qprep(lat, g_q_lat, wq, g1, g2, c, s, tm=512):
    t = lat.shape[0]
    tn = MLA_HEADS * QK_PAD
    row = lambda m: (m, 0)
    const = lambda m: (0, 0)
    return pl.pallas_call(
        _qprep_kernel,
        out_shape=jax.ShapeDtypeStruct((t, tn), BF16),
        grid=(t // tm,),
        in_specs=[
            pl.BlockSpec((tm, Q_LORA), row),
            pl.BlockSpec((1, Q_LORA), const),
            pl.BlockSpec((Q_LORA, tn), const, pipeline_mode=pl.Buffered(1)),
            pl.BlockSpec((1, LANES), const),
            pl.BlockSpec((1, LANES), const),
            pl.BlockSpec((tm, LANES), row),
            pl.BlockSpec((tm, LANES), row),
        ],
        out_specs=pl.BlockSpec((tm, tn), row),
        compiler_params=_params("parallel"),
        name="q_prep",
    )(lat, g_q_lat, wq, g1, g2, c, s)


def _kvprep_kernel(lat_ref, kpe_ref, glat_ref, w_ref, g1_ref, g2_ref, c_ref, s_ref,
                   k_ref, v_ref):
    xn = _rms_rows(lat_ref[...], glat_ref[...]).astype(BF16)
    kpe = kpe_ref[...]
    ssq_pe = 0.5 * jnp.sum(kpe * kpe, axis=-1, keepdims=True)
    kr = _rope(kpe * g2_ref[...], c_ref[...], s_ref[...])
    g1 = g1_ref[...]
    width = QK_NOPE + V_HEAD
    for h in range(MLA_HEADS):
        acc = jnp.dot(xn, w_ref[:, h * width:(h + 1) * width], preferred_element_type=F32)
        a = acc[:, :QK_NOPE]
        ssq = jnp.sum(a * a, axis=-1, keepdims=True) + ssq_pe
        r = lax.rsqrt(ssq * (1.0 / QK_HEAD) + EPS)
        k_ref[:, h * QK_PAD:h * QK_PAD + LANES] = (a * r * g1).astype(k_ref.dtype)
        k_ref[:, h * QK_PAD + LANES:(h + 1) * QK_PAD] = (kr * r).astype(k_ref.dtype)
        v_ref[:, h * V_HEAD:(h + 1) * V_HEAD] = acc[:, QK_NOPE:].astype(v_ref.dtype)


def _kvprep(lat, g_kv_lat, wkv, g1, g2, c, s, tm=512):
    t = lat.shape[0]
    row = lambda m: (m, 0)
    const = lambda m: (0, 0)
    return pl.pallas_call(
        _kvprep_kernel,
        out_shape=(jax.ShapeDtypeStruct((t, MLA_HEADS * QK_PAD), BF16),
                   jax.ShapeDtypeStruct((t, MLA_HEADS * V_HEAD), BF16)),
        grid=(t // tm,),
        in_specs=[
            pl.BlockSpec((tm, KV_LORA), lambda m: (m, Q_LORA // KV_LORA)),
            pl.BlockSpec((tm, LANES), lambda m: (m, (Q_LORA + KV_LORA) // LANES)),
            pl.BlockSpec((1, KV_LORA), const),
            pl.BlockSpec(wkv.shape, const, pipeline_mode=pl.Buffered(1)),
            pl.BlockSpec((1, LANES), const),
            pl.BlockSpec((1, LANES), const),
            pl.BlockSpec((tm, LANES), row),
            pl.BlockSpec((tm, LANES), row),
        ],
        out_specs=(pl.BlockSpec((tm, MLA_HEADS * QK_PAD), row),
                   pl.BlockSpec((tm, MLA_HEADS * V_HEAD), row)),
        compiler_params=_params("parallel"),
        name="kv_prep",
    )(lat, lat, g_kv_lat, wkv, g1, g2, c, s)


def _attn_kernel(q_ref, k_ref, v_ref, *refs, tq, tk, heads, n_cast):
    o_ref, m_ref, acc_ref = refs[n_cast], refs[-2], refs[-1]
    _do_casts(refs[:-2], n_cast)
    i = pl.program_id(2)
    nc = tk // LANES
    ratio = tq // tk
    m_ref[...] = jnp.full(m_ref.shape, -jnp.inf, F32)
    acc_ref[...] = jnp.zeros(acc_ref.shape, F32)

    def block(j, r0, masked):
        keys = pl.ds(pl.multiple_of(j * tk, tk), tk)
        nr = tq - r0
        for h in range(heads):
            q = q_ref[r0:, h * QK_PAD:(h + 1) * QK_PAD]
            k = k_ref[keys, h * QK_PAD:(h + 1) * QK_PAD]
            s = lax.dot_general(q, k, (((1,), (1,)), ((), ())), preferred_element_type=F32)
            if masked:
                row = lax.broadcasted_iota(jnp.int32, (nr, tk), 0)
                col = lax.broadcasted_iota(jnp.int32, (nr, tk), 1)
                s = jnp.where(col <= row, s, -jnp.inf)
            chunks = [s[:, c * LANES:(c + 1) * LANES] for c in range(nc)]
            m_part = functools.reduce(jnp.maximum, chunks)
            m_prev = m_ref[h, r0:, :]
            m_cur = jnp.broadcast_to(jnp.max(m_part, axis=1, keepdims=True), m_part.shape)
            m_new = jnp.maximum(m_prev, m_cur)
            alpha = jnp.exp2(m_prev - m_new)
            m_ref[h, r0:, :] = m_new
            p = jnp.concatenate([jnp.exp2(ch - m_new) for ch in chunks], axis=1).astype(BF16)
            v = v_ref[keys, h * V_HEAD:(h + 1) * V_HEAD]
            v1 = jnp.concatenate([v, jnp.ones_like(v)], axis=1)
            alpha2 = jnp.concatenate([alpha, alpha], axis=1)
            acc_ref[h, r0:, :] = alpha2 * acc_ref[h, r0:, :] + jnp.dot(p, v1, preferred_element_type=F32)

    def body(j, carry):
        block(j, 0, False)
        return carry

    if tq < SEQ:
        lax.fori_loop(0, ratio * i, body, 0)
    for d in range(ratio):
        block(ratio * i + d, d * tk, True)
    for h in range(heads):
        acc = acc_ref[h]
        o_ref[:, h * V_HEAD:(h + 1) * V_HEAD] = (acc[:, :V_HEAD] / acc[:, V_HEAD:]).astype(o_ref.dtype)


def _attention(q, k, v, batch, tq=2048, tk=256, heads=2, casts=()):
    t = q.shape[0]
    nq = SEQ // tq
    gh = MLA_HEADS // heads
    cast_in, cast_out, cast_shapes, cast_ops = _cast_specs(
        casts, batch * gh * nq, lambda b, h, i: (b * gh + h) * nq + i)
    out = pl.pallas_call(
        functools.partial(_attn_kernel, tq=tq, tk=tk, heads=heads, n_cast=len(casts)),
        out_shape=[jax.ShapeDtypeStruct((t, MLA_HEADS * V_HEAD), BF16)] + cast_shapes,
        grid=(batch, gh, nq),
        in_specs=[
            pl.BlockSpec((tq, heads * QK_PAD), lambda b, h, i: (b * nq + i, h)),
            pl.BlockSpec((SEQ, heads * QK_PAD), lambda b, h, i: (b, h)),
            pl.BlockSpec((SEQ, heads * V_HEAD), lambda b, h, i: (b, h)),
        ] + cast_in,
        out_specs=[pl.BlockSpec((tq, heads * V_HEAD), lambda b, h, i: (b * nq + i, h))] + cast_out,
        scratch_shapes=[pltpu.VMEM((heads, tq, LANES), F32), pltpu.VMEM((heads, tq, 2 * V_HEAD), F32)],
        compiler_params=_params("parallel", "parallel", "arbitrary"),
        name="attention",
    )(q, k, v, *cast_ops)
    return out if casts else out[0]


def _branch_kernel(a_ref, wa_ref, b_ref, wb_ref, ga_ref, gb_ref, o_ref):
    ya = jnp.dot(a_ref[...], wa_ref[...], preferred_element_type=F32)
    yb = jnp.dot(b_ref[...], wb_ref[...], preferred_element_type=F32)
    o_ref[...] = (ga_ref[...].astype(F32) * ya + gb_ref[...].astype(F32) * yb).astype(o_ref.dtype)


def _branch(a, wa, b, wb, gates, tm=1024, tn=512):
    t = a.shape[0]
    nb = D_MODEL // tn
    return pl.pallas_call(
        _branch_kernel,
        out_shape=jax.ShapeDtypeStruct((t, D_MODEL), BF16),
        grid=(t // tm, nb),
        in_specs=[
            pl.BlockSpec((tm, a.shape[1]), lambda m, j: (m, 0)),
            pl.BlockSpec((wa.shape[0], tn), lambda m, j: (0, j)),
            pl.BlockSpec((tm, b.shape[1]), lambda m, j: (m, 0)),
            pl.BlockSpec((wb.shape[0], tn), lambda m, j: (0, j)),
            pl.BlockSpec((tm, tn), lambda m, j: (m, j)),
            pl.BlockSpec((tm, tn), lambda m, j: (m, nb + j)),
        ],
        out_specs=pl.BlockSpec((tm, tn), lambda m, j: (m, j)),
        compiler_params=_params("parallel", "parallel"),
        name="branch_merge",
    )(a, wa, b, wb, gates, gates)


def _mm_resid_kernel(x_ref, w_ref, res_ref, gate_ref, *refs, nk, n_cast):
    o_ref = refs[n_cast]
    _do_casts(refs, n_cast)
    kk = pl.program_id(2)
    if nk == 1:
        part = jnp.dot(x_ref[...], w_ref[...], preferred_element_type=F32)
        o_ref[...] = res_ref[...] + gate_ref[0] * part
        return

    @pl.when(kk == 0)
    def _():
        o_ref[...] = jnp.zeros(o_ref.shape, F32)

    o_ref[...] += jnp.dot(x_ref[...], w_ref[...], preferred_element_type=F32)

    @pl.when(kk == nk - 1)
    def _():
        o_ref[...] = res_ref[...] + gate_ref[0] * o_ref[...]


def _mm_resid(x, w, res, mod3, gate_idx, tm, tn, tk, name, casts=()):
    t, k = x.shape
    n = w.shape[1]
    gm, gn, nk = t // tm, n // tn, k // tk
    cast_in, cast_out, cast_shapes, cast_ops = _cast_specs(
        casts, gm * gn * nk, lambda m, j, kk: (m * gn + j) * nk + kk)
    out = pl.pallas_call(
        functools.partial(_mm_resid_kernel, nk=nk, n_cast=len(casts)),
        out_shape=[jax.ShapeDtypeStruct((t, n), F32)] + cast_shapes,
        grid=(gm, gn, nk),
        in_specs=[
            pl.BlockSpec((tm, tk), lambda m, j, kk: (m, kk)),
            pl.BlockSpec((tk, tn), lambda m, j, kk: (kk, j)),
            pl.BlockSpec((tm, tn), lambda m, j, kk: (m, j)),
            _mod_spec(gate_idx, tm, tn, ncol_arg=True),
        ] + cast_in,
        out_specs=[pl.BlockSpec((tm, tn), lambda m, j, kk: (m, j))] + cast_out,
        compiler_params=_params("parallel", "parallel", "arbitrary"),
        name=name,
    )(x, w, res, mod3, *cast_ops)
    return out if casts else out[0]


def _repack_kernel(wt_ref, o_ref, *, dup_tail):
    blk = wt_ref[...]
    if dup_tail:
        last = pl.program_id(0) == pl.num_programs(0) - 1
        half = blk.shape[0] // 2
        blk = jnp.where(last, jnp.concatenate([blk[:half], blk[:half]], axis=0), blk)
    o_ref[...] = blk.T.astype(BF16)


def _repack_w_in(w_in_t, start, width, cols, name, dup_tail=False):
    d = w_in_t.shape[1]
    return pl.pallas_call(
        functools.partial(_repack_kernel, dup_tail=dup_tail),
        out_shape=jax.ShapeDtypeStruct((d, width), BF16),
        grid=(width // cols,),
        in_specs=[pl.BlockSpec((pl.Element(cols), pl.Element(d)),
                               lambda i: (pl.multiple_of(start + i * cols, math.gcd(start, cols)), 0))],
        out_specs=pl.BlockSpec((d, cols), lambda i: (0, i)),
        compiler_params=_params("parallel"),
        name=name,
    )(w_in_t)


def _pad_cols(a, width):
    return jnp.pad(a, ((0, 0), (0, width - a.shape[1])))


def _layer(x2, batch, mod3, rope_in, tables, g_norm1, w_in, g_v, w_s, b_s, g_q_lat, g_kv_lat,
           w_uq, w_ukv, g_qn, g_kn, w_branch_a, w_branch_b, w_out, g_norm2, w_ff1, w_ff2):
    w_in_t = w_in.T
    w_uv = _repack_w_in(w_in_t, 0, OFF_Q, 512, "repack_uv")
    w_lat = _repack_w_in(w_in_t, OFF_Q, LAT_COLS, LANES, "repack_lat", dup_tail=True)
    bias_full = jnp.repeat(b_s.T, GM_GROUP_W, axis=1)
    row = lambda v: v.reshape(1, -1)
    twice = lambda v: jnp.concatenate([v, v]).reshape(1, -1)
    qg1, qg2 = row(g_qn[:QK_NOPE]), twice(g_qn[QK_NOPE:])
    kg1, kg2 = row(g_kn[:QK_NOPE]), twice(g_kn[QK_NOPE:])

    if tables is None:
        h1, *tables = _normmod(x2, row(g_norm1), mod3, 0, 1, rope_in=rope_in)
    else:
        h1 = _normmod(x2, row(g_norm1), mod3, 0, 1)
    c_tab, s_tab = tables
    z, w_gate, wq = _mm_act(h1, w_uv, "gelu", BF16, 1024, 512, "in_proj_uv",
                            casts=((w_in_t, OFF_GATE, w_in.shape[1] - OFF_GATE), ("widen_heads", w_uq)))
    lat, wkv = _mm_act(h1, w_lat, "none", F32, 512, LAT_COLS, "in_proj_lat", casts=(w_ukv,))
    gates, wa, wb, wo = _mm_act(h1, w_gate, "sigmoid", BF16, 1024, 1024, "in_proj_gate",
                                casts=(w_branch_a, w_branch_b, w_out))
    gm = _gmlp(z, row(g_v), w_s, bias_full)
    q = _qprep(lat, row(g_q_lat), wq, qg1, qg2, c_tab, s_tab)
    k, v = _kvprep(lat, row(g_kv_lat), wkv, kg1, kg2, c_tab, s_tab)
    att, w1 = _attention(q, k, v, batch, casts=(w_ff1,))
    mixed = _branch(gm, wa, att, wb, gates)
    x2 = _mm_resid(mixed, wo, x2, mod3, 2, 1024, 1024, D_MODEL, "out_proj")

    h2 = _normmod(x2, row(g_norm2), mod3, 3, 4)
    hid, w2 = _mm_act(h2, w1, "relu2", BF16, 1024, 1024, "ff1", casts=(w_ff2,))
    return _mm_resid(hid, w2, x2, mod3, 5, 1024, 1024, 4096, "ff2"), tables


def kernel(x, c, positions, w_ada, b_ada, g_norm1, w_in, g_v, w_s, b_s, g_q_lat, g_kv_lat, w_uq, w_ukv,
           g_qn, g_kn, w_branch_a, w_branch_b, w_out, g_norm2, w_ff1, w_ff2):
    batch, seq, d = x.shape
    assert (seq, d) == (SEQ, D_MODEL)
    depth = w_ada.shape[0]
    t = batch * seq
    x2 = x.reshape(t, d)

    inv_freq = 1.0 / (ROPE_THETA ** (jnp.arange(0, QK_ROPE, 2, dtype=F32) / QK_ROPE))
    freq_row = _pad_cols(jnp.concatenate([inv_freq, inv_freq]).reshape(1, QK_ROPE), LANES)
    rope_in = (positions.reshape(t, 1), freq_row)

    c8 = jnp.pad(c, ((0, 8 - batch), (0, 0)))
    tables = None
    for l in range(depth):
        mod = _ada(c8, w_ada[l], b_ada[l].reshape(1, -1))[:batch]
        mod3 = mod.reshape(batch * N_MOD, 1, d)
        x2, tables = _layer(x2, batch, mod3, rope_in, tables, g_norm1[l], w_in[l], g_v[l], w_s[l], b_s[l],
                            g_q_lat[l], g_kv_lat[l], w_uq[l], w_ukv[l], g_qn[l], g_kn[l], w_branch_a[l],
                            w_branch_b[l], w_out[l], g_norm2[l], w_ff1[l], w_ff2[l])
    return x2.reshape(batch, seq, d)
```

```python
import functools
import math

import jax
import jax.numpy as jnp
from jax import lax
from jax.experimental import pallas as pl
from jax.experimental.pallas import tpu as pltpu

D_MODEL = 4096
SEQ = 2048
GM_WIDTH = 2048
GM_GROUPS = 8
GM_GROUP_W = GM_WIDTH // GM_GROUPS
CHUNK = 128
MLA_HEADS = 32
QK_NOPE = 128
QK_ROPE = 64
QK_HEAD = QK_NOPE + QK_ROPE
V_HEAD = 128
Q_LORA = 1024
KV_LORA = 512
ROPE_THETA = 10000.0
D_FF = 4 * D_MODEL
N_MOD = 6
EPS = 1e-6
OFF_Q = 2 * GM_WIDTH
OFF_KV = OFF_Q + Q_LORA
OFF_KPE = OFF_KV + KV_LORA
OFF_GATE = OFF_KPE + QK_ROPE

LANES = 128
QK_PAD = 2 * LANES
LAT_COLS = Q_LORA + KV_LORA + LANES
VMEM_LIMIT = 56 * 1024 * 1024

F32 = jnp.float32
BF16 = jnp.bfloat16


def _params(*sem):
    return pltpu.CompilerParams(dimension_semantics=sem, vmem_limit_bytes=VMEM_LIMIT)


def _ada_kernel(c_ref, w_ref, b_ref, o_ref, cond_ref):
    c = c_ref[...]
    cond = (c * jax.nn.sigmoid(c)).astype(BF16)
    cond_ref[...] = cond
    o_ref[...] = jnp.dot(cond, w_ref[...].astype(BF16), preferred_element_type=F32) + b_ref[...]


def _ada(c8, w_ada, b_ada, n, tn=1024):
    return pl.pallas_call(
        _ada_kernel,
        out_shape=(jax.ShapeDtypeStruct((8, n), F32), jax.ShapeDtypeStruct((8, D_MODEL), BF16)),
        grid=(n // tn,),
        in_specs=[
            pl.BlockSpec((8, D_MODEL), lambda j: (0, 0)),
            pl.BlockSpec((D_MODEL, tn), lambda j: (0, j)),
            pl.BlockSpec((1, tn), lambda j: (0, j)),
        ],
        out_specs=(pl.BlockSpec((8, tn), lambda j: (0, j)), pl.BlockSpec((8, D_MODEL), lambda j: (0, 0))),
        compiler_params=_params("arbitrary"),
        name="ada",
    )(c8, w_ada, b_ada)


def _normmod_kernel(x_ref, g_ref, sh_ref, sc_ref, *refs):
    o_ref = refs[-1] if len(refs) == 1 else refs[2]
    x = x_ref[...]
    y = x * lax.rsqrt(jnp.mean(x * x, axis=-1, keepdims=True) + EPS)
    y = y * g_ref[...]
    o_ref[...] = (y * (1.0 + sc_ref[0, 0]) + sh_ref[0, 0]).astype(o_ref.dtype)
    if len(refs) > 1:
        pos_ref, freq_ref, _, c_ref, s_ref = refs
        _rope_tab_kernel(pos_ref, freq_ref, c_ref, s_ref)


def _mod_spec(idx, tm, width, ncol_arg=False):
    per_batch = SEQ // tm
    if ncol_arg:
        return pl.BlockSpec((1, 1, 1, width), lambda m, n, *_: (m // per_batch, idx, 0, n))
    return pl.BlockSpec((1, 1, 1, width), lambda m, *_: (m // per_batch, idx, 0, 0))


def _normmod(x2, g, mod4, shift_idx, scale_idx, tm=512, rope_in=None):
    t, d = x2.shape
    rows = lambda width: pl.BlockSpec((tm, width), lambda m: (m, 0))
    in_specs = [rows(d), pl.BlockSpec((1, d), lambda m: (0, 0)),
                _mod_spec(shift_idx, tm, d), _mod_spec(scale_idx, tm, d)]
    out_shape, out_specs, extra = [jax.ShapeDtypeStruct((t, d), BF16)], [rows(d)], ()
    if rope_in is not None:
        extra = rope_in
        in_specs += [rows(1), pl.BlockSpec((1, LANES), lambda m: (0, 0))]
        out_shape += [jax.ShapeDtypeStruct((t, LANES), F32)] * 2
        out_specs += [rows(LANES)] * 2
    out = pl.pallas_call(
        _normmod_kernel,
        out_shape=out_shape,
        grid=(t // tm,),
        in_specs=in_specs,
        out_specs=out_specs,
        compiler_params=_params("parallel"),
        name="normmod",
    )(x2, g, mod4, mod4, *extra)
    return out if rope_in is not None else out[0]


def _gelu(v):
    return 0.5 * v * (1.0 + lax.erf(v * math.sqrt(0.5)))


def _relu2(v):
    r = jnp.maximum(v, 0.0)
    return r * r


def _sigmoid(v):
    return 0.5 + 0.5 * jnp.tanh(0.5 * v)


_ACTS = {"gelu": _gelu, "sigmoid": _sigmoid, "relu2": _relu2, "none": lambda v: v}


def _cast_specs(jobs, n_steps, step_of):
    in_specs, out_specs, shapes, operands = [], [], [], []

    def rows_job(a):
        rows, cols = a.shape
        assert rows % (n_steps * 16) == 0
        spec = pl.BlockSpec((rows // n_steps, cols), lambda *ids: (step_of(*ids), 0))
        return a, spec, spec, a.shape

    def transposed_job(w_t, start, width):
        d, cols = w_t.shape[1], width // n_steps
        assert width % (n_steps * LANES) == 0
        src = pl.BlockSpec(
            (pl.Element(cols), pl.Element(d)),
            lambda *ids: (pl.multiple_of(start + step_of(*ids) * cols, math.gcd(start, cols)), 0))
        return w_t, src, pl.BlockSpec((d, cols), lambda *ids: (0, step_of(*ids))), (d, width)

    def widen_heads_job(w_uq):
        rows = w_uq.shape[0] // n_steps
        assert w_uq.shape[0] % (n_steps * 16) == 0
        index = lambda *ids: (step_of(*ids), 0)
        wide = MLA_HEADS * QK_PAD
        return (w_uq, pl.BlockSpec((rows, w_uq.shape[1]), index), pl.BlockSpec((rows, wide), index),
                (w_uq.shape[0], wide))

    for job in jobs:
        if not isinstance(job, tuple):
            a, src, dst, shape = rows_job(job)
        elif isinstance(job[0], str):
            a, src, dst, shape = widen_heads_job(job[1])
        else:
            a, src, dst, shape = transposed_job(*job)
        operands.append(a)
        in_specs.append(src)
        out_specs.append(dst)
        shapes.append(jax.ShapeDtypeStruct(shape, BF16))
    return in_specs, out_specs, shapes, operands


def _widen_heads(w, o_ref):
    for h in range(MLA_HEADS):
        src, dst = h * QK_HEAD, h * QK_PAD
        o_ref[:, dst:dst + QK_HEAD] = w[:, src:src + QK_HEAD]
        o_ref[:, dst + QK_HEAD:dst + QK_PAD] = w[:, src + QK_NOPE:src + QK_HEAD]


def _do_casts(refs, n):
    for src, dst in zip(refs[:n], refs[len(refs) - n:]):
        blk = src[...]
        if src.shape == dst.shape:
            dst[...] = blk.astype(BF16)
        elif src.shape[0] == dst.shape[0]:
            _widen_heads(blk.astype(BF16), dst)
        else:
            dst[...] = blk.T.astype(BF16)


def _mm_act_kernel(x_ref, w_ref, *refs, act, n_cast):
    o_ref = refs[n_cast]
    acc = jnp.dot(x_ref[...], w_ref[...], preferred_element_type=F32)
    o_ref[...] = _ACTS[act](acc).astype(o_ref.dtype)
    _do_casts(refs, n_cast)


def _mm_act(x, w, act, out_dtype, tm, tn, name, casts=()):
    t, k = x.shape
    n = w.shape[1]
    gm, gn = t // tm, n // tn
    cast_in, cast_out, cast_shapes, cast_ops = _cast_specs(casts, gm * gn, lambda m, j: m * gn + j)
    out = pl.pallas_call(
        functools.partial(_mm_act_kernel, act=act, n_cast=len(casts)),
        out_shape=[jax.ShapeDtypeStruct((t, n), out_dtype)] + cast_shapes,
        grid=(gm, gn),
        in_specs=[
            pl.BlockSpec((tm, k), lambda m, j: (m, 0)),
            pl.BlockSpec((k, tn), lambda m, j: (0, j)),
        ] + cast_in,
        out_specs=[pl.BlockSpec((tm, tn), lambda m, j: (m, j))] + cast_out,
        compiler_params=_params("parallel", "parallel"),
        name=name,
    )(x, w, *cast_ops)
    return out if casts else out[0]


def _gmlp_kernel(u_ref, v_ref, gv_ref, ws_ref, bias_ref, o_ref, *, n_chunks):
    row = lax.broadcasted_iota(jnp.int32, (CHUNK, CHUNK), 0)
    col = lax.broadcasted_iota(jnp.int32, (CHUNK, CHUNK), 1)
    causal = col <= row
    gv = gv_ref[...]
    for ci in range(n_chunks):
        rows = pl.ds(ci * CHUNK, CHUNK)
        v = v_ref[rows, :].astype(F32)
        vn = v * lax.rsqrt(jnp.mean(v * v, axis=-1, keepdims=True) + EPS) * gv
        vn = vn.astype(BF16)
        for g in range(GM_GROUPS):
            cols = slice(g * GM_GROUP_W, (g + 1) * GM_GROUP_W)
            w = jnp.where(causal, ws_ref[g], 0.0).astype(BF16)
            mixed = jnp.dot(w, vn[:, cols], preferred_element_type=F32) + bias_ref[:, cols]
            o_ref[rows, cols] = (u_ref[rows, cols].astype(F32) * mixed).astype(o_ref.dtype)


def _gmlp(z, g_v, w_s, bias_full, tm=512):
    t = z.shape[0]
    return pl.pallas_call(
        functools.partial(_gmlp_kernel, n_chunks=tm // CHUNK),
        out_shape=jax.ShapeDtypeStruct((t, GM_WIDTH), BF16),
        grid=(t // tm,),
        in_specs=[
            pl.BlockSpec((tm, GM_WIDTH), lambda m: (m, 0)),
            pl.BlockSpec((tm, GM_WIDTH), lambda m: (m, 1)),
            pl.BlockSpec((1, GM_WIDTH), lambda m: (0, 0)),
            pl.BlockSpec((GM_GROUPS, CHUNK, CHUNK), lambda m: (0, 0, 0)),
            pl.BlockSpec((CHUNK, GM_WIDTH), lambda m: (0, 0)),
        ],
        out_specs=pl.BlockSpec((tm, GM_WIDTH), lambda m: (m, 0)),
        compiler_params=_params("parallel"),
        name="gmlp",
    )(z, z, g_v, w_s, bias_full)


def _rope_tab_kernel(pos_ref, freq_ref, c_ref, s_ref):
    ang = pos_ref[...].astype(F32) * freq_ref[...]
    lane = lax.broadcasted_iota(jnp.int32, ang.shape, 1)
    cos = jnp.cos(ang)
    sin = jnp.sin(ang)
    c_ref[...] = jnp.where(lane < QK_ROPE, cos, 0.0)
    s_ref[...] = jnp.where(lane < QK_ROPE // 2, -sin, jnp.where(lane < QK_ROPE, sin, 0.0))


def _rope(v, c, s):
    return v * c + pltpu.roll(v, QK_ROPE // 2, 1) * s


def _rms_rows(x, g):
    return x * lax.rsqrt(jnp.mean(x * x, axis=-1, keepdims=True) + EPS) * g


def _qprep_kernel(lat_ref, glat_ref, w_ref, g1_ref, g2_ref, c_ref, s_ref, o_ref):
    xn = _rms_rows(lat_ref[...], glat_ref[...]).astype(BF16)
    c, s = c_ref[...], s_ref[...]
    g1, g2 = g1_ref[...], g2_ref[...]
    scale = QK_HEAD ** -0.5 * math.log2(math.e)
    for h in range(MLA_HEADS):
        acc = jnp.dot(xn, w_ref[:, h * QK_PAD:(h + 1) * QK_PAD], preferred_element_type=F32)
        a = acc[:, :LANES]
        b = acc[:, LANES:]
        ssq = jnp.sum(a * a + 0.5 * (b * b), axis=-1, keepdims=True)
        r = lax.rsqrt(ssq * (1.0 / QK_HEAD) + EPS) * scale
        o_ref[:, h * QK_PAD:h * QK_PAD + LANES] = (a * r * g1).astype(o_ref.dtype)
        o_ref[:, h * QK_PAD + LANES:(h + 1) * QK_PAD] = _rope(b * r * g2, c, s).astype(o_ref.dtype)


def _qprep(lat, g_q_lat, wq, g1, g2, c, s, tm=512):
    t = lat.shape[0]
    tn = MLA_HEADS * QK_PAD
    row = lambda m: (m, 0)
    const = lambda m: (0, 0)
    return pl.pallas_call(
        _qprep_kernel,
        out_shape=jax.ShapeDtypeStruct((t, tn), BF16),
        grid=(t // tm,),
        in_specs=[
            pl.BlockSpec((tm, Q_LORA), row),
            pl.BlockSpec((1, Q_LORA), const),
            pl.BlockSpec((Q_LORA, tn), const, pipeline_mode=pl.Buffered(1)),
            pl.BlockSpec((1, LANES), const),
            pl.BlockSpec((1, LANES), const),
            pl.BlockSpec((tm, LANES), row),
            pl.BlockSpec((tm, LANES), row),
        ],
        out_specs=pl.BlockSpec((tm, tn), row),
        compiler_params=_params("parallel"),
        name="q_prep",
    )(lat, g_q_lat, wq, g1, g2, c, s)


def _kvprep_kernel(lat_ref, kpe_ref, glat_ref, w_ref, g1_ref, g2_ref, c_ref, s_ref,
                   k_ref, v_ref):
    xn = _rms_rows(lat_ref[...], glat_ref[...]).astype(BF16)
    kpe = kpe_ref[...]
    ssq_pe = 0.5 * jnp.sum(kpe * kpe, axis=-1, keepdims=True)
    kr = _rope(kpe * g2_ref[...], c_ref[...], s_ref[...])
    g1 = g1_ref[...]
    width = QK_NOPE + V_HEAD
    for h in range(MLA_HEADS):
        acc = jnp.dot(xn, w_ref[:, h * width:(h + 1) * width], preferred_element_type=F32)
        a = acc[:, :QK_NOPE]
        ssq = jnp.sum(a * a, axis=-1, keepdims=True) + ssq_pe
        r = lax.rsqrt(ssq * (1.0 / QK_HEAD) + EPS)
        k_ref[:, h * QK_PAD:h * QK_PAD + LANES] = (a * r * g1).astype(k_ref.dtype)
        k_ref[:, h * QK_PAD + LANES:(h + 1) * QK_PAD] = (kr * r).astype(k_ref.dtype)
        v_ref[:, h * V_HEAD:(h + 1) * V_HEAD] = acc[:, QK_NOPE:].astype(v_ref.dtype)


def _kvprep(lat, g_kv_lat, wkv, g1, g2, c, s, tm=512):
    t = lat.shape[0]
    row = lambda m: (m, 0)
    const = lambda m: (0, 0)
    return pl.pallas_call(
        _kvprep_kernel,
        out_shape=(jax.ShapeDtypeStruct((t, MLA_HEADS * QK_PAD), BF16),
                   jax.ShapeDtypeStruct((t, MLA_HEADS * V_HEAD), BF16)),
        grid=(t // tm,),
        in_specs=[
            pl.BlockSpec((tm, KV_LORA), lambda m: (m, Q_LORA // KV_LORA)),
            pl.BlockSpec((tm, LANES), lambda m: (m, (Q_LORA + KV_LORA) // LANES)),
            pl.BlockSpec((1, KV_LORA), const),
            pl.BlockSpec(wkv.shape, const, pipeline_mode=pl.Buffered(1)),
            pl.BlockSpec((1, LANES), const),
            pl.BlockSpec((1, LANES), const),
            pl.BlockSpec((tm, LANES), row),
            pl.BlockSpec((tm, LANES), row),
        ],
        out_specs=(pl.BlockSpec((tm, MLA_HEADS * QK_PAD), row),
                   pl.BlockSpec((tm, MLA_HEADS * V_HEAD), row)),
        compiler_params=_params("parallel"),
        name="kv_prep",
    )(lat, lat, g_kv_lat, wkv, g1, g2, c, s)


def _attn_kernel(q_ref, k_ref, v_ref, *refs, tq, tk, heads, n_cast, ada):
    n_in = n_cast + (3 if ada else 0)
    o_ref, m_ref, acc_ref = refs[n_in], refs[-2], refs[-1]
    _do_casts(refs[:n_cast] + refs[n_in + 1:n_in + 1 + n_cast], n_cast)
    if ada:
        cond_ref, wada_ref, bada_ref = refs[n_cast:n_in]
        mod_ref = refs[n_in + 1 + n_cast]
        mod_ref[...] = jnp.dot(cond_ref[...], wada_ref[...].astype(BF16),
                               preferred_element_type=F32) + bada_ref[...]
    i = pl.program_id(2)
    nc = tk // LANES
    ratio = tq // tk
    m_ref[...] = jnp.full(m_ref.shape, -jnp.inf, F32)
    acc_ref[...] = jnp.zeros(acc_ref.shape, F32)

    def block(j, r0, masked):
        keys = pl.ds(pl.multiple_of(j * tk, tk), tk)
        nr = tq - r0
        for h in range(heads):
            q = q_ref[r0:, h * QK_PAD:(h + 1) * QK_PAD]
            k = k_ref[keys, h * QK_PAD:(h + 1) * QK_PAD]
            s = lax.dot_general(q, k, (((1,), (1,)), ((), ())), preferred_element_type=F32)
            if masked:
                row = lax.broadcasted_iota(jnp.int32, (nr, tk), 0)
                col = lax.broadcasted_iota(jnp.int32, (nr, tk), 1)
                s = jnp.where(col <= row, s, -jnp.inf)
            chunks = [s[:, c * LANES:(c + 1) * LANES] for c in range(nc)]
            m_part = functools.reduce(jnp.maximum, chunks)
            m_prev = m_ref[h, r0:, :]
            m_cur = jnp.broadcast_to(jnp.max(m_part, axis=1, keepdims=True), m_part.shape)
            m_new = jnp.maximum(m_prev, m_cur)
            alpha = jnp.exp2(m_prev - m_new)
            m_ref[h, r0:, :] = m_new
            p = jnp.concatenate([jnp.exp2((ch - m_new).astype(BF16)) for ch in chunks], axis=1)
            v = v_ref[keys, h * V_HEAD:(h + 1) * V_HEAD]
            v1 = jnp.concatenate([v, jnp.ones_like(v)], axis=1)
            alpha2 = jnp.concatenate([alpha, alpha], axis=1)
            acc_ref[h, r0:, :] = alpha2 * acc_ref[h, r0:, :] + jnp.dot(p, v1, preferred_element_type=F32)

    def body(j, carry):
        block(j, 0, False)
        return carry

    if tq < SEQ:
        lax.fori_loop(0, ratio * i, body, 0)
    for d in range(ratio):
        block(ratio * i + d, d * tk, True)
    for h in range(heads):
        acc = acc_ref[h]
        o_ref[:, h * V_HEAD:(h + 1) * V_HEAD] = (acc[:, :V_HEAD] / acc[:, V_HEAD:]).astype(o_ref.dtype)


def _attention(q, k, v, batch, tq=2048, tk=256, heads=2, casts=(), ada=None):
    t = q.shape[0]
    nq = SEQ // tq
    gh = MLA_HEADS // heads
    n_steps = batch * gh * nq
    step_of = lambda b, h, i: (b * gh + h) * nq + i
    cast_in, cast_out, cast_shapes, cast_ops = _cast_specs(casts, n_steps, step_of)
    ada_in, ada_out, ada_shapes, ada_ops = [], [], [], []
    if ada is not None:
        cond, w_ada, b_ada, first_col = ada
        width = (w_ada.shape[1] - first_col) // n_steps
        assert (w_ada.shape[1] - first_col) % (n_steps * LANES) == 0 and first_col % width == 0
        src = lambda b, h, i: (0, first_col // width + step_of(b, h, i))
        ada_in = [pl.BlockSpec(cond.shape, lambda b, h, i: (0, 0)),
                  pl.BlockSpec((w_ada.shape[0], width), src), pl.BlockSpec((1, width), src)]
        ada_out = [pl.BlockSpec((cond.shape[0], width), lambda b, h, i: (0, step_of(b, h, i)))]
        ada_shapes = [jax.ShapeDtypeStruct((cond.shape[0], w_ada.shape[1] - first_col), F32)]
        ada_ops = [cond, w_ada, b_ada]
    out = pl.pallas_call(
        functools.partial(_attn_kernel, tq=tq, tk=tk, heads=heads, n_cast=len(casts), ada=ada is not None),
        out_shape=[jax.ShapeDtypeStruct((t, MLA_HEADS * V_HEAD), BF16)] + cast_shapes + ada_shapes,
        grid=(batch, gh, nq),
        in_specs=[
            pl.BlockSpec((tq, heads * QK_PAD), lambda b, h, i: (b * nq + i, h)),
            pl.BlockSpec((SEQ, heads * QK_PAD), lambda b, h, i: (b, h)),
            pl.BlockSpec((SEQ, heads * V_HEAD), lambda b, h, i: (b, h)),
        ] + cast_in + ada_in,
        out_specs=[pl.BlockSpec((tq, heads * V_HEAD), lambda b, h, i: (b * nq + i, h))] + cast_out + ada_out,
        scratch_shapes=[pltpu.VMEM((heads, tq, LANES), F32), pltpu.VMEM((heads, tq, 2 * V_HEAD), F32)],
        compiler_params=_params("parallel", "parallel", "arbitrary"),
        name="attention",
    )(q, k, v, *cast_ops, *ada_ops)
    return out if (casts or ada is not None) else out[0]


def _branch_kernel(a_ref, wa_ref, b_ref, wb_ref, ga_ref, gb_ref, o_ref):
    ya = jnp.dot(a_ref[...], wa_ref[...], preferred_element_type=F32)
    yb = jnp.dot(b_ref[...], wb_ref[...], preferred_element_type=F32)
    o_ref[...] = (ga_ref[...].astype(F32) * ya + gb_ref[...].astype(F32) * yb).astype(o_ref.dtype)


def _branch(a, wa, b, wb, gates, tm=1024, tn=512):
    t = a.shape[0]
    nb = D_MODEL // tn
    return pl.pallas_call(
        _branch_kernel,
        out_shape=jax.ShapeDtypeStruct((t, D_MODEL), BF16),
        grid=(t // tm, nb),
        in_specs=[
            pl.BlockSpec((tm, a.shape[1]), lambda m, j: (m, 0)),
            pl.BlockSpec((wa.shape[0], tn), lambda m, j: (0, j)),
            pl.BlockSpec((tm, b.shape[1]), lambda m, j: (m, 0)),
            pl.BlockSpec((wb.shape[0], tn), lambda m, j: (0, j)),
            pl.BlockSpec((tm, tn), lambda m, j: (m, j)),
            pl.BlockSpec((tm, tn), lambda m, j: (m, nb + j)),
        ],
        out_specs=pl.BlockSpec((tm, tn), lambda m, j: (m, j)),
        compiler_params=_params("parallel", "parallel"),
        name="branch_merge",
    )(a, wa, b, wb, gates, gates)


def _mm_resid_kernel(x_ref, w_ref, res_ref, gate_ref, *refs, nk, n_cast):
    o_ref = refs[n_cast]
    _do_casts(refs, n_cast)
    kk = pl.program_id(2)
    if nk == 1:
        part = jnp.dot(x_ref[...], w_ref[...], preferred_element_type=F32)
        o_ref[...] = res_ref[...] + gate_ref[0, 0] * part
        return

    @pl.when(kk == 0)
    def _():
        o_ref[...] = jnp.zeros(o_ref.shape, F32)

    o_ref[...] += jnp.dot(x_ref[...], w_ref[...], preferred_element_type=F32)

    @pl.when(kk == nk - 1)
    def _():
        o_ref[...] = res_ref[...] + gate_ref[0, 0] * o_ref[...]


def _mm_resid(x, w, res, mod3, gate_idx, tm, tn, tk, name, casts=()):
    t, k = x.shape
    n = w.shape[1]
    gm, gn, nk = t // tm, n // tn, k // tk
    cast_in, cast_out, cast_shapes, cast_ops = _cast_specs(
        casts, gm * gn * nk, lambda m, j, kk: (m * gn + j) * nk + kk)
    out = pl.pallas_call(
        functools.partial(_mm_resid_kernel, nk=nk, n_cast=len(casts)),
        out_shape=[jax.ShapeDtypeStruct((t, n), F32)] + cast_shapes,
        grid=(gm, gn, nk),
        in_specs=[
            pl.BlockSpec((tm, tk), lambda m, j, kk: (m, kk)),
            pl.BlockSpec((tk, tn), lambda m, j, kk: (kk, j)),
            pl.BlockSpec((tm, tn), lambda m, j, kk: (m, j)),
            _mod_spec(gate_idx, tm, tn, ncol_arg=True),
        ] + cast_in,
        out_specs=[pl.BlockSpec((tm, tn), lambda m, j, kk: (m, j))] + cast_out,
        compiler_params=_params("parallel", "parallel", "arbitrary"),
        name=name,
    )(x, w, res, mod3, *cast_ops)
    return out if casts else out[0]


def _repack_kernel(wt_ref, o_ref, *, dup_tail):
    blk = wt_ref[...]
    if dup_tail:
        last = pl.program_id(0) == pl.num_programs(0) - 1
        half = blk.shape[0] // 2
        blk = jnp.where(last, jnp.concatenate([blk[:half], blk[:half]], axis=0), blk)
    o_ref[...] = blk.T.astype(BF16)


def _repack_w_in(w_in_t, start, width, cols, name, dup_tail=False):
    d = w_in_t.shape[1]
    return pl.pallas_call(
        functools.partial(_repack_kernel, dup_tail=dup_tail),
        out_shape=jax.ShapeDtypeStruct((d, width), BF16),
        grid=(width // cols,),
        in_specs=[pl.BlockSpec((pl.Element(cols), pl.Element(d)),
                               lambda i: (pl.multiple_of(start + i * cols, math.gcd(start, cols)), 0))],
        out_specs=pl.BlockSpec((d, cols), lambda i: (0, i)),
        compiler_params=_params("parallel"),
        name=name,
    )(w_in_t)


def _pad_cols(a, width):
    return jnp.pad(a, ((0, 0), (0, width - a.shape[1])))


def _layer(x2, batch, c8, rope_in, tables, w_ada, b_ada, g_norm1, w_in, g_v, w_s, b_s, g_q_lat, g_kv_lat,
           w_uq, w_ukv, g_qn, g_kn, w_branch_a, w_branch_b, w_out, g_norm2, w_ff1, w_ff2):
    d = x2.shape[1]
    b_ada = b_ada.reshape(1, -1)
    mod_a, cond = _ada(c8, w_ada, b_ada, 2 * d)
    mod_a = mod_a[:batch].reshape(batch, 2, 1, d)
    w_in_t = w_in.T
    w_uv = _repack_w_in(w_in_t, 0, OFF_Q, 512, "repack_uv")
    w_lat = _repack_w_in(w_in_t, OFF_Q, LAT_COLS, LANES, "repack_lat", dup_tail=True)
    bias_full = jnp.repeat(b_s.T, GM_GROUP_W, axis=1)
    row = lambda v: v.reshape(1, -1)
    twice = lambda v: jnp.concatenate([v, v]).reshape(1, -1)
    qg1, qg2 = row(g_qn[:QK_NOPE]), twice(g_qn[QK_NOPE:])
    kg1, kg2 = row(g_kn[:QK_NOPE]), twice(g_kn[QK_NOPE:])

    if tables is None:
        h1, *tables = _normmod(x2, row(g_norm1), mod_a, 0, 1, rope_in=rope_in)
    else:
        h1 = _normmod(x2, row(g_norm1), mod_a, 0, 1)
    c_tab, s_tab = tables
    z, w_gate, wq = _mm_act(h1, w_uv, "gelu", BF16, 1024, 512, "in_proj_uv",
                            casts=((w_in_t, OFF_GATE, w_in.shape[1] - OFF_GATE), ("widen_heads", w_uq)))
    lat, wkv = _mm_act(h1, w_lat, "none", F32, 512, LAT_COLS, "in_proj_lat", casts=(w_ukv,))
    gates, wa, wb, wo = _mm_act(h1, w_gate, "sigmoid", BF16, 1024, 1024, "in_proj_gate",
                                casts=(w_branch_a, w_branch_b, w_out))
    gm = _gmlp(z, row(g_v), w_s, bias_full)
    q = _qprep(lat, row(g_q_lat), wq, qg1, qg2, c_tab, s_tab)
    k, v = _kvprep(lat, row(g_kv_lat), wkv, kg1, kg2, c_tab, s_tab)
    att, w1, mod_b = _attention(q, k, v, batch, casts=(w_ff1,), ada=(cond, w_ada, b_ada, 2 * d))
    mod_b = mod_b[:batch].reshape(batch, N_MOD - 2, 1, d)
    mixed = _branch(gm, wa, att, wb, gates)
    x2 = _mm_resid(mixed, wo, x2, mod_b, 0, 1024, 1024, D_MODEL, "out_proj")

    h2 = _normmod(x2, row(g_norm2), mod_b, 1, 2)
    hid, w2 = _mm_act(h2, w1, "relu2", BF16, 1024, 1024, "ff1", casts=(w_ff2,))
    return _mm_resid(hid, w2, x2, mod_b, 3, 1024, 1024, 4096, "ff2"), tables


def kernel(x, c, positions, w_ada, b_ada, g_norm1, w_in, g_v, w_s, b_s, g_q_lat, g_kv_lat, w_uq, w_ukv,
           g_qn, g_kn, w_branch_a, w_branch_b, w_out, g_norm2, w_ff1, w_ff2):
    batch, seq, d = x.shape
    assert (seq, d) == (SEQ, D_MODEL)
    depth = w_ada.shape[0]
    t = batch * seq
    x2 = x.reshape(t, d)

    inv_freq = 1.0 / (ROPE_THETA ** (jnp.arange(0, QK_ROPE, 2, dtype=F32) / QK_ROPE))
    freq_row = _pad_cols(jnp.concatenate([inv_freq, inv_freq]).reshape(1, QK_ROPE), LANES)
    rope_in = (positions.reshape(t, 1), freq_row)

    c8 = jnp.pad(c, ((0, 8 - batch), (0, 0)))
    tables = None
    for l in range(depth):
        x2, tables = _layer(x2, batch, c8, rope_in, tables, w_ada[l], b_ada[l], g_norm1[l], w_in[l], g_v[l],
                            w_s[l], b_s[l],
                            g_q_lat[l], g_kv_lat[l], w_uq[l], w_ukv[l], g_qn[l], g_kn[l], w_branch_a[l],
                            w_branch_b[l], w_out[l], g_norm2[l], w_ff1[l], w_ff2[l])
    return x2.reshape(batch, seq, d)
```

```python
import functools
import math

import jax
import jax.numpy as jnp
from jax import lax
from jax.experimental import pallas as pl
from jax.experimental.pallas import tpu as pltpu

D_MODEL = 4096
SEQ = 2048
GM_WIDTH = 2048
GM_GROUPS = 8
GM_GROUP_W = GM_WIDTH // GM_GROUPS
CHUNK = 128
MLA_HEADS = 32
QK_NOPE = 128
QK_ROPE = 64
QK_HEAD = QK_NOPE + QK_ROPE
V_HEAD = 128
Q_LORA = 1024
KV_LORA = 512
ROPE_THETA = 10000.0
D_FF = 4 * D_MODEL
N_MOD = 6
EPS = 1e-6
OFF_Q = 2 * GM_WIDTH
OFF_KV = OFF_Q + Q_LORA
OFF_KPE = OFF_KV + KV_LORA
OFF_GATE = OFF_KPE + QK_ROPE

LANES = 128
QK_PAD = 2 * LANES
LAT_COLS = Q_LORA + KV_LORA + LANES
VMEM_LIMIT = 58 * 1024 * 1024

F32 = jnp.float32
BF16 = jnp.bfloat16


def _params(*sem):
    return pltpu.CompilerParams(dimension_semantics=sem, vmem_limit_bytes=VMEM_LIMIT)


def _ada_kernel(c_ref, w_ref, b_ref, o_ref, cond_ref):
    c = c_ref[...]
    cond = (c * jax.nn.sigmoid(c)).astype(BF16)
    cond_ref[...] = cond
    o_ref[...] = jnp.dot(cond, w_ref[...].astype(BF16), preferred_element_type=F32) + b_ref[...]


def _ada(c8, w_ada, b_ada, n, tn=1024):
    return pl.pallas_call(
        _ada_kernel,
        out_shape=(jax.ShapeDtypeStruct((8, n), F32), jax.ShapeDtypeStruct((8, D_MODEL), BF16)),
        grid=(n // tn,),
        in_specs=[
            pl.BlockSpec((8, D_MODEL), lambda j: (0, 0)),
            pl.BlockSpec((D_MODEL, tn), lambda j: (0, j)),
            pl.BlockSpec((1, tn), lambda j: (0, j)),
        ],
        out_specs=(pl.BlockSpec((8, tn), lambda j: (0, j)), pl.BlockSpec((8, D_MODEL), lambda j: (0, 0))),
        compiler_params=_params("arbitrary"),
        name="ada",
    )(c8, w_ada, b_ada)


def _normmod_kernel(x_ref, g_ref, sh_ref, sc_ref, *refs):
    o_ref = refs[-1] if len(refs) == 1 else refs[2]
    x = x_ref[...]
    y = x * lax.rsqrt(jnp.mean(x * x, axis=-1, keepdims=True) + EPS)
    y = y * g_ref[...]
    o_ref[...] = (y * (1.0 + sc_ref[0, 0]) + sh_ref[0, 0]).astype(o_ref.dtype)
    if len(refs) > 1:
        pos_ref, freq_ref, _, c_ref, s_ref = refs
        _rope_tab_kernel(pos_ref, freq_ref, c_ref, s_ref)


def _mod_spec(idx, tm, width, ncol_arg=False):
    per_batch = SEQ // tm
    if ncol_arg:
        return pl.BlockSpec((1, 1, 1, width), lambda m, n, *_: (m // per_batch, idx, 0, n))
    return pl.BlockSpec((1, 1, 1, width), lambda m, *_: (m // per_batch, idx, 0, 0))


def _normmod(x2, g, mod4, shift_idx, scale_idx, tm=512, rope_in=None):
    t, d = x2.shape
    rows = lambda width: pl.BlockSpec((tm, width), lambda m: (m, 0))
    in_specs = [rows(d), pl.BlockSpec((1, d), lambda m: (0, 0)),
                _mod_spec(shift_idx, tm, d), _mod_spec(scale_idx, tm, d)]
    out_shape, out_specs, extra = [jax.ShapeDtypeStruct((t, d), BF16)], [rows(d)], ()
    if rope_in is not None:
        extra = rope_in
        in_specs += [rows(1), pl.BlockSpec((1, LANES), lambda m: (0, 0))]
        out_shape += [jax.ShapeDtypeStruct((t, LANES), F32)] * 2
        out_specs += [rows(LANES)] * 2
    out = pl.pallas_call(
        _normmod_kernel,
        out_shape=out_shape,
        grid=(t // tm,),
        in_specs=in_specs,
        out_specs=out_specs,
        compiler_params=_params("parallel"),
        name="normmod",
    )(x2, g, mod4, mod4, *extra)
    return out if rope_in is not None else out[0]


def _gelu(v):
    return 0.5 * v * (1.0 + lax.erf(v * math.sqrt(0.5)))


def _relu2(v):
    r = jnp.maximum(v, 0.0)
    return r * r


def _sigmoid(v):
    return 0.5 + 0.5 * jnp.tanh(0.5 * v)


_ACTS = {"gelu": _gelu, "sigmoid": _sigmoid, "relu2": _relu2, "none": lambda v: v}


def _cast_specs(jobs, n_steps, step_of):
    in_specs, out_specs, shapes, operands = [], [], [], []

    def rows_job(a):
        rows, cols = a.shape
        assert rows % (n_steps * 16) == 0
        spec = pl.BlockSpec((rows // n_steps, cols), lambda *ids: (step_of(*ids), 0))
        return a, spec, spec, a.shape

    def transposed_job(w_t, start, width):
        d, cols = w_t.shape[1], width // n_steps
        assert width % (n_steps * LANES) == 0
        src = pl.BlockSpec(
            (pl.Element(cols), pl.Element(d)),
            lambda *ids: (pl.multiple_of(start + step_of(*ids) * cols, math.gcd(start, cols)), 0))
        return w_t, src, pl.BlockSpec((d, cols), lambda *ids: (0, step_of(*ids))), (d, width)

    def widen_heads_job(w_uq):
        rows = w_uq.shape[0] // n_steps
        assert w_uq.shape[0] % (n_steps * 16) == 0
        index = lambda *ids: (step_of(*ids), 0)
        wide = MLA_HEADS * QK_PAD
        return (w_uq, pl.BlockSpec((rows, w_uq.shape[1]), index), pl.BlockSpec((rows, wide), index),
                (w_uq.shape[0], wide))

    for job in jobs:
        if not isinstance(job, tuple):
            a, src, dst, shape = rows_job(job)
        elif isinstance(job[0], str):
            a, src, dst, shape = widen_heads_job(job[1])
        else:
            a, src, dst, shape = transposed_job(*job)
        operands.append(a)
        in_specs.append(src)
        out_specs.append(dst)
        shapes.append(jax.ShapeDtypeStruct(shape, BF16))
    return in_specs, out_specs, shapes, operands


def _widen_heads(w, o_ref):
    for h in range(MLA_HEADS):
        src, dst = h * QK_HEAD, h * QK_PAD
        o_ref[:, dst:dst + QK_HEAD] = w[:, src:src + QK_HEAD]
        o_ref[:, dst + QK_HEAD:dst + QK_PAD] = w[:, src + QK_NOPE:src + QK_HEAD]


def _do_casts(refs, n):
    for src, dst in zip(refs[:n], refs[len(refs) - n:]):
        blk = src[...]
        if src.shape == dst.shape:
            dst[...] = blk.astype(BF16)
        elif src.shape[0] == dst.shape[0]:
            _widen_heads(blk.astype(BF16), dst)
        else:
            dst[...] = blk.T.astype(BF16)


def _mm_act_kernel(x_ref, w_ref, *refs, act, n_cast):
    o_ref = refs[n_cast]
    acc = jnp.dot(x_ref[...], w_ref[...], preferred_element_type=F32)
    o_ref[...] = _ACTS[act](acc).astype(o_ref.dtype)
    _do_casts(refs, n_cast)


def _mm_act(x, w, act, out_dtype, tm, tn, name, casts=()):
    t, k = x.shape
    n = w.shape[1]
    gm, gn = t // tm, n // tn
    cast_in, cast_out, cast_shapes, cast_ops = _cast_specs(casts, gm * gn, lambda m, j: m * gn + j)
    out = pl.pallas_call(
        functools.partial(_mm_act_kernel, act=act, n_cast=len(casts)),
        out_shape=[jax.ShapeDtypeStruct((t, n), out_dtype)] + cast_shapes,
        grid=(gm, gn),
        in_specs=[
            pl.BlockSpec((tm, k), lambda m, j: (m, 0)),
            pl.BlockSpec((k, tn), lambda m, j: (0, j)),
        ] + cast_in,
        out_specs=[pl.BlockSpec((tm, tn), lambda m, j: (m, j))] + cast_out,
        compiler_params=_params("parallel", "parallel"),
        name=name,
    )(x, w, *cast_ops)
    return out if casts else out[0]


def _gmlp_kernel(u_ref, v_ref, gv_ref, ws_ref, bias_ref, o_ref, *, n_chunks):
    row = lax.broadcasted_iota(jnp.int32, (CHUNK, CHUNK), 0)
    col = lax.broadcasted_iota(jnp.int32, (CHUNK, CHUNK), 1)
    causal = col <= row
    gv = gv_ref[...]
    for ci in range(n_chunks):
        rows = pl.ds(ci * CHUNK, CHUNK)
        v = v_ref[rows, :].astype(F32)
        vn = v * lax.rsqrt(jnp.mean(v * v, axis=-1, keepdims=True) + EPS) * gv
        vn = vn.astype(BF16)
        for g in range(GM_GROUPS):
            cols = slice(g * GM_GROUP_W, (g + 1) * GM_GROUP_W)
            w = jnp.where(causal, ws_ref[g], 0.0).astype(BF16)
            mixed = jnp.dot(w, vn[:, cols], preferred_element_type=F32) + bias_ref[:, cols]
            o_ref[rows, cols] = (u_ref[rows, cols].astype(F32) * mixed).astype(o_ref.dtype)


def _gmlp(z, g_v, w_s, bias_full, tm=1024):
    t = z.shape[0]
    return pl.pallas_call(
        functools.partial(_gmlp_kernel, n_chunks=tm // CHUNK),
        out_shape=jax.ShapeDtypeStruct((t, GM_WIDTH), BF16),
        grid=(t // tm,),
        in_specs=[
            pl.BlockSpec((tm, GM_WIDTH), lambda m: (m, 0)),
            pl.BlockSpec((tm, GM_WIDTH), lambda m: (m, 1)),
            pl.BlockSpec((1, GM_WIDTH), lambda m: (0, 0)),
            pl.BlockSpec((GM_GROUPS, CHUNK, CHUNK), lambda m: (0, 0, 0)),
            pl.BlockSpec((CHUNK, GM_WIDTH), lambda m: (0, 0)),
        ],
        out_specs=pl.BlockSpec((tm, GM_WIDTH), lambda m: (m, 0)),
        compiler_params=_params("parallel"),
        name="gmlp",
    )(z, z, g_v, w_s, bias_full)


def _rope_tab_kernel(pos_ref, freq_ref, c_ref, s_ref):
    ang = pos_ref[...].astype(F32) * freq_ref[...]
    lane = lax.broadcasted_iota(jnp.int32, ang.shape, 1)
    cos = jnp.cos(ang)
    sin = jnp.sin(ang)
    c_ref[...] = jnp.where(lane < QK_ROPE, cos, 0.0)
    s_ref[...] = jnp.where(lane < QK_ROPE // 2, -sin, jnp.where(lane < QK_ROPE, sin, 0.0))


def _rope(v, c, s):
    return v * c + pltpu.roll(v, QK_ROPE // 2, 1) * s


def _rms_rows(x, g):
    return x * lax.rsqrt(jnp.mean(x * x, axis=-1, keepdims=True) + EPS) * g


def _qprep_kernel(lat_ref, glat_ref, w_ref, g1_ref, g2_ref, c_ref, s_ref, o_ref):
    xn = _rms_rows(lat_ref[...], glat_ref[...]).astype(BF16)
    c, s = c_ref[...], s_ref[...]
    g1, g2 = g1_ref[...], g2_ref[...]
    scale = QK_HEAD ** -0.5 * math.log2(math.e)
    for h in range(MLA_HEADS):
        acc = jnp.dot(xn, w_ref[:, h * QK_PAD:(h + 1) * QK_PAD], preferred_element_type=F32)
        a = acc[:, :LANES]
        b = acc[:, LANES:]
        ssq = jnp.sum(a * a + 0.5 * (b * b), axis=-1, keepdims=True)
        r = lax.rsqrt(ssq * (1.0 / QK_HEAD) + EPS) * scale
        o_ref[:, h * QK_PAD:h * QK_PAD + LANES] = (a * r * g1).astype(o_ref.dtype)
        o_ref[:, h * QK_PAD + LANES:(h + 1) * QK_PAD] = _rope(b * r * g2, c, s).astype(o_ref.dtype)


def _qprep(lat, g_q_lat, wq, g1, g2, c, s, tm=512):
    t = lat.shape[0]
    tn = MLA_HEADS * QK_PAD
    row = lambda m: (m, 0)
    const = lambda m: (0, 0)
    return pl.pallas_call(
        _qprep_kernel,
        out_shape=jax.ShapeDtypeStruct((t, tn), BF16),
        grid=(t // tm,),
        in_specs=[
            pl.BlockSpec((tm, Q_LORA), row),
            pl.BlockSpec((1, Q_LORA), const),
            pl.BlockSpec((Q_LORA, tn), const, pipeline_mode=pl.Buffered(1)),
            pl.BlockSpec((1, LANES), const),
            pl.BlockSpec((1, LANES), const),
            pl.BlockSpec((tm, LANES), row),
            pl.BlockSpec((tm, LANES), row),
        ],
        out_specs=pl.BlockSpec((tm, tn), row),
        compiler_params=_params("parallel"),
        name="q_prep",
    )(lat, g_q_lat, wq, g1, g2, c, s)


def _kvprep_kernel(lat_ref, kpe_ref, glat_ref, w_ref, g1_ref, g2_ref, c_ref, s_ref,
                   k_ref, v_ref):
    xn = _rms_rows(lat_ref[...], glat_ref[...]).astype(BF16)
    kpe = kpe_ref[...]
    ssq_pe = 0.5 * jnp.sum(kpe * kpe, axis=-1, keepdims=True)
    kr = _rope(kpe * g2_ref[...], c_ref[...], s_ref[...])
    g1 = g1_ref[...]
    width = QK_NOPE + V_HEAD
    for h in range(MLA_HEADS):
        acc = jnp.dot(xn, w_ref[:, h * width:(h + 1) * width], preferred_element_type=F32)
        a = acc[:, :QK_NOPE]
        ssq = jnp.sum(a * a, axis=-1, keepdims=True) + ssq_pe
        r = lax.rsqrt(ssq * (1.0 / QK_HEAD) + EPS)
        k_ref[:, h * QK_PAD:h * QK_PAD + LANES] = (a * r * g1).astype(k_ref.dtype)
        k_ref[:, h * QK_PAD + LANES:(h + 1) * QK_PAD] = (kr * r).astype(k_ref.dtype)
        v_ref[:, h * V_HEAD:(h + 1) * V_HEAD] = acc[:, QK_NOPE:].astype(v_ref.dtype)


def _kvprep(lat, g_kv_lat, wkv, g1, g2, c, s, tm=512):
    t = lat.shape[0]
    row = lambda m: (m, 0)
    const = lambda m: (0, 0)
    return pl.pallas_call(
        _kvprep_kernel,
        out_shape=(jax.ShapeDtypeStruct((t, MLA_HEADS * QK_PAD), BF16),
                   jax.ShapeDtypeStruct((t, MLA_HEADS * V_HEAD), BF16)),
        grid=(t // tm,),
        in_specs=[
            pl.BlockSpec((tm, KV_LORA), lambda m: (m, Q_LORA // KV_LORA)),
            pl.BlockSpec((tm, LANES), lambda m: (m, (Q_LORA + KV_LORA) // LANES)),
            pl.BlockSpec((1, KV_LORA), const),
            pl.BlockSpec(wkv.shape, const, pipeline_mode=pl.Buffered(1)),
            pl.BlockSpec((1, LANES), const),
            pl.BlockSpec((1, LANES), const),
            pl.BlockSpec((tm, LANES), row),
            pl.BlockSpec((tm, LANES), row),
        ],
        out_specs=(pl.BlockSpec((tm, MLA_HEADS * QK_PAD), row),
                   pl.BlockSpec((tm, MLA_HEADS * V_HEAD), row)),
        compiler_params=_params("parallel"),
        name="kv_prep",
    )(lat, lat, g_kv_lat, wkv, g1, g2, c, s)


def _attn_kernel(q_ref, k_ref, v_ref, *refs, tq, tk, heads, n_cast, ada):
    n_in = n_cast + (3 if ada else 0)
    o_ref, m_ref, acc_ref = refs[n_in], refs[-2], refs[-1]
    _do_casts(refs[:n_cast] + refs[n_in + 1:n_in + 1 + n_cast], n_cast)
    if ada:
        cond_ref, wada_ref, bada_ref = refs[n_cast:n_in]
        mod_ref = refs[n_in + 1 + n_cast]
        mod_ref[...] = jnp.dot(cond_ref[...], wada_ref[...].astype(BF16),
                               preferred_element_type=F32) + bada_ref[...]
    i = pl.program_id(2)
    nc = tk // LANES
    ratio = tq // tk
    m_ref[...] = jnp.full(m_ref.shape, -jnp.inf, F32)
    acc_ref[...] = jnp.zeros(acc_ref.shape, F32)

    def block(j, r0, masked):
        keys = pl.ds(pl.multiple_of(j * tk, tk), tk)
        nr = tq - r0
        for h in range(heads):
            q = q_ref[r0:, h * QK_PAD:(h + 1) * QK_PAD]
            k = k_ref[keys, h * QK_PAD:(h + 1) * QK_PAD]
            s = lax.dot_general(q, k, (((1,), (1,)), ((), ())), preferred_element_type=F32)
            if masked:
                row = lax.broadcasted_iota(jnp.int32, (nr, tk), 0)
                col = lax.broadcasted_iota(jnp.int32, (nr, tk), 1)
                s = jnp.where(col <= row, s, -jnp.inf)
            chunks = [s[:, c * LANES:(c + 1) * LANES] for c in range(nc)]
            m_part = functools.reduce(jnp.maximum, chunks)
            m_prev = m_ref[h, r0:, :]
            m_cur = jnp.broadcast_to(jnp.max(m_part, axis=1, keepdims=True), m_part.shape)
            m_new = jnp.maximum(m_prev, m_cur)
            alpha = jnp.exp2(m_prev - m_new)
            m_ref[h, r0:, :] = m_new
            p = jnp.concatenate([jnp.exp2((ch - m_new).astype(BF16)) for ch in chunks], axis=1)
            v = v_ref[keys, h * V_HEAD:(h + 1) * V_HEAD]
            v1 = jnp.concatenate([v, jnp.ones_like(v)], axis=1)
            alpha2 = jnp.concatenate([alpha, alpha], axis=1)
            acc_ref[h, r0:, :] = alpha2 * acc_ref[h, r0:, :] + jnp.dot(p, v1, preferred_element_type=F32)

    def body(j, carry):
        block(j, 0, False)
        return carry

    if tq < SEQ:
        lax.fori_loop(0, ratio * i, body, 0)
    for d in range(ratio):
        block(ratio * i + d, d * tk, True)
    for h in range(heads):
        acc = acc_ref[h]
        o_ref[:, h * V_HEAD:(h + 1) * V_HEAD] = (acc[:, :V_HEAD] / acc[:, V_HEAD:]).astype(o_ref.dtype)


def _attention(q, k, v, batch, tq=2048, tk=256, heads=2, casts=(), ada=None):
    t = q.shape[0]
    nq = SEQ // tq
    gh = MLA_HEADS // heads
    n_steps = batch * gh * nq
    step_of = lambda b, h, i: (b * gh + h) * nq + i
    cast_in, cast_out, cast_shapes, cast_ops = _cast_specs(casts, n_steps, step_of)
    ada_in, ada_out, ada_shapes, ada_ops = [], [], [], []
    if ada is not None:
        cond, w_ada, b_ada, first_col = ada
        width = (w_ada.shape[1] - first_col) // n_steps
        assert (w_ada.shape[1] - first_col) % (n_steps * LANES) == 0 and first_col % width == 0
        src = lambda b, h, i: (0, first_col // width + step_of(b, h, i))
        ada_in = [pl.BlockSpec(cond.shape, lambda b, h, i: (0, 0)),
                  pl.BlockSpec((w_ada.shape[0], width), src), pl.BlockSpec((1, width), src)]
        ada_out = [pl.BlockSpec((cond.shape[0], width), lambda b, h, i: (0, step_of(b, h, i)))]
        ada_shapes = [jax.ShapeDtypeStruct((cond.shape[0], w_ada.shape[1] - first_col), F32)]
        ada_ops = [cond, w_ada, b_ada]
    out = pl.pallas_call(
        functools.partial(_attn_kernel, tq=tq, tk=tk, heads=heads, n_cast=len(casts), ada=ada is not None),
        out_shape=[jax.ShapeDtypeStruct((t, MLA_HEADS * V_HEAD), BF16)] + cast_shapes + ada_shapes,
        grid=(batch, gh, nq),
        in_specs=[
            pl.BlockSpec((tq, heads * QK_PAD), lambda b, h, i: (b * nq + i, h)),
            pl.BlockSpec((SEQ, heads * QK_PAD), lambda b, h, i: (b, h)),
            pl.BlockSpec((SEQ, heads * V_HEAD), lambda b, h, i: (b, h)),
        ] + cast_in + ada_in,
        out_specs=[pl.BlockSpec((tq, heads * V_HEAD), lambda b, h, i: (b * nq + i, h))] + cast_out + ada_out,
        scratch_shapes=[pltpu.VMEM((heads, tq, LANES), F32), pltpu.VMEM((heads, tq, 2 * V_HEAD), F32)],
        compiler_params=_params("parallel", "parallel", "arbitrary"),
        name="attention",
    )(q, k, v, *cast_ops, *ada_ops)
    return out if (casts or ada is not None) else out[0]


def _branch_kernel(a_ref, wa_ref, b_ref, wb_ref, ga_ref, gb_ref, o_ref):
    ya = jnp.dot(a_ref[...], wa_ref[...], preferred_element_type=F32)
    yb = jnp.dot(b_ref[...], wb_ref[...], preferred_element_type=F32)
    o_ref[...] = (ga_ref[...].astype(F32) * ya + gb_ref[...].astype(F32) * yb).astype(o_ref.dtype)


def _branch(a, wa, b, wb, gates, tm=1024, tn=512):
    t = a.shape[0]
    nb = D_MODEL // tn
    return pl.pallas_call(
        _branch_kernel,
        out_shape=jax.ShapeDtypeStruct((t, D_MODEL), BF16),
        grid=(t // tm, nb),
        in_specs=[
            pl.BlockSpec((tm, a.shape[1]), lambda m, j: (m, 0)),
            pl.BlockSpec((wa.shape[0], tn), lambda m, j: (0, j)),
            pl.BlockSpec((tm, b.shape[1]), lambda m, j: (m, 0)),
            pl.BlockSpec((wb.shape[0], tn), lambda m, j: (0, j)),
            pl.BlockSpec((tm, tn), lambda m, j: (m, j)),
            pl.BlockSpec((tm, tn), lambda m, j: (m, nb + j)),
        ],
        out_specs=pl.BlockSpec((tm, tn), lambda m, j: (m, j)),
        compiler_params=_params("parallel", "parallel"),
        name="branch_merge",
    )(a, wa, b, wb, gates, gates)


def _mm_resid_kernel(x_ref, w_ref, res_ref, gate_ref, *refs, nk, n_cast):
    o_ref = refs[n_cast]
    _do_casts(refs, n_cast)
    kk = pl.program_id(2)
    if nk == 1:
        part = jnp.dot(x_ref[...], w_ref[...], preferred_element_type=F32)
        o_ref[...] = res_ref[...] + gate_ref[0, 0] * part
        return

    @pl.when(kk == 0)
    def _():
        o_ref[...] = jnp.zeros(o_ref.shape, F32)

    o_ref[...] += jnp.dot(x_ref[...], w_ref[...], preferred_element_type=F32)

    @pl.when(kk == nk - 1)
    def _():
        o_ref[...] = res_ref[...] + gate_ref[0, 0] * o_ref[...]


def _mm_resid(x, w, res, mod3, gate_idx, tm, tn, tk, name, casts=()):
    t, k = x.shape
    n = w.shape[1]
    gm, gn, nk = t // tm, n // tn, k // tk
    cast_in, cast_out, cast_shapes, cast_ops = _cast_specs(
        casts, gm * gn * nk, lambda m, j, kk: (m * gn + j) * nk + kk)
    out = pl.pallas_call(
        functools.partial(_mm_resid_kernel, nk=nk, n_cast=len(casts)),
        out_shape=[jax.ShapeDtypeStruct((t, n), F32)] + cast_shapes,
        grid=(gm, gn, nk),
        in_specs=[
            pl.BlockSpec((tm, tk), lambda m, j, kk: (m, kk)),
            pl.BlockSpec((tk, tn), lambda m, j, kk: (kk, j)),
            pl.BlockSpec((tm, tn), lambda m, j, kk: (m, j)),
            _mod_spec(gate_idx, tm, tn, ncol_arg=True),
        ] + cast_in,
        out_specs=[pl.BlockSpec((tm, tn), lambda m, j, kk: (m, j))] + cast_out,
        compiler_params=_params("parallel", "parallel", "arbitrary"),
        name=name,
    )(x, w, res, mod3, *cast_ops)
    return out if casts else out[0]


def _repack_kernel(wt_ref, o_ref, *, dup_tail):
    blk = wt_ref[...]
    if dup_tail:
        last = pl.program_id(0) == pl.num_programs(0) - 1
        half = blk.shape[0] // 2
        blk = jnp.where(last, jnp.concatenate([blk[:half], blk[:half]], axis=0), blk)
    o_ref[...] = blk.T.astype(BF16)


def _repack_w_in(w_in_t, start, width, cols, name, dup_tail=False):
    d = w_in_t.shape[1]
    return pl.pallas_call(
        functools.partial(_repack_kernel, dup_tail=dup_tail),
        out_shape=jax.ShapeDtypeStruct((d, width), BF16),
        grid=(width // cols,),
        in_specs=[pl.BlockSpec((pl.Element(cols), pl.Element(d)),
                               lambda i: (pl.multiple_of(start + i * cols, math.gcd(start, cols)), 0))],
        out_specs=pl.BlockSpec((d, cols), lambda i: (0, i)),
        compiler_params=_params("parallel"),
        name=name,
    )(w_in_t)


def _pad_cols(a, width):
    return jnp.pad(a, ((0, 0), (0, width - a.shape[1])))


def _layer(x2, batch, c8, rope_in, tables, w_ada, b_ada, g_norm1, w_in, g_v, w_s, b_s, g_q_lat, g_kv_lat,
           w_uq, w_ukv, g_qn, g_kn, w_branch_a, w_branch_b, w_out, g_norm2, w_ff1, w_ff2):
    d = x2.shape[1]
    b_ada = b_ada.reshape(1, -1)
    mod_a, cond = _ada(c8, w_ada, b_ada, 2 * d)
    mod_a = mod_a[:batch].reshape(batch, 2, 1, d)
    w_in_t = w_in.T
    w_uv = _repack_w_in(w_in_t, 0, OFF_Q, 512, "repack_uv")
    w_lat = _repack_w_in(w_in_t, OFF_Q, LAT_COLS, LANES, "repack_lat", dup_tail=True)
    bias_full = jnp.repeat(b_s.T, GM_GROUP_W, axis=1)
    row = lambda v: v.reshape(1, -1)
    twice = lambda v: jnp.concatenate([v, v]).reshape(1, -1)
    qg1, qg2 = row(g_qn[:QK_NOPE]), twice(g_qn[QK_NOPE:])
    kg1, kg2 = row(g_kn[:QK_NOPE]), twice(g_kn[QK_NOPE:])

    if tables is None:
        h1, *tables = _normmod(x2, row(g_norm1), mod_a, 0, 1, rope_in=rope_in)
    else:
        h1 = _normmod(x2, row(g_norm1), mod_a, 0, 1)
    c_tab, s_tab = tables
    z, w_gate, wq = _mm_act(h1, w_uv, "gelu", BF16, 1024, 512, "in_proj_uv",
                            casts=((w_in_t, OFF_GATE, w_in.shape[1] - OFF_GATE), ("widen_heads", w_uq)))
    lat, wkv = _mm_act(h1, w_lat, "none", F32, 512, LAT_COLS, "in_proj_lat", casts=(w_ukv,))
    gates, w1 = _mm_act(h1, w_gate, "sigmoid", BF16, 1024, 1024, "in_proj_gate", casts=(w_ff1,))
    gm = _gmlp(z, row(g_v), w_s, bias_full)
    q = _qprep(lat, row(g_q_lat), wq, qg1, qg2, c_tab, s_tab)
    k, v = _kvprep(lat, row(g_kv_lat), wkv, kg1, kg2, c_tab, s_tab)
    att, wa, wb, wo, mod_b = _attention(q, k, v, batch, casts=(w_branch_a, w_branch_b, w_out),
                                        ada=(cond, w_ada, b_ada, 2 * d))
    mod_b = mod_b[:batch].reshape(batch, N_MOD - 2, 1, d)
    mixed = _branch(gm, wa, att, wb, gates)
    x2 = _mm_resid(mixed, wo, x2, mod_b, 0, 1024, 1024, D_MODEL, "out_proj")

    h2 = _normmod(x2, row(g_norm2), mod_b, 1, 2)
    hid, w2 = _mm_act(h2, w1, "relu2", BF16, 1024, 1024, "ff1", casts=(w_ff2,))
    return _mm_resid(hid, w2, x2, mod_b, 3, 1024, 1024, 4096, "ff2"), tables


def kernel(x, c, positions, w_ada, b_ada, g_norm1, w_in, g_v, w_s, b_s, g_q_lat, g_kv_lat, w_uq, w_ukv,
           g_qn, g_kn, w_branch_a, w_branch_b, w_out, g_norm2, w_ff1, w_ff2):
    batch, seq, d = x.shape
    assert (seq, d) == (SEQ, D_MODEL)
    depth = w_ada.shape[0]
    t = batch * seq
    x2 = x.reshape(t, d)

    inv_freq = 1.0 / (ROPE_THETA ** (jnp.arange(0, QK_ROPE, 2, dtype=F32) / QK_ROPE))
    freq_row = _pad_cols(jnp.concatenate([inv_freq, inv_freq]).reshape(1, QK_ROPE), LANES)
    rope_in = (positions.reshape(t, 1), freq_row)

    c8 = jnp.pad(c, ((0, 8 - batch), (0, 0)))
    tables = None
    for l in range(depth):
        x2, tables = _layer(x2, batch, c8, rope_in, tables, w_ada[l], b_ada[l], g_norm1[l], w_in[l], g_v[l],
                            w_s[l], b_s[l],
                            g_q_lat[l], g_kv_lat[l], w_uq[l], w_ukv[l], g_qn[l], g_kn[l], w_branch_a[l],
                            w_branch_b[l], w_out[l], g_norm2[l], w_ff1[l], w_ff2[l])
    return x2.reshape(batch, seq, d)
```

```python
import functools
import math

import jax
import jax.numpy as jnp
from jax import lax
from jax.experimental import pallas as pl
from jax.experimental.pallas import tpu as pltpu

D_MODEL = 4096
SEQ = 2048
GM_WIDTH = 2048
GM_GROUPS = 8
GM_GROUP_W = GM_WIDTH // GM_GROUPS
CHUNK = 128
MLA_HEADS = 32
QK_NOPE = 128
QK_ROPE = 64
QK_HEAD = QK_NOPE + QK_ROPE
V_HEAD = 128
Q_LORA = 1024
KV_LORA = 512
ROPE_THETA = 10000.0
D_FF = 4 * D_MODEL
N_MOD = 6
EPS = 1e-6
OFF_Q = 2 * GM_WIDTH
OFF_KV = OFF_Q + Q_LORA
OFF_KPE = OFF_KV + KV_LORA
OFF_GATE = OFF_KPE + QK_ROPE

LANES = 128
QK_PAD = 2 * LANES
LAT_COLS = Q_LORA + KV_LORA + LANES
VMEM_LIMIT = 58 * 1024 * 1024

F32 = jnp.float32
BF16 = jnp.bfloat16


def _params(*sem):
    return pltpu.CompilerParams(dimension_semantics=sem, vmem_limit_bytes=VMEM_LIMIT)


def _ada_kernel(c_ref, w_ref, b_ref, o_ref, cond_ref):
    c = c_ref[...]
    cond = (c * jax.nn.sigmoid(c)).astype(BF16)
    cond_ref[...] = cond
    o_ref[...] = jnp.dot(cond, w_ref[...].astype(BF16), preferred_element_type=F32) + b_ref[...]


def _ada(c8, w_ada, b_ada, n, tn=1024):
    return pl.pallas_call(
        _ada_kernel,
        out_shape=(jax.ShapeDtypeStruct((8, n), F32), jax.ShapeDtypeStruct((8, D_MODEL), BF16)),
        grid=(n // tn,),
        in_specs=[
            pl.BlockSpec((8, D_MODEL), lambda j: (0, 0)),
            pl.BlockSpec((D_MODEL, tn), lambda j: (0, j)),
            pl.BlockSpec((1, tn), lambda j: (0, j)),
        ],
        out_specs=(pl.BlockSpec((8, tn), lambda j: (0, j)), pl.BlockSpec((8, D_MODEL), lambda j: (0, 0))),
        compiler_params=_params("arbitrary"),
        name="ada",
    )(c8, w_ada, b_ada)


def _normmod_kernel(x_ref, g_ref, sh_ref, sc_ref, *refs):
    o_ref = refs[-1] if len(refs) == 1 else refs[2]
    x = x_ref[...]
    y = x * lax.rsqrt(jnp.mean(x * x, axis=-1, keepdims=True) + EPS)
    y = y * g_ref[...]
    o_ref[...] = (y * (1.0 + sc_ref[0, 0]) + sh_ref[0, 0]).astype(o_ref.dtype)
    if len(refs) > 1:
        pos_ref, freq_ref, _, c_ref, s_ref = refs
        _rope_tab_kernel(pos_ref, freq_ref, c_ref, s_ref)


def _mod_spec(idx, tm, width, ncol_arg=False):
    per_batch = SEQ // tm
    if ncol_arg:
        return pl.BlockSpec((1, 1, 1, width), lambda m, n, *_: (m // per_batch, idx, 0, n))
    return pl.BlockSpec((1, 1, 1, width), lambda m, *_: (m // per_batch, idx, 0, 0))


def _normmod(x2, g, mod4, shift_idx, scale_idx, tm=512, rope_in=None):
    t, d = x2.shape
    rows = lambda width: pl.BlockSpec((tm, width), lambda m: (m, 0))
    in_specs = [rows(d), pl.BlockSpec((1, d), lambda m: (0, 0)),
                _mod_spec(shift_idx, tm, d), _mod_spec(scale_idx, tm, d)]
    out_shape, out_specs, extra = [jax.ShapeDtypeStruct((t, d), BF16)], [rows(d)], ()
    if rope_in is not None:
        extra = rope_in
        in_specs += [rows(1), pl.BlockSpec((1, LANES), lambda m: (0, 0))]
        out_shape += [jax.ShapeDtypeStruct((t, LANES), F32)] * 2
        out_specs += [rows(LANES)] * 2
    out = pl.pallas_call(
        _normmod_kernel,
        out_shape=out_shape,
        grid=(t // tm,),
        in_specs=in_specs,
        out_specs=out_specs,
        compiler_params=_params("parallel"),
        name="normmod",
    )(x2, g, mod4, mod4, *extra)
    return out if rope_in is not None else out[0]


def _gelu(v):
    return 0.5 * v * (1.0 + lax.erf(v * math.sqrt(0.5)))


def _relu2(v):
    r = jnp.maximum(v, 0.0)
    return r * r


def _sigmoid(v):
    return 0.5 + 0.5 * jnp.tanh(0.5 * v)


_ACTS = {"gelu": _gelu, "sigmoid": _sigmoid, "relu2": _relu2, "none": lambda v: v}


def _cast_specs(jobs, n_steps, step_of):
    in_specs, out_specs, shapes, operands = [], [], [], []

    def rows_job(a):
        rows, cols = a.shape
        assert rows % (n_steps * 16) == 0
        spec = pl.BlockSpec((rows // n_steps, cols), lambda *ids: (step_of(*ids), 0))
        return a, spec, spec, a.shape

    def transposed_job(w_t, start, width):
        d, cols = w_t.shape[1], width // n_steps
        assert width % (n_steps * LANES) == 0
        src = pl.BlockSpec(
            (pl.Element(cols), pl.Element(d)),
            lambda *ids: (pl.multiple_of(start + step_of(*ids) * cols, math.gcd(start, cols)), 0))
        return w_t, src, pl.BlockSpec((d, cols), lambda *ids: (0, step_of(*ids))), (d, width)

    def widen_heads_job(w_uq):
        rows = w_uq.shape[0] // n_steps
        assert w_uq.shape[0] % (n_steps * 16) == 0
        index = lambda *ids: (step_of(*ids), 0)
        wide = MLA_HEADS * QK_PAD
        return (w_uq, pl.BlockSpec((rows, w_uq.shape[1]), index), pl.BlockSpec((rows, wide), index),
                (w_uq.shape[0], wide))

    for job in jobs:
        if not isinstance(job, tuple):
            a, src, dst, shape = rows_job(job)
        elif isinstance(job[0], str):
            a, src, dst, shape = widen_heads_job(job[1])
        else:
            a, src, dst, shape = transposed_job(*job)
        operands.append(a)
        in_specs.append(src)
        out_specs.append(dst)
        shapes.append(jax.ShapeDtypeStruct(shape, BF16))
    return in_specs, out_specs, shapes, operands


def _widen_heads(w, o_ref):
    for h in range(MLA_HEADS):
        src, dst = h * QK_HEAD, h * QK_PAD
        o_ref[:, dst:dst + QK_HEAD] = w[:, src:src + QK_HEAD]
        o_ref[:, dst + QK_HEAD:dst + QK_PAD] = w[:, src + QK_NOPE:src + QK_HEAD]


def _do_casts(refs, n):
    for src, dst in zip(refs[:n], refs[len(refs) - n:]):
        blk = src[...]
        if src.shape == dst.shape:
            dst[...] = blk.astype(BF16)
        elif src.shape[0] == dst.shape[0]:
            _widen_heads(blk.astype(BF16), dst)
        else:
            dst[...] = blk.T.astype(BF16)


def _mm_act_kernel(x_ref, w_ref, *refs, act, n_cast, w_transposed):
    o_ref = refs[n_cast]
    if w_transposed:
        acc = lax.dot_general(x_ref[...], w_ref[...].astype(BF16), (((1,), (1,)), ((), ())),
                              preferred_element_type=F32)
    else:
        acc = jnp.dot(x_ref[...], w_ref[...], preferred_element_type=F32)
    o_ref[...] = _ACTS[act](acc).astype(o_ref.dtype)
    _do_casts(refs, n_cast)


def _mm_act(x, w, act, out_dtype, tm, tn, name, casts=(), w_transposed_cols=None):
    t, k = x.shape
    n = w.shape[1] if w_transposed_cols is None else w_transposed_cols
    gm, gn = t // tm, n // tn
    w_spec = (pl.BlockSpec((k, tn), lambda m, j: (0, j)) if w_transposed_cols is None
              else pl.BlockSpec((tn, k), lambda m, j: (j, 0)))
    cast_in, cast_out, cast_shapes, cast_ops = _cast_specs(casts, gm * gn, lambda m, j: m * gn + j)
    out = pl.pallas_call(
        functools.partial(_mm_act_kernel, act=act, n_cast=len(casts),
                          w_transposed=w_transposed_cols is not None),
        out_shape=[jax.ShapeDtypeStruct((t, n), out_dtype)] + cast_shapes,
        grid=(gm, gn),
        in_specs=[pl.BlockSpec((tm, k), lambda m, j: (m, 0)), w_spec] + cast_in,
        out_specs=[pl.BlockSpec((tm, tn), lambda m, j: (m, j))] + cast_out,
        compiler_params=_params("parallel", "parallel"),
        name=name,
    )(x, w, *cast_ops)
    return out if casts else out[0]


def _gmlp_kernel(u_ref, v_ref, gv_ref, ws_ref, bias_ref, o_ref, *, n_chunks):
    row = lax.broadcasted_iota(jnp.int32, (CHUNK, CHUNK), 0)
    col = lax.broadcasted_iota(jnp.int32, (CHUNK, CHUNK), 1)
    causal = col <= row
    gv = gv_ref[...]
    for ci in range(n_chunks):
        rows = pl.ds(ci * CHUNK, CHUNK)
        v = v_ref[rows, :].astype(F32)
        vn = v * lax.rsqrt(jnp.mean(v * v, axis=-1, keepdims=True) + EPS) * gv
        vn = vn.astype(BF16)
        for g in range(GM_GROUPS):
            cols = slice(g * GM_GROUP_W, (g + 1) * GM_GROUP_W)
            w = jnp.where(causal, ws_ref[g], 0.0).astype(BF16)
            mixed = jnp.dot(w, vn[:, cols], preferred_element_type=F32) + bias_ref[:, cols]
            o_ref[rows, cols] = (u_ref[rows, cols].astype(F32) * mixed).astype(o_ref.dtype)


def _gmlp(z, g_v, w_s, bias_full, tm=1024):
    t = z.shape[0]
    return pl.pallas_call(
        functools.partial(_gmlp_kernel, n_chunks=tm // CHUNK),
        out_shape=jax.ShapeDtypeStruct((t, GM_WIDTH), BF16),
        grid=(t // tm,),
        in_specs=[
            pl.BlockSpec((tm, GM_WIDTH), lambda m: (m, 0)),
            pl.BlockSpec((tm, GM_WIDTH), lambda m: (m, 1)),
            pl.BlockSpec((1, GM_WIDTH), lambda m: (0, 0)),
            pl.BlockSpec((GM_GROUPS, CHUNK, CHUNK), lambda m: (0, 0, 0)),
            pl.BlockSpec((CHUNK, GM_WIDTH), lambda m: (0, 0)),
        ],
        out_specs=pl.BlockSpec((tm, GM_WIDTH), lambda m: (m, 0)),
        compiler_params=_params("parallel"),
        name="gmlp",
    )(z, z, g_v, w_s, bias_full)


def _rope_tab_kernel(pos_ref, freq_ref, c_ref, s_ref):
    ang = pos_ref[...].astype(F32) * freq_ref[...]
    lane = lax.broadcasted_iota(jnp.int32, ang.shape, 1)
    cos = jnp.cos(ang)
    sin = jnp.sin(ang)
    c_ref[...] = jnp.where(lane < QK_ROPE, cos, 0.0)
    s_ref[...] = jnp.where(lane < QK_ROPE // 2, -sin, jnp.where(lane < QK_ROPE, sin, 0.0))


def _rope(v, c, s):
    return v * c + pltpu.roll(v, QK_ROPE // 2, 1) * s


def _rms_rows(x, g):
    return x * lax.rsqrt(jnp.mean(x * x, axis=-1, keepdims=True) + EPS) * g


def _qprep_kernel(lat_ref, glat_ref, w_ref, g1_ref, g2_ref, c_ref, s_ref, o_ref):
    xn = _rms_rows(lat_ref[...], glat_ref[...]).astype(BF16)
    c, s = c_ref[...], s_ref[...]
    g1, g2 = g1_ref[...], g2_ref[...]
    scale = QK_HEAD ** -0.5 * math.log2(math.e)
    for h in range(MLA_HEADS):
        acc = jnp.dot(xn, w_ref[:, h * QK_PAD:(h + 1) * QK_PAD], preferred_element_type=F32)
        a = acc[:, :LANES]
        b = acc[:, LANES:]
        ssq = jnp.sum(a * a + 0.5 * (b * b), axis=-1, keepdims=True)
        r = lax.rsqrt(ssq * (1.0 / QK_HEAD) + EPS) * scale
        o_ref[:, h * QK_PAD:h * QK_PAD + LANES] = (a * r * g1).astype(o_ref.dtype)
        o_ref[:, h * QK_PAD + LANES:(h + 1) * QK_PAD] = _rope(b * r * g2, c, s).astype(o_ref.dtype)


def _qprep(lat, g_q_lat, wq, g1, g2, c, s, tm=512):
    t = lat.shape[0]
    tn = MLA_HEADS * QK_PAD
    row = lambda m: (m, 0)
    const = lambda m: (0, 0)
    return pl.pallas_call(
        _qprep_kernel,
        out_shape=jax.ShapeDtypeStruct((t, tn), BF16),
        grid=(t // tm,),
        in_specs=[
            pl.BlockSpec((tm, Q_LORA), row),
            pl.BlockSpec((1, Q_LORA), const),
            pl.BlockSpec((Q_LORA, tn), const, pipeline_mode=pl.Buffered(1)),
            pl.BlockSpec((1, LANES), const),
            pl.BlockSpec((1, LANES), const),
            pl.BlockSpec((tm, LANES), row),
            pl.BlockSpec((tm, LANES), row),
        ],
        out_specs=pl.BlockSpec((tm, tn), row),
        compiler_params=_params("parallel"),
        name="q_prep",
    )(lat, g_q_lat, wq, g1, g2, c, s)


def _kvprep_kernel(lat_ref, kpe_ref, glat_ref, w_ref, g1_ref, g2_ref, c_ref, s_ref,
                   k_ref, v_ref):
    xn = _rms_rows(lat_ref[...], glat_ref[...]).astype(BF16)
    kpe = kpe_ref[...]
    ssq_pe = 0.5 * jnp.sum(kpe * kpe, axis=-1, keepdims=True)
    kr = _rope(kpe * g2_ref[...], c_ref[...], s_ref[...])
    g1 = g1_ref[...]
    width = QK_NOPE + V_HEAD
    for h in range(MLA_HEADS):
        acc = jnp.dot(xn, w_ref[:, h * width:(h + 1) * width], preferred_element_type=F32)
        a = acc[:, :QK_NOPE]
        ssq = jnp.sum(a * a, axis=-1, keepdims=True) + ssq_pe
        r = lax.rsqrt(ssq * (1.0 / QK_HEAD) + EPS)
        k_ref[:, h * QK_PAD:h * QK_PAD + LANES] = (a * r * g1).astype(k_ref.dtype)
        k_ref[:, h * QK_PAD + LANES:(h + 1) * QK_PAD] = (kr * r).astype(k_ref.dtype)
        v_ref[:, h * V_HEAD:(h + 1) * V_HEAD] = acc[:, QK_NOPE:].astype(v_ref.dtype)


def _kvprep(lat, g_kv_lat, wkv, g1, g2, c, s, tm=512):
    t = lat.shape[0]
    row = lambda m: (m, 0)
    const = lambda m: (0, 0)
    return pl.pallas_call(
        _kvprep_kernel,
        out_shape=(jax.ShapeDtypeStruct((t, MLA_HEADS * QK_PAD), BF16),
                   jax.ShapeDtypeStruct((t, MLA_HEADS * V_HEAD), BF16)),
        grid=(t // tm,),
        in_specs=[
            pl.BlockSpec((tm, KV_LORA), lambda m: (m, Q_LORA // KV_LORA)),
            pl.BlockSpec((tm, LANES), lambda m: (m, (Q_LORA + KV_LORA) // LANES)),
            pl.BlockSpec((1, KV_LORA), const),
            pl.BlockSpec(wkv.shape, const, pipeline_mode=pl.Buffered(1)),
            pl.BlockSpec((1, LANES), const),
            pl.BlockSpec((1, LANES), const),
            pl.BlockSpec((tm, LANES), row),
            pl.BlockSpec((tm, LANES), row),
        ],
        out_specs=(pl.BlockSpec((tm, MLA_HEADS * QK_PAD), row),
                   pl.BlockSpec((tm, MLA_HEADS * V_HEAD), row)),
        compiler_params=_params("parallel"),
        name="kv_prep",
    )(lat, lat, g_kv_lat, wkv, g1, g2, c, s)


def _attn_kernel(q_ref, k_ref, v_ref, *refs, tq, tk, heads, n_cast, ada):
    n_in = n_cast + (3 if ada else 0)
    o_ref, m_ref, acc_ref = refs[n_in], refs[-2], refs[-1]
    _do_casts(refs[:n_cast] + refs[n_in + 1:n_in + 1 + n_cast], n_cast)
    if ada:
        cond_ref, wada_ref, bada_ref = refs[n_cast:n_in]
        mod_ref = refs[n_in + 1 + n_cast]
        mod_ref[...] = jnp.dot(cond_ref[...], wada_ref[...].astype(BF16),
                               preferred_element_type=F32) + bada_ref[...]
    i = pl.program_id(2)
    nc = tk // LANES
    ratio = tq // tk
    m_ref[...] = jnp.full(m_ref.shape, -jnp.inf, F32)
    acc_ref[...] = jnp.zeros(acc_ref.shape, F32)

    def block(j, r0, masked):
        keys = pl.ds(pl.multiple_of(j * tk, tk), tk)
        nr = tq - r0
        for h in range(heads):
            q = q_ref[r0:, h * QK_PAD:(h + 1) * QK_PAD]
            k = k_ref[keys, h * QK_PAD:(h + 1) * QK_PAD]
            s = lax.dot_general(q, k, (((1,), (1,)), ((), ())), preferred_element_type=F32)
            if masked:
                row = lax.broadcasted_iota(jnp.int32, (nr, tk), 0)
                col = lax.broadcasted_iota(jnp.int32, (nr, tk), 1)
                s = jnp.where(col <= row, s, -jnp.inf)
            chunks = [s[:, c * LANES:(c + 1) * LANES] for c in range(nc)]
            m_part = functools.reduce(jnp.maximum, chunks)
            m_prev = m_ref[h, r0:, :]
            m_cur = jnp.broadcast_to(jnp.max(m_part, axis=1, keepdims=True), m_part.shape)
            m_new = jnp.maximum(m_prev, m_cur)
            alpha = jnp.exp2(m_prev - m_new)
            m_ref[h, r0:, :] = m_new
            p = jnp.concatenate([jnp.exp2((ch - m_new).astype(BF16)) for ch in chunks], axis=1)
            v = v_ref[keys, h * V_HEAD:(h + 1) * V_HEAD]
            v1 = jnp.concatenate([v, jnp.ones_like(v)], axis=1)
            alpha2 = jnp.concatenate([alpha, alpha], axis=1)
            acc_ref[h, r0:, :] = alpha2 * acc_ref[h, r0:, :] + jnp.dot(p, v1, preferred_element_type=F32)

    def body(j, carry):
        block(j, 0, False)
        return carry

    if tq < SEQ:
        lax.fori_loop(0, ratio * i, body, 0)
    for d in range(ratio):
        block(ratio * i + d, d * tk, True)
    for h in range(heads):
        acc = acc_ref[h]
        o_ref[:, h * V_HEAD:(h + 1) * V_HEAD] = (acc[:, :V_HEAD] / acc[:, V_HEAD:]).astype(o_ref.dtype)


def _attention(q, k, v, batch, tq=2048, tk=256, heads=2, casts=(), ada=None):
    t = q.shape[0]
    nq = SEQ // tq
    gh = MLA_HEADS // heads
    n_steps = batch * gh * nq
    step_of = lambda b, h, i: (b * gh + h) * nq + i
    cast_in, cast_out, cast_shapes, cast_ops = _cast_specs(casts, n_steps, step_of)
    ada_in, ada_out, ada_shapes, ada_ops = [], [], [], []
    if ada is not None:
        cond, w_ada, b_ada, first_col = ada
        width = (w_ada.shape[1] - first_col) // n_steps
        assert (w_ada.shape[1] - first_col) % (n_steps * LANES) == 0 and first_col % width == 0
        src = lambda b, h, i: (0, first_col // width + step_of(b, h, i))
        ada_in = [pl.BlockSpec(cond.shape, lambda b, h, i: (0, 0)),
                  pl.BlockSpec((w_ada.shape[0], width), src), pl.BlockSpec((1, width), src)]
        ada_out = [pl.BlockSpec((cond.shape[0], width), lambda b, h, i: (0, step_of(b, h, i)))]
        ada_shapes = [jax.ShapeDtypeStruct((cond.shape[0], w_ada.shape[1] - first_col), F32)]
        ada_ops = [cond, w_ada, b_ada]
    out = pl.pallas_call(
        functools.partial(_attn_kernel, tq=tq, tk=tk, heads=heads, n_cast=len(casts), ada=ada is not None),
        out_shape=[jax.ShapeDtypeStruct((t, MLA_HEADS * V_HEAD), BF16)] + cast_shapes + ada_shapes,
        grid=(batch, gh, nq),
        in_specs=[
            pl.BlockSpec((tq, heads * QK_PAD), lambda b, h, i: (b * nq + i, h)),
            pl.BlockSpec((SEQ, heads * QK_PAD), lambda b, h, i: (b, h)),
            pl.BlockSpec((SEQ, heads * V_HEAD), lambda b, h, i: (b, h)),
        ] + cast_in + ada_in,
        out_specs=[pl.BlockSpec((tq, heads * V_HEAD), lambda b, h, i: (b * nq + i, h))] + cast_out + ada_out,
        scratch_shapes=[pltpu.VMEM((heads, tq, LANES), F32), pltpu.VMEM((heads, tq, 2 * V_HEAD), F32)],
        compiler_params=_params("parallel", "parallel", "arbitrary"),
        name="attention",
    )(q, k, v, *cast_ops, *ada_ops)
    return out if (casts or ada is not None) else out[0]


def _branch_kernel(a_ref, wa_ref, b_ref, wb_ref, ga_ref, gb_ref, o_ref):
    ya = jnp.dot(a_ref[...], wa_ref[...], preferred_element_type=F32)
    yb = jnp.dot(b_ref[...], wb_ref[...], preferred_element_type=F32)
    o_ref[...] = (ga_ref[...].astype(F32) * ya + gb_ref[...].astype(F32) * yb).astype(o_ref.dtype)


def _branch(a, wa, b, wb, gates, tm=1024, tn=512):
    t = a.shape[0]
    nb = D_MODEL // tn
    return pl.pallas_call(
        _branch_kernel,
        out_shape=jax.ShapeDtypeStruct((t, D_MODEL), BF16),
        grid=(t // tm, nb),
        in_specs=[
            pl.BlockSpec((tm, a.shape[1]), lambda m, j: (m, 0)),
            pl.BlockSpec((wa.shape[0], tn), lambda m, j: (0, j)),
            pl.BlockSpec((tm, b.shape[1]), lambda m, j: (m, 0)),
            pl.BlockSpec((wb.shape[0], tn), lambda m, j: (0, j)),
            pl.BlockSpec((tm, tn), lambda m, j: (m, j)),
            pl.BlockSpec((tm, tn), lambda m, j: (m, nb + j)),
        ],
        out_specs=pl.BlockSpec((tm, tn), lambda m, j: (m, j)),
        compiler_params=_params("parallel", "parallel"),
        name="branch_merge",
    )(a, wa, b, wb, gates, gates)


def _mm_resid_kernel(x_ref, w_ref, res_ref, gate_ref, *refs, nk, n_cast):
    o_ref = refs[n_cast]
    _do_casts(refs, n_cast)
    kk = pl.program_id(2)
    if nk == 1:
        part = jnp.dot(x_ref[...], w_ref[...], preferred_element_type=F32)
        o_ref[...] = res_ref[...] + gate_ref[0, 0] * part
        return

    @pl.when(kk == 0)
    def _():
        o_ref[...] = jnp.zeros(o_ref.shape, F32)

    o_ref[...] += jnp.dot(x_ref[...], w_ref[...], preferred_element_type=F32)

    @pl.when(kk == nk - 1)
    def _():
        o_ref[...] = res_ref[...] + gate_ref[0, 0] * o_ref[...]


def _mm_resid(x, w, res, mod3, gate_idx, tm, tn, tk, name, casts=()):
    t, k = x.shape
    n = w.shape[1]
    gm, gn, nk = t // tm, n // tn, k // tk
    cast_in, cast_out, cast_shapes, cast_ops = _cast_specs(
        casts, gm * gn * nk, lambda m, j, kk: (m * gn + j) * nk + kk)
    out = pl.pallas_call(
        functools.partial(_mm_resid_kernel, nk=nk, n_cast=len(casts)),
        out_shape=[jax.ShapeDtypeStruct((t, n), F32)] + cast_shapes,
        grid=(gm, gn, nk),
        in_specs=[
            pl.BlockSpec((tm, tk), lambda m, j, kk: (m, kk)),
            pl.BlockSpec((tk, tn), lambda m, j, kk: (kk, j)),
            pl.BlockSpec((tm, tn), lambda m, j, kk: (m, j)),
            _mod_spec(gate_idx, tm, tn, ncol_arg=True),
        ] + cast_in,
        out_specs=[pl.BlockSpec((tm, tn), lambda m, j, kk: (m, j))] + cast_out,
        compiler_params=_params("parallel", "parallel", "arbitrary"),
        name=name,
    )(x, w, res, mod3, *cast_ops)
    return out if casts else out[0]


def _repack_kernel(wt_ref, o_ref, *, dup_tail):
    blk = wt_ref[...]
    if dup_tail:
        last = pl.program_id(0) == pl.num_programs(0) - 1
        half = blk.shape[0] // 2
        blk = jnp.where(last, jnp.concatenate([blk[:half], blk[:half]], axis=0), blk)
    o_ref[...] = blk.T.astype(BF16)


def _repack_w_in(w_in_t, start, width, cols, name, dup_tail=False):
    d = w_in_t.shape[1]
    return pl.pallas_call(
        functools.partial(_repack_kernel, dup_tail=dup_tail),
        out_shape=jax.ShapeDtypeStruct((d, width), BF16),
        grid=(width // cols,),
        in_specs=[pl.BlockSpec((pl.Element(cols), pl.Element(d)),
                               lambda i: (pl.multiple_of(start + i * cols, math.gcd(start, cols)), 0))],
        out_specs=pl.BlockSpec((d, cols), lambda i: (0, i)),
        compiler_params=_params("parallel"),
        name=name,
    )(w_in_t)


def _pad_cols(a, width):
    return jnp.pad(a, ((0, 0), (0, width - a.shape[1])))


def _layer(x2, batch, c8, rope_in, tables, w_ada, b_ada, g_norm1, w_in, g_v, w_s, b_s, g_q_lat, g_kv_lat,
           w_uq, w_ukv, g_qn, g_kn, w_branch_a, w_branch_b, w_out, g_norm2, w_ff1, w_ff2):
    d = x2.shape[1]
    b_ada = b_ada.reshape(1, -1)
    mod_a, cond = _ada(c8, w_ada, b_ada, 2 * d)
    mod_a = mod_a[:batch].reshape(batch, 2, 1, d)
    w_in_t = w_in.T
    w_lat = _repack_w_in(w_in_t, OFF_Q, LAT_COLS, LANES, "repack_lat", dup_tail=True)
    bias_full = jnp.repeat(b_s.T, GM_GROUP_W, axis=1)
    row = lambda v: v.reshape(1, -1)
    twice = lambda v: jnp.concatenate([v, v]).reshape(1, -1)
    qg1, qg2 = row(g_qn[:QK_NOPE]), twice(g_qn[QK_NOPE:])
    kg1, kg2 = row(g_kn[:QK_NOPE]), twice(g_kn[QK_NOPE:])

    if tables is None:
        h1, *tables = _normmod(x2, row(g_norm1), mod_a, 0, 1, rope_in=rope_in)
    else:
        h1 = _normmod(x2, row(g_norm1), mod_a, 0, 1)
    c_tab, s_tab = tables
    z, w_gate, wq = _mm_act(h1, w_in_t, "gelu", BF16, 1024, 512, "in_proj_uv", w_transposed_cols=OFF_Q,
                            casts=((w_in_t, OFF_GATE, w_in.shape[1] - OFF_GATE), ("widen_heads", w_uq)))
    lat, wkv = _mm_act(h1, w_lat, "none", F32, 512, LAT_COLS, "in_proj_lat", casts=(w_ukv,))
    gates, w1 = _mm_act(h1, w_gate, "sigmoid", BF16, 1024, 1024, "in_proj_gate", casts=(w_ff1,))
    gm = _gmlp(z, row(g_v), w_s, bias_full)
    q = _qprep(lat, row(g_q_lat), wq, qg1, qg2, c_tab, s_tab)
    k, v = _kvprep(lat, row(g_kv_lat), wkv, kg1, kg2, c_tab, s_tab)
    att, wa, wb, wo, mod_b = _attention(q, k, v, batch, casts=(w_branch_a, w_branch_b, w_out),
                                        ada=(cond, w_ada, b_ada, 2 * d))
    mod_b = mod_b[:batch].reshape(batch, N_MOD - 2, 1, d)
    mixed = _branch(gm, wa, att, wb, gates)
    x2 = _mm_resid(mixed, wo, x2, mod_b, 0, 1024, 1024, D_MODEL, "out_proj")

    h2 = _normmod(x2, row(g_norm2), mod_b, 1, 2)
    hid, w2 = _mm_act(h2, w1, "relu2", BF16, 1024, 1024, "ff1", casts=(w_ff2,))
    return _mm_resid(hid, w2, x2, mod_b, 3, 1024, 1024, 4096, "ff2"), tables


def kernel(x, c, positions, w_ada, b_ada, g_norm1, w_in, g_v, w_s, b_s, g_q_lat, g_kv_lat, w_uq, w_ukv,
           g_qn, g_kn, w_branch_a, w_branch_b, w_out, g_norm2, w_ff1, w_ff2):
    batch, seq, d = x.shape
    assert (seq, d) == (SEQ, D_MODEL)
    depth = w_ada.shape[0]
    t = batch * seq
    x2 = x.reshape(t, d)

    inv_freq = 1.0 / (ROPE_THETA ** (jnp.arange(0, QK_ROPE, 2, dtype=F32) / QK_ROPE))
    freq_row = _pad_cols(jnp.concatenate([inv_freq, inv_freq]).reshape(1, QK_ROPE), LANES)
    rope_in = (positions.reshape(t, 1), freq_row)

    c8 = jnp.pad(c, ((0, 8 - batch), (0, 0)))
    tables = None
    for l in range(depth):
        x2, tables = _layer(x2, batch, c8, rope_in, tables, w_ada[l], b_ada[l], g_norm1[l], w_in[l], g_v[l],
                            w_s[l], b_s[l],
                            g_q_lat[l], g_kv_lat[l], w_uq[l], w_ukv[l], g_qn[l], g_kn[l], w_branch_a[l],
                            w_branch_b[l], w_out[l], g_norm2[l], w_ff1[l], w_ff2[l])
    return x2.reshape(batch, seq, d)
```

```python
import functools
import math

import jax
import jax.numpy as jnp
from jax import lax
from jax.experimental import pallas as pl
from jax.experimental.pallas import tpu as pltpu

D_MODEL = 4096
SEQ = 2048
GM_WIDTH = 2048
GM_GROUPS = 8
GM_GROUP_W = GM_WIDTH // GM_GROUPS
CHUNK = 128
MLA_HEADS = 32
QK_NOPE = 128
QK_ROPE = 64
QK_HEAD = QK_NOPE + QK_ROPE
V_HEAD = 128
Q_LORA = 1024
KV_LORA = 512
ROPE_THETA = 10000.0
N_MOD = 6
EPS = 1e-6
OFF_Q = 2 * GM_WIDTH
OFF_KPE = OFF_Q + Q_LORA + KV_LORA
OFF_GATE = OFF_KPE + QK_ROPE

LANES = 128
QK_PAD = 2 * LANES
LAT_COLS = Q_LORA + KV_LORA + LANES
VMEM_LIMIT = 58 * 1024 * 1024

TILE_IN_UV = (1024, 512)
TILE_IN_LAT = (512, LAT_COLS)
TILE_IN_GATE = (1024, 1024)
TILE_OUT = (1024, 1024, D_MODEL)
TILE_FF1 = (1024, 1024)
TILE_FF2 = (1024, 1024, 4096)

F32 = jnp.float32
BF16 = jnp.bfloat16


def _params(*sem):
    return pltpu.CompilerParams(dimension_semantics=sem, vmem_limit_bytes=VMEM_LIMIT)


def _ada_kernel(c_ref, w_ref, b_ref, o_ref, cond_ref):
    c = c_ref[...]
    cond = (c * jax.nn.sigmoid(c)).astype(BF16)
    cond_ref[...] = cond
    o_ref[...] = jnp.dot(cond, w_ref[...].astype(BF16), preferred_element_type=F32) + b_ref[...]


def _ada(c8, w_ada, b_ada, n, tn=1024):
    return pl.pallas_call(
        _ada_kernel,
        out_shape=(jax.ShapeDtypeStruct((8, n), F32), jax.ShapeDtypeStruct((8, D_MODEL), BF16)),
        grid=(n // tn,),
        in_specs=[
            pl.BlockSpec((8, D_MODEL), lambda j: (0, 0)),
            pl.BlockSpec((D_MODEL, tn), lambda j: (0, j)),
            pl.BlockSpec((1, tn), lambda j: (0, j)),
        ],
        out_specs=(pl.BlockSpec((8, tn), lambda j: (0, j)), pl.BlockSpec((8, D_MODEL), lambda j: (0, 0))),
        compiler_params=_params("arbitrary"),
        name="ada",
    )(c8, w_ada, b_ada)


def _normmod_kernel(x_ref, g_ref, sh_ref, sc_ref, *refs):
    o_ref = refs[-1] if len(refs) == 1 else refs[2]
    x = x_ref[...]
    y = x * lax.rsqrt(jnp.mean(x * x, axis=-1, keepdims=True) + EPS)
    y = y * g_ref[...]
    o_ref[...] = (y * (1.0 + sc_ref[0, 0]) + sh_ref[0, 0]).astype(o_ref.dtype)
    if len(refs) > 1:
        pos_ref, freq_ref, _, c_ref, s_ref = refs
        _rope_tab_kernel(pos_ref, freq_ref, c_ref, s_ref)


def _mod_spec(idx, tm, width, ncol_arg=False):
    per_batch = SEQ // tm
    if ncol_arg:
        return pl.BlockSpec((1, 1, 1, width), lambda m, n, *_: (m // per_batch, idx, 0, n))
    return pl.BlockSpec((1, 1, 1, width), lambda m, *_: (m // per_batch, idx, 0, 0))


def _normmod(x2, g, mod4, shift_idx, scale_idx, tm=512, rope_in=None):
    t, d = x2.shape
    rows = lambda width: pl.BlockSpec((tm, width), lambda m: (m, 0))
    in_specs = [rows(d), pl.BlockSpec((1, d), lambda m: (0, 0)),
                _mod_spec(shift_idx, tm, d), _mod_spec(scale_idx, tm, d)]
    out_shape, out_specs, extra = [jax.ShapeDtypeStruct((t, d), BF16)], [rows(d)], ()
    if rope_in is not None:
        extra = rope_in
        in_specs += [rows(1), pl.BlockSpec((1, LANES), lambda m: (0, 0))]
        out_shape += [jax.ShapeDtypeStruct((t, LANES), F32)] * 2
        out_specs += [rows(LANES)] * 2
    out = pl.pallas_call(
        _normmod_kernel,
        out_shape=out_shape,
        grid=(t // tm,),
        in_specs=in_specs,
        out_specs=out_specs,
        compiler_params=_params("parallel"),
        name="normmod",
    )(x2, g, mod4, mod4, *extra)
    return out if rope_in is not None else out[0]


def _gelu(v):
    return 0.5 * v * (1.0 + lax.erf(v * math.sqrt(0.5)))


def _relu2(v):
    r = jnp.maximum(v, 0.0)
    return r * r


def _sigmoid(v):
    return 0.5 + 0.5 * jnp.tanh(0.5 * v)


_ACTS = {"gelu": _gelu, "sigmoid": _sigmoid, "relu2": _relu2, "none": lambda v: v}


def _cast_specs(jobs, n_steps, step_of):
    in_specs, out_specs, shapes, operands = [], [], [], []

    def rows_job(a):
        rows, cols = a.shape
        assert rows % (n_steps * 16) == 0
        spec = pl.BlockSpec((rows // n_steps, cols), lambda *ids: (step_of(*ids), 0))
        return a, spec, spec, a.shape

    def transposed_job(w_t, start, width):
        d, cols = w_t.shape[1], width // n_steps
        assert width % (n_steps * LANES) == 0
        src = pl.BlockSpec(
            (pl.Element(cols), pl.Element(d)),
            lambda *ids: (pl.multiple_of(start + step_of(*ids) * cols, math.gcd(start, cols)), 0))
        return w_t, src, pl.BlockSpec((d, cols), lambda *ids: (0, step_of(*ids))), (d, width)

    def widen_heads_job(w_uq):
        rows = w_uq.shape[0] // n_steps
        assert w_uq.shape[0] % (n_steps * 16) == 0
        index = lambda *ids: (step_of(*ids), 0)
        wide = MLA_HEADS * QK_PAD
        return (w_uq, pl.BlockSpec((rows, w_uq.shape[1]), index), pl.BlockSpec((rows, wide), index),
                (w_uq.shape[0], wide))

    for job in jobs:
        if not isinstance(job, tuple):
            a, src, dst, shape = rows_job(job)
        elif isinstance(job[0], str):
            a, src, dst, shape = widen_heads_job(job[1])
        else:
            a, src, dst, shape = transposed_job(*job)
        operands.append(a)
        in_specs.append(src)
        out_specs.append(dst)
        shapes.append(jax.ShapeDtypeStruct(shape, BF16))
    return in_specs, out_specs, shapes, operands


def _widen_heads(w, o_ref):
    for h in range(MLA_HEADS):
        src, dst = h * QK_HEAD, h * QK_PAD
        o_ref[:, dst:dst + QK_HEAD] = w[:, src:src + QK_HEAD]
        o_ref[:, dst + QK_HEAD:dst + QK_PAD] = w[:, src + QK_NOPE:src + QK_HEAD]


def _do_casts(refs, n):
    for src, dst in zip(refs[:n], refs[len(refs) - n:]):
        blk = src[...]
        if src.shape == dst.shape:
            dst[...] = blk.astype(BF16)
        elif src.shape[0] == dst.shape[0]:
            _widen_heads(blk.astype(BF16), dst)
        else:
            dst[...] = blk.T.astype(BF16)


def _mm_act_kernel(x_ref, w_ref, *refs, act, n_cast, w_transposed):
    o_ref = refs[n_cast]
    if w_transposed:
        acc = lax.dot_general(x_ref[...], w_ref[...].astype(BF16), (((1,), (1,)), ((), ())),
                              preferred_element_type=F32)
    else:
        acc = jnp.dot(x_ref[...], w_ref[...], preferred_element_type=F32)
    o_ref[...] = _ACTS[act](acc).astype(o_ref.dtype)
    _do_casts(refs, n_cast)


def _mm_act(x, w, act, out_dtype, tm, tn, name, casts=(), w_transposed_cols=None):
    t, k = x.shape
    n = w.shape[1] if w_transposed_cols is None else w_transposed_cols
    gm, gn = t // tm, n // tn
    w_spec = (pl.BlockSpec((k, tn), lambda m, j: (0, j)) if w_transposed_cols is None
              else pl.BlockSpec((tn, k), lambda m, j: (j, 0)))
    cast_in, cast_out, cast_shapes, cast_ops = _cast_specs(casts, gm * gn, lambda m, j: m * gn + j)
    out = pl.pallas_call(
        functools.partial(_mm_act_kernel, act=act, n_cast=len(casts),
                          w_transposed=w_transposed_cols is not None),
        out_shape=[jax.ShapeDtypeStruct((t, n), out_dtype)] + cast_shapes,
        grid=(gm, gn),
        in_specs=[pl.BlockSpec((tm, k), lambda m, j: (m, 0)), w_spec] + cast_in,
        out_specs=[pl.BlockSpec((tm, tn), lambda m, j: (m, j))] + cast_out,
        compiler_params=_params("parallel", "parallel"),
        name=name,
    )(x, w, *cast_ops)
    return out if casts else out[0]


def _gmlp_kernel(u_ref, v_ref, gv_ref, ws_ref, bias_ref, o_ref, *, n_chunks):
    row = lax.broadcasted_iota(jnp.int32, (CHUNK, CHUNK), 0)
    col = lax.broadcasted_iota(jnp.int32, (CHUNK, CHUNK), 1)
    causal = col <= row
    gv = gv_ref[...]
    for ci in range(n_chunks):
        rows = pl.ds(ci * CHUNK, CHUNK)
        v = v_ref[rows, :].astype(F32)
        vn = v * lax.rsqrt(jnp.mean(v * v, axis=-1, keepdims=True) + EPS) * gv
        vn = vn.astype(BF16)
        for g in range(GM_GROUPS):
            cols = slice(g * GM_GROUP_W, (g + 1) * GM_GROUP_W)
            w = jnp.where(causal, ws_ref[g], 0.0).astype(BF16)
            mixed = jnp.dot(w, vn[:, cols], preferred_element_type=F32) + bias_ref[:, cols]
            o_ref[rows, cols] = (u_ref[rows, cols].astype(F32) * mixed).astype(o_ref.dtype)


def _gmlp(z, g_v, w_s, bias_full, tm=1024):
    t = z.shape[0]
    return pl.pallas_call(
        functools.partial(_gmlp_kernel, n_chunks=tm // CHUNK),
        out_shape=jax.ShapeDtypeStruct((t, GM_WIDTH), BF16),
        grid=(t // tm,),
        in_specs=[
            pl.BlockSpec((tm, GM_WIDTH), lambda m: (m, 0)),
            pl.BlockSpec((tm, GM_WIDTH), lambda m: (m, 1)),
            pl.BlockSpec((1, GM_WIDTH), lambda m: (0, 0)),
            pl.BlockSpec((GM_GROUPS, CHUNK, CHUNK), lambda m: (0, 0, 0)),
            pl.BlockSpec((CHUNK, GM_WIDTH), lambda m: (0, 0)),
        ],
        out_specs=pl.BlockSpec((tm, GM_WIDTH), lambda m: (m, 0)),
        compiler_params=_params("parallel"),
        name="gmlp",
    )(z, z, g_v, w_s, bias_full)


def _rope_tab_kernel(pos_ref, freq_ref, c_ref, s_ref):
    ang = pos_ref[...].astype(F32) * freq_ref[...]
    lane = lax.broadcasted_iota(jnp.int32, ang.shape, 1)
    cos = jnp.cos(ang)
    sin = jnp.sin(ang)
    c_ref[...] = jnp.where(lane < QK_ROPE, cos, 0.0)
    s_ref[...] = jnp.where(lane < QK_ROPE // 2, -sin, jnp.where(lane < QK_ROPE, sin, 0.0))


def _rope(v, c, s):
    return v * c + pltpu.roll(v, QK_ROPE // 2, 1) * s


def _rms_rows(x, g):
    return x * lax.rsqrt(jnp.mean(x * x, axis=-1, keepdims=True) + EPS) * g


def _qprep_kernel(lat_ref, glat_ref, w_ref, g1_ref, g2_ref, c_ref, s_ref, o_ref):
    xn = _rms_rows(lat_ref[...], glat_ref[...]).astype(BF16)
    c, s = c_ref[...], s_ref[...]
    g1, g2 = g1_ref[...], g2_ref[...]
    scale = QK_HEAD ** -0.5 * math.log2(math.e)
    for h in range(MLA_HEADS):
        acc = jnp.dot(xn, w_ref[:, h * QK_PAD:(h + 1) * QK_PAD], preferred_element_type=F32)
        a = acc[:, :LANES]
        b = acc[:, LANES:]
        ssq = jnp.sum(a * a + 0.5 * (b * b), axis=-1, keepdims=True)
        r = lax.rsqrt(ssq * (1.0 / QK_HEAD) + EPS) * scale
        o_ref[:, h * QK_PAD:h * QK_PAD + LANES] = (a * r * g1).astype(o_ref.dtype)
        o_ref[:, h * QK_PAD + LANES:(h + 1) * QK_PAD] = _rope(b * r * g2, c, s).astype(o_ref.dtype)


def _qprep(lat, g_q_lat, wq, g1, g2, c, s, tm=512):
    t = lat.shape[0]
    tn = MLA_HEADS * QK_PAD
    row = lambda m: (m, 0)
    const = lambda m: (0, 0)
    return pl.pallas_call(
        _qprep_kernel,
        out_shape=jax.ShapeDtypeStruct((t, tn), BF16),
        grid=(t // tm,),
        in_specs=[
            pl.BlockSpec((tm, Q_LORA), row),
            pl.BlockSpec((1, Q_LORA), const),
            pl.BlockSpec((Q_LORA, tn), const, pipeline_mode=pl.Buffered(1)),
            pl.BlockSpec((1, LANES), const),
            pl.BlockSpec((1, LANES), const),
            pl.BlockSpec((tm, LANES), row),
            pl.BlockSpec((tm, LANES), row),
        ],
        out_specs=pl.BlockSpec((tm, tn), row),
        compiler_params=_params("parallel"),
        name="q_prep",
    )(lat, g_q_lat, wq, g1, g2, c, s)


def _kvprep_kernel(lat_ref, kpe_ref, glat_ref, w_ref, g1_ref, g2_ref, c_ref, s_ref,
                   k_ref, v_ref):
    xn = _rms_rows(lat_ref[...], glat_ref[...]).astype(BF16)
    kpe = kpe_ref[...]
    ssq_pe = 0.5 * jnp.sum(kpe * kpe, axis=-1, keepdims=True)
    kr = _rope(kpe * g2_ref[...], c_ref[...], s_ref[...])
    g1 = g1_ref[...]
    width = QK_NOPE + V_HEAD
    for h in range(MLA_HEADS):
        acc = jnp.dot(xn, w_ref[:, h * width:(h + 1) * width], preferred_element_type=F32)
        a = acc[:, :QK_NOPE]
        ssq = jnp.sum(a * a, axis=-1, keepdims=True) + ssq_pe
        r = lax.rsqrt(ssq * (1.0 / QK_HEAD) + EPS)
        k_ref[:, h * QK_PAD:h * QK_PAD + LANES] = (a * r * g1).astype(k_ref.dtype)
        k_ref[:, h * QK_PAD + LANES:(h + 1) * QK_PAD] = (kr * r).astype(k_ref.dtype)
        v_ref[:, h * V_HEAD:(h + 1) * V_HEAD] = acc[:, QK_NOPE:].astype(v_ref.dtype)


def _kvprep(lat, g_kv_lat, wkv, g1, g2, c, s, tm=512):
    t = lat.shape[0]
    row = lambda m: (m, 0)
    const = lambda m: (0, 0)
    return pl.pallas_call(
        _kvprep_kernel,
        out_shape=(jax.ShapeDtypeStruct((t, MLA_HEADS * QK_PAD), BF16),
                   jax.ShapeDtypeStruct((t, MLA_HEADS * V_HEAD), BF16)),
        grid=(t // tm,),
        in_specs=[
            pl.BlockSpec((tm, KV_LORA), lambda m: (m, Q_LORA // KV_LORA)),
            pl.BlockSpec((tm, LANES), lambda m: (m, (Q_LORA + KV_LORA) // LANES)),
            pl.BlockSpec((1, KV_LORA), const),
            pl.BlockSpec(wkv.shape, const, pipeline_mode=pl.Buffered(1)),
            pl.BlockSpec((1, LANES), const),
            pl.BlockSpec((1, LANES), const),
            pl.BlockSpec((tm, LANES), row),
            pl.BlockSpec((tm, LANES), row),
        ],
        out_specs=(pl.BlockSpec((tm, MLA_HEADS * QK_PAD), row),
                   pl.BlockSpec((tm, MLA_HEADS * V_HEAD), row)),
        compiler_params=_params("parallel"),
        name="kv_prep",
    )(lat, lat, g_kv_lat, wkv, g1, g2, c, s)


def _attn_kernel(q_ref, k_ref, v_ref, *refs, tq, tk, heads, n_cast, ada):
    n_in = n_cast + (3 if ada else 0)
    o_ref, m_ref, acc_ref = refs[n_in], refs[-2], refs[-1]
    _do_casts(refs[:n_cast] + refs[n_in + 1:n_in + 1 + n_cast], n_cast)
    if ada:
        cond_ref, wada_ref, bada_ref = refs[n_cast:n_in]
        mod_ref = refs[n_in + 1 + n_cast]
        mod_ref[...] = jnp.dot(cond_ref[...], wada_ref[...].astype(BF16),
                               preferred_element_type=F32) + bada_ref[...]
    i = pl.program_id(2)
    nc = tk // LANES
    ratio = tq // tk
    m_ref[...] = jnp.full(m_ref.shape, -jnp.inf, F32)
    acc_ref[...] = jnp.zeros(acc_ref.shape, F32)

    def block(j, r0, masked):
        keys = pl.ds(pl.multiple_of(j * tk, tk), tk)
        nr = tq - r0
        for h in range(heads):
            q = q_ref[r0:, h * QK_PAD:(h + 1) * QK_PAD]
            k = k_ref[keys, h * QK_PAD:(h + 1) * QK_PAD]
            s = lax.dot_general(q, k, (((1,), (1,)), ((), ())), preferred_element_type=F32)
            if masked:
                row = lax.broadcasted_iota(jnp.int32, (nr, tk), 0)
                col = lax.broadcasted_iota(jnp.int32, (nr, tk), 1)
                s = jnp.where(col <= row, s, -jnp.inf)
            chunks = [s[:, c * LANES:(c + 1) * LANES] for c in range(nc)]
            m_part = functools.reduce(jnp.maximum, chunks)
            m_prev = m_ref[h, r0:, :]
            m_cur = jnp.broadcast_to(jnp.max(m_part, axis=1, keepdims=True), m_part.shape)
            m_new = jnp.maximum(m_prev, m_cur)
            alpha = jnp.exp2(m_prev - m_new)
            m_ref[h, r0:, :] = m_new
            p = jnp.concatenate([jnp.exp2((ch - m_new).astype(BF16)) for ch in chunks], axis=1)
            v = v_ref[keys, h * V_HEAD:(h + 1) * V_HEAD]
            v1 = jnp.concatenate([v, jnp.ones_like(v)], axis=1)
            alpha2 = jnp.concatenate([alpha, alpha], axis=1)
            acc_ref[h, r0:, :] = alpha2 * acc_ref[h, r0:, :] + jnp.dot(p, v1, preferred_element_type=F32)

    def body(j, carry):
        block(j, 0, False)
        return carry

    if tq < SEQ:
        lax.fori_loop(0, ratio * i, body, 0)
    for d in range(ratio):
        block(ratio * i + d, d * tk, True)
    for h in range(heads):
        acc = acc_ref[h]
        o_ref[:, h * V_HEAD:(h + 1) * V_HEAD] = (acc[:, :V_HEAD] / acc[:, V_HEAD:]).astype(o_ref.dtype)


def _attention(q, k, v, batch, tq=2048, tk=256, heads=2, casts=(), ada=None):
    t = q.shape[0]
    nq = SEQ // tq
    gh = MLA_HEADS // heads
    n_steps = batch * gh * nq
    step_of = lambda b, h, i: (b * gh + h) * nq + i
    cast_in, cast_out, cast_shapes, cast_ops = _cast_specs(casts, n_steps, step_of)
    ada_in, ada_out, ada_shapes, ada_ops = [], [], [], []
    if ada is not None:
        cond, w_ada, b_ada, first_col = ada
        width = (w_ada.shape[1] - first_col) // n_steps
        assert (w_ada.shape[1] - first_col) % (n_steps * LANES) == 0 and first_col % width == 0
        src = lambda b, h, i: (0, first_col // width + step_of(b, h, i))
        ada_in = [pl.BlockSpec(cond.shape, lambda b, h, i: (0, 0)),
                  pl.BlockSpec((w_ada.shape[0], width), src), pl.BlockSpec((1, width), src)]
        ada_out = [pl.BlockSpec((cond.shape[0], width), lambda b, h, i: (0, step_of(b, h, i)))]
        ada_shapes = [jax.ShapeDtypeStruct((cond.shape[0], w_ada.shape[1] - first_col), F32)]
        ada_ops = [cond, w_ada, b_ada]
    out = pl.pallas_call(
        functools.partial(_attn_kernel, tq=tq, tk=tk, heads=heads, n_cast=len(casts), ada=ada is not None),
        out_shape=[jax.ShapeDtypeStruct((t, MLA_HEADS * V_HEAD), BF16)] + cast_shapes + ada_shapes,
        grid=(batch, gh, nq),
        in_specs=[
            pl.BlockSpec((tq, heads * QK_PAD), lambda b, h, i: (b * nq + i, h)),
            pl.BlockSpec((SEQ, heads * QK_PAD), lambda b, h, i: (b, h)),
            pl.BlockSpec((SEQ, heads * V_HEAD), lambda b, h, i: (b, h)),
        ] + cast_in + ada_in,
        out_specs=[pl.BlockSpec((tq, heads * V_HEAD), lambda b, h, i: (b * nq + i, h))] + cast_out + ada_out,
        scratch_shapes=[pltpu.VMEM((heads, tq, LANES), F32), pltpu.VMEM((heads, tq, 2 * V_HEAD), F32)],
        compiler_params=_params("parallel", "parallel", "arbitrary"),
        name="attention",
    )(q, k, v, *cast_ops, *ada_ops)
    return out if (casts or ada is not None) else out[0]


def _branch_kernel(a_ref, wa_ref, b_ref, wb_ref, ga_ref, gb_ref, o_ref):
    ya = jnp.dot(a_ref[...], wa_ref[...], preferred_element_type=F32)
    yb = jnp.dot(b_ref[...], wb_ref[...], preferred_element_type=F32)
    o_ref[...] = (ga_ref[...].astype(F32) * ya + gb_ref[...].astype(F32) * yb).astype(o_ref.dtype)


def _branch(a, wa, b, wb, gates, tm=1024, tn=512):
    t = a.shape[0]
    nb = D_MODEL // tn
    return pl.pallas_call(
        _branch_kernel,
        out_shape=jax.ShapeDtypeStruct((t, D_MODEL), BF16),
        grid=(t // tm, nb),
        in_specs=[
            pl.BlockSpec((tm, a.shape[1]), lambda m, j: (m, 0)),
            pl.BlockSpec((wa.shape[0], tn), lambda m, j: (0, j)),
            pl.BlockSpec((tm, b.shape[1]), lambda m, j: (m, 0)),
            pl.BlockSpec((wb.shape[0], tn), lambda m, j: (0, j)),
            pl.BlockSpec((tm, tn), lambda m, j: (m, j)),
            pl.BlockSpec((tm, tn), lambda m, j: (m, nb + j)),
        ],
        out_specs=pl.BlockSpec((tm, tn), lambda m, j: (m, j)),
        compiler_params=_params("parallel", "parallel"),
        name="branch_merge",
    )(a, wa, b, wb, gates, gates)


def _mm_resid_kernel(x_ref, w_ref, res_ref, gate_ref, *refs, nk, n_cast):
    o_ref = refs[n_cast]
    _do_casts(refs, n_cast)
    kk = pl.program_id(2)
    if nk == 1:
        part = jnp.dot(x_ref[...], w_ref[...], preferred_element_type=F32)
        o_ref[...] = res_ref[...] + gate_ref[0, 0] * part
        return

    @pl.when(kk == 0)
    def _():
        o_ref[...] = jnp.dot(x_ref[...], w_ref[...], preferred_element_type=F32)

    @pl.when(kk > 0)
    def _():
        o_ref[...] += jnp.dot(x_ref[...], w_ref[...], preferred_element_type=F32)

    @pl.when(kk == nk - 1)
    def _():
        o_ref[...] = res_ref[...] + gate_ref[0, 0] * o_ref[...]


def _mm_resid(x, w, res, mod3, gate_idx, tm, tn, tk, name, casts=()):
    t, k = x.shape
    n = w.shape[1]
    gm, gn, nk = t // tm, n // tn, k // tk
    cast_in, cast_out, cast_shapes, cast_ops = _cast_specs(
        casts, gm * gn * nk, lambda m, j, kk: (m * gn + j) * nk + kk)
    out = pl.pallas_call(
        functools.partial(_mm_resid_kernel, nk=nk, n_cast=len(casts)),
        out_shape=[jax.ShapeDtypeStruct((t, n), F32)] + cast_shapes,
        grid=(gm, gn, nk),
        in_specs=[
            pl.BlockSpec((tm, tk), lambda m, j, kk: (m, kk)),
            pl.BlockSpec((tk, tn), lambda m, j, kk: (kk, j)),
            pl.BlockSpec((tm, tn), lambda m, j, kk: (m, j)),
            _mod_spec(gate_idx, tm, tn, ncol_arg=True),
        ] + cast_in,
        out_specs=[pl.BlockSpec((tm, tn), lambda m, j, kk: (m, j))] + cast_out,
        compiler_params=_params("parallel", "parallel", "arbitrary"),
        name=name,
    )(x, w, res, mod3, *cast_ops)
    return out if casts else out[0]


def _repack_kernel(wt_ref, o_ref, *, dup_tail):
    blk = wt_ref[...]
    if dup_tail:
        last = pl.program_id(0) == pl.num_programs(0) - 1
        half = blk.shape[0] // 2
        blk = jnp.where(last, jnp.concatenate([blk[:half], blk[:half]], axis=0), blk)
    o_ref[...] = blk.T.astype(BF16)


def _repack_w_in(w_in_t, start, width, cols, name, dup_tail=False):
    d = w_in_t.shape[1]
    return pl.pallas_call(
        functools.partial(_repack_kernel, dup_tail=dup_tail),
        out_shape=jax.ShapeDtypeStruct((d, width), BF16),
        grid=(width // cols,),
        in_specs=[pl.BlockSpec((pl.Element(cols), pl.Element(d)),
                               lambda i: (pl.multiple_of(start + i * cols, math.gcd(start, cols)), 0))],
        out_specs=pl.BlockSpec((d, cols), lambda i: (0, i)),
        compiler_params=_params("parallel"),
        name=name,
    )(w_in_t)


def _pad_cols(a, width):
    return jnp.pad(a, ((0, 0), (0, width - a.shape[1])))


def _layer(x2, batch, c8, rope_in, tables, w_ada, b_ada, g_norm1, w_in, g_v, w_s, b_s, g_q_lat, g_kv_lat,
           w_uq, w_ukv, g_qn, g_kn, w_branch_a, w_branch_b, w_out, g_norm2, w_ff1, w_ff2):
    d = x2.shape[1]
    b_ada = b_ada.reshape(1, -1)
    mod_a, cond = _ada(c8, w_ada, b_ada, 2 * d)
    mod_a = mod_a[:batch].reshape(batch, 2, 1, d)
    w_in_t = w_in.T
    w_lat = _repack_w_in(w_in_t, OFF_Q, LAT_COLS, LANES, "repack_lat", dup_tail=True)
    bias_full = jnp.repeat(b_s.T, GM_GROUP_W, axis=1)
    row = lambda v: v.reshape(1, -1)
    twice = lambda v: jnp.concatenate([v, v]).reshape(1, -1)
    qg1, qg2 = row(g_qn[:QK_NOPE]), twice(g_qn[QK_NOPE:])
    kg1, kg2 = row(g_kn[:QK_NOPE]), twice(g_kn[QK_NOPE:])

    if tables is None:
        h1, *tables = _normmod(x2, row(g_norm1), mod_a, 0, 1, rope_in=rope_in)
    else:
        h1 = _normmod(x2, row(g_norm1), mod_a, 0, 1)
    c_tab, s_tab = tables
    z, w_gate, wq = _mm_act(h1, w_in_t, "gelu", BF16, *TILE_IN_UV, "in_proj_uv", w_transposed_cols=OFF_Q,
                            casts=((w_in_t, OFF_GATE, w_in.shape[1] - OFF_GATE), ("widen_heads", w_uq)))
    lat, wkv = _mm_act(h1, w_lat, "none", F32, *TILE_IN_LAT, "in_proj_lat", casts=(w_ukv,))
    gates, w1 = _mm_act(h1, w_gate, "sigmoid", BF16, *TILE_IN_GATE, "in_proj_gate", casts=(w_ff1,))
    gm = _gmlp(z, row(g_v), w_s, bias_full)
    q = _qprep(lat, row(g_q_lat), wq, qg1, qg2, c_tab, s_tab)
    k, v = _kvprep(lat, row(g_kv_lat), wkv, kg1, kg2, c_tab, s_tab)
    att, wa, wb, wo, mod_b = _attention(q, k, v, batch, casts=(w_branch_a, w_branch_b, w_out),
                                        ada=(cond, w_ada, b_ada, 2 * d))
    mod_b = mod_b[:batch].reshape(batch, N_MOD - 2, 1, d)
    mixed = _branch(gm, wa, att, wb, gates)
    x2 = _mm_resid(mixed, wo, x2, mod_b, 0, *TILE_OUT, "out_proj")

    h2 = _normmod(x2, row(g_norm2), mod_b, 1, 2)
    hid, w2 = _mm_act(h2, w1, "relu2", BF16, *TILE_FF1, "ff1", casts=(w_ff2,))
    return _mm_resid(hid, w2, x2, mod_b, 3, *TILE_FF2, "ff2"), tables


def kernel(x, c, positions, w_ada, b_ada, g_norm1, w_in, g_v, w_s, b_s, g_q_lat, g_kv_lat, w_uq, w_ukv,
           g_qn, g_kn, w_branch_a, w_branch_b, w_out, g_norm2, w_ff1, w_ff2):
    batch, seq, d = x.shape
    assert (seq, d) == (SEQ, D_MODEL)
    depth = w_ada.shape[0]
    t = batch * seq
    x2 = x.reshape(t, d)

    inv_freq = 1.0 / (ROPE_THETA ** (jnp.arange(0, QK_ROPE, 2, dtype=F32) / QK_ROPE))
    freq_row = _pad_cols(jnp.concatenate([inv_freq, inv_freq]).reshape(1, QK_ROPE), LANES)
    rope_in = (positions.reshape(t, 1), freq_row)

    c8 = jnp.pad(c, ((0, 8 - batch), (0, 0)))
    tables = None
    for l in range(depth):
        x2, tables = _layer(x2, batch, c8, rope_in, tables, w_ada[l], b_ada[l], g_norm1[l], w_in[l], g_v[l],
                            w_s[l], b_s[l], g_q_lat[l], g_kv_lat[l], w_uq[l], w_ukv[l], g_qn[l], g_kn[l],
                            w_branch_a[l], w_branch_b[l], w_out[l], g_norm2[l], w_ff1[l], w_ff2[l])
    return x2.reshape(batch, seq, d)
```

```python
import functools
import math

import jax
import jax.numpy as jnp
from jax import lax
from jax.experimental import pallas as pl
from jax.experimental.pallas import tpu as pltpu

D_MODEL = 4096
SEQ = 2048
GM_WIDTH = 2048
GM_GROUPS = 8
GM_GROUP_W = GM_WIDTH // GM_GROUPS
CHUNK = 128
MLA_HEADS = 32
QK_NOPE = 128
QK_ROPE = 64
QK_HEAD = QK_NOPE + QK_ROPE
V_HEAD = 128
Q_LORA = 1024
KV_LORA = 512
ROPE_THETA = 10000.0
N_MOD = 6
EPS = 1e-6
OFF_Q = 2 * GM_WIDTH
OFF_KPE = OFF_Q + Q_LORA + KV_LORA
OFF_GATE = OFF_KPE + QK_ROPE

LANES = 128
QK_PAD = 2 * LANES
LAT_COLS = Q_LORA + KV_LORA + LANES
VMEM_LIMIT = 58 * 1024 * 1024

TILE_IN_UV = (1024, 512)
TILE_IN_LAT = (512, LAT_COLS)
TILE_IN_GATE = (1024, 1024)
TILE_OUT = (1024, 1024, D_MODEL)
TILE_FF1 = (1024, 1024)
TILE_FF2 = (1024, 1024, 4096)

F32 = jnp.float32
BF16 = jnp.bfloat16


def _params(*sem):
    return pltpu.CompilerParams(dimension_semantics=sem, vmem_limit_bytes=VMEM_LIMIT)


def _ada_kernel(c_ref, w_ref, b_ref, o_ref, cond_ref):
    c = c_ref[...]
    cond = (c * jax.nn.sigmoid(c)).astype(BF16)
    cond_ref[...] = cond
    o_ref[...] = jnp.dot(cond, w_ref[...].astype(BF16), preferred_element_type=F32) + b_ref[...]


def _ada(c8, w_ada, b_ada, n, tn=1024):
    return pl.pallas_call(
        _ada_kernel,
        out_shape=(jax.ShapeDtypeStruct((8, n), F32), jax.ShapeDtypeStruct((8, D_MODEL), BF16)),
        grid=(n // tn,),
        in_specs=[
            pl.BlockSpec((8, D_MODEL), lambda j: (0, 0)),
            pl.BlockSpec((D_MODEL, tn), lambda j: (0, j)),
            pl.BlockSpec((1, tn), lambda j: (0, j)),
        ],
        out_specs=(pl.BlockSpec((8, tn), lambda j: (0, j)), pl.BlockSpec((8, D_MODEL), lambda j: (0, 0))),
        compiler_params=_params("arbitrary"),
        name="ada",
    )(c8, w_ada, b_ada)


def _normmod_kernel(x_ref, g_ref, sh_ref, sc_ref, *refs):
    o_ref = refs[-1] if len(refs) == 1 else refs[2]
    x = x_ref[...]
    y = x * lax.rsqrt(jnp.mean(x * x, axis=-1, keepdims=True) + EPS)
    y = y * g_ref[...]
    o_ref[...] = (y * (1.0 + sc_ref[0, 0]) + sh_ref[0, 0]).astype(o_ref.dtype)
    if len(refs) > 1:
        pos_ref, freq_ref, _, c_ref, s_ref = refs
        _rope_tab_kernel(pos_ref, freq_ref, c_ref, s_ref)


def _mod_spec(idx, tm, width, ncol_arg=False):
    per_batch = SEQ // tm
    if ncol_arg:
        return pl.BlockSpec((1, 1, 1, width), lambda m, n, *_: (m // per_batch, idx, 0, n))
    return pl.BlockSpec((1, 1, 1, width), lambda m, *_: (m // per_batch, idx, 0, 0))


def _normmod(x2, g, mod4, shift_idx, scale_idx, tm=512, rope_in=None):
    t, d = x2.shape
    rows = lambda width: pl.BlockSpec((tm, width), lambda m: (m, 0))
    in_specs = [rows(d), pl.BlockSpec((1, d), lambda m: (0, 0)),
                _mod_spec(shift_idx, tm, d), _mod_spec(scale_idx, tm, d)]
    out_shape, out_specs, extra = [jax.ShapeDtypeStruct((t, d), BF16)], [rows(d)], ()
    if rope_in is not None:
        extra = rope_in
        in_specs += [rows(1), pl.BlockSpec((1, LANES), lambda m: (0, 0))]
        out_shape += [jax.ShapeDtypeStruct((t, LANES), F32)] * 2
        out_specs += [rows(LANES)] * 2
    out = pl.pallas_call(
        _normmod_kernel,
        out_shape=out_shape,
        grid=(t // tm,),
        in_specs=in_specs,
        out_specs=out_specs,
        compiler_params=_params("parallel"),
        name="normmod",
    )(x2, g, mod4, mod4, *extra)
    return out if rope_in is not None else out[0]


def _gelu(v):
    return 0.5 * v * (1.0 + lax.erf(v * math.sqrt(0.5)))


def _relu2(v):
    r = jnp.maximum(v, 0.0)
    return r * r


def _sigmoid(v):
    return 0.5 + 0.5 * jnp.tanh(0.5 * v)


_ACTS = {"gelu": _gelu, "sigmoid": _sigmoid, "relu2": _relu2, "none": lambda v: v}


def _cast_specs(jobs, n_steps, step_of):
    in_specs, out_specs, shapes, operands = [], [], [], []

    def rows_job(a):
        rows, cols = a.shape
        assert rows % (n_steps * 16) == 0
        spec = pl.BlockSpec((rows // n_steps, cols), lambda *ids: (step_of(*ids), 0))
        return a, spec, spec, a.shape

    def transposed_job(w_t, start, width):
        d, cols = w_t.shape[1], width // n_steps
        assert width % (n_steps * LANES) == 0
        src = pl.BlockSpec(
            (pl.Element(cols), pl.Element(d)),
            lambda *ids: (pl.multiple_of(start + step_of(*ids) * cols, math.gcd(start, cols)), 0))
        return w_t, src, pl.BlockSpec((d, cols), lambda *ids: (0, step_of(*ids))), (d, width)

    def widen_heads_job(w_uq):
        rows = w_uq.shape[0] // n_steps
        assert w_uq.shape[0] % (n_steps * 16) == 0
        index = lambda *ids: (step_of(*ids), 0)
        wide = MLA_HEADS * QK_PAD
        return (w_uq, pl.BlockSpec((rows, w_uq.shape[1]), index), pl.BlockSpec((rows, wide), index),
                (w_uq.shape[0], wide))

    for job in jobs:
        if not isinstance(job, tuple):
            a, src, dst, shape = rows_job(job)
        elif isinstance(job[0], str):
            a, src, dst, shape = widen_heads_job(job[1])
        else:
            a, src, dst, shape = transposed_job(*job)
        operands.append(a)
        in_specs.append(src)
        out_specs.append(dst)
        shapes.append(jax.ShapeDtypeStruct(shape, BF16))
    return in_specs, out_specs, shapes, operands


def _widen_heads(w, o_ref):
    for h in range(MLA_HEADS):
        src, dst = h * QK_HEAD, h * QK_PAD
        o_ref[:, dst:dst + QK_HEAD] = w[:, src:src + QK_HEAD]
        o_ref[:, dst + QK_HEAD:dst + QK_PAD] = w[:, src + QK_NOPE:src + QK_HEAD]


def _do_casts(refs, n):
    for src, dst in zip(refs[:n], refs[len(refs) - n:]):
        blk = src[...]
        if src.shape == dst.shape:
            dst[...] = blk.astype(BF16)
        elif src.shape[0] == dst.shape[0]:
            _widen_heads(blk.astype(BF16), dst)
        else:
            dst[...] = blk.T.astype(BF16)


def _mm_act_kernel(x_ref, w_ref, *refs, act, n_cast):
    o_ref = refs[n_cast]
    acc = jnp.dot(x_ref[...], w_ref[...], preferred_element_type=F32)
    o_ref[...] = _ACTS[act](acc).astype(o_ref.dtype)
    _do_casts(refs, n_cast)


def _mm_act(x, w, act, out_dtype, tm, tn, name, casts=()):
    t, k = x.shape
    n = w.shape[1]
    gm, gn = t // tm, n // tn
    cast_in, cast_out, cast_shapes, cast_ops = _cast_specs(casts, gm * gn, lambda m, j: m * gn + j)
    out = pl.pallas_call(
        functools.partial(_mm_act_kernel, act=act, n_cast=len(casts)),
        out_shape=[jax.ShapeDtypeStruct((t, n), out_dtype)] + cast_shapes,
        grid=(gm, gn),
        in_specs=[pl.BlockSpec((tm, k), lambda m, j: (m, 0)),
                  pl.BlockSpec((k, tn), lambda m, j: (0, j))] + cast_in,
        out_specs=[pl.BlockSpec((tm, tn), lambda m, j: (m, j))] + cast_out,
        compiler_params=_params("parallel", "parallel"),
        name=name,
    )(x, w, *cast_ops)
    return out if casts else out[0]


def _gmlp_kernel(u_ref, v_ref, gv_ref, ws_ref, bias_ref, o_ref, *, n_chunks):
    row = lax.broadcasted_iota(jnp.int32, (CHUNK, CHUNK), 0)
    col = lax.broadcasted_iota(jnp.int32, (CHUNK, CHUNK), 1)
    causal = col <= row
    gv = gv_ref[...]
    for ci in range(n_chunks):
        rows = pl.ds(ci * CHUNK, CHUNK)
        v = v_ref[rows, :].astype(F32)
        vn = v * lax.rsqrt(jnp.mean(v * v, axis=-1, keepdims=True) + EPS) * gv
        vn = vn.astype(BF16)
        for g in range(GM_GROUPS):
            cols = slice(g * GM_GROUP_W, (g + 1) * GM_GROUP_W)
            w = jnp.where(causal, ws_ref[g], 0.0).astype(BF16)
            mixed = jnp.dot(w, vn[:, cols], preferred_element_type=F32) + bias_ref[:, cols]
            o_ref[rows, cols] = (u_ref[rows, cols].astype(F32) * mixed).astype(o_ref.dtype)


def _gmlp(z, g_v, w_s, bias_full, tm=1024):
    t = z.shape[0]
    return pl.pallas_call(
        functools.partial(_gmlp_kernel, n_chunks=tm // CHUNK),
        out_shape=jax.ShapeDtypeStruct((t, GM_WIDTH), BF16),
        grid=(t // tm,),
        in_specs=[
            pl.BlockSpec((tm, GM_WIDTH), lambda m: (m, 0)),
            pl.BlockSpec((tm, GM_WIDTH), lambda m: (m, 1)),
            pl.BlockSpec((1, GM_WIDTH), lambda m: (0, 0)),
            pl.BlockSpec((GM_GROUPS, CHUNK, CHUNK), lambda m: (0, 0, 0)),
            pl.BlockSpec((CHUNK, GM_WIDTH), lambda m: (0, 0)),
        ],
        out_specs=pl.BlockSpec((tm, GM_WIDTH), lambda m: (m, 0)),
        compiler_params=_params("parallel"),
        name="gmlp",
    )(z, z, g_v, w_s, bias_full)


def _rope_tab_kernel(pos_ref, freq_ref, c_ref, s_ref):
    ang = pos_ref[...].astype(F32) * freq_ref[...]
    lane = lax.broadcasted_iota(jnp.int32, ang.shape, 1)
    cos = jnp.cos(ang)
    sin = jnp.sin(ang)
    c_ref[...] = jnp.where(lane < QK_ROPE, cos, 0.0)
    s_ref[...] = jnp.where(lane < QK_ROPE // 2, -sin, jnp.where(lane < QK_ROPE, sin, 0.0))


def _rope(v, c, s):
    return v * c + pltpu.roll(v, QK_ROPE // 2, 1) * s


def _rms_rows(x, g):
    return x * lax.rsqrt(jnp.mean(x * x, axis=-1, keepdims=True) + EPS) * g


def _qprep_kernel(lat_ref, glat_ref, w_ref, g1_ref, g2_ref, c_ref, s_ref, o_ref):
    xn = _rms_rows(lat_ref[...], glat_ref[...]).astype(BF16)
    c, s = c_ref[...], s_ref[...]
    g1, g2 = g1_ref[...], g2_ref[...]
    scale = QK_HEAD ** -0.5 * math.log2(math.e)
    for h in range(MLA_HEADS):
        acc = jnp.dot(xn, w_ref[:, h * QK_PAD:(h + 1) * QK_PAD], preferred_element_type=F32)
        a = acc[:, :LANES]
        b = acc[:, LANES:]
        ssq = jnp.sum(a * a + 0.5 * (b * b), axis=-1, keepdims=True)
        r = lax.rsqrt(ssq * (1.0 / QK_HEAD) + EPS) * scale
        o_ref[:, h * QK_PAD:h * QK_PAD + LANES] = (a * r * g1).astype(o_ref.dtype)
        o_ref[:, h * QK_PAD + LANES:(h + 1) * QK_PAD] = _rope(b * r * g2, c, s).astype(o_ref.dtype)


def _qprep(lat, g_q_lat, wq, g1, g2, c, s, tm=512):
    t = lat.shape[0]
    tn = MLA_HEADS * QK_PAD
    row = lambda m: (m, 0)
    const = lambda m: (0, 0)
    return pl.pallas_call(
        _qprep_kernel,
        out_shape=jax.ShapeDtypeStruct((t, tn), BF16),
        grid=(t // tm,),
        in_specs=[
            pl.BlockSpec((tm, Q_LORA), row),
            pl.BlockSpec((1, Q_LORA), const),
            pl.BlockSpec((Q_LORA, tn), const, pipeline_mode=pl.Buffered(1)),
            pl.BlockSpec((1, LANES), const),
            pl.BlockSpec((1, LANES), const),
            pl.BlockSpec((tm, LANES), row),
            pl.BlockSpec((tm, LANES), row),
        ],
        out_specs=pl.BlockSpec((tm, tn), row),
        compiler_params=_params("parallel"),
        name="q_prep",
    )(lat, g_q_lat, wq, g1, g2, c, s)


def _kvprep_kernel(lat_ref, kpe_ref, glat_ref, w_ref, g1_ref, g2_ref, c_ref, s_ref,
                   k_ref, v_ref):
    xn = _rms_rows(lat_ref[...], glat_ref[...]).astype(BF16)
    kpe = kpe_ref[...]
    ssq_pe = 0.5 * jnp.sum(kpe * kpe, axis=-1, keepdims=True)
    kr = _rope(kpe * g2_ref[...], c_ref[...], s_ref[...])
    g1 = g1_ref[...]
    width = QK_NOPE + V_HEAD
    for h in range(MLA_HEADS):
        acc = jnp.dot(xn, w_ref[:, h * width:(h + 1) * width], preferred_element_type=F32)
        a = acc[:, :QK_NOPE]
        ssq = jnp.sum(a * a, axis=-1, keepdims=True) + ssq_pe
        r = lax.rsqrt(ssq * (1.0 / QK_HEAD) + EPS)
        k_ref[:, h * QK_PAD:h * QK_PAD + LANES] = (a * r * g1).astype(k_ref.dtype)
        k_ref[:, h * QK_PAD + LANES:(h + 1) * QK_PAD] = (kr * r).astype(k_ref.dtype)
        v_ref[:, h * V_HEAD:(h + 1) * V_HEAD] = acc[:, QK_NOPE:].astype(v_ref.dtype)


def _kvprep(lat, g_kv_lat, wkv, g1, g2, c, s, tm=512):
    t = lat.shape[0]
    row = lambda m: (m, 0)
    const = lambda m: (0, 0)
    return pl.pallas_call(
        _kvprep_kernel,
        out_shape=(jax.ShapeDtypeStruct((t, MLA_HEADS * QK_PAD), BF16),
                   jax.ShapeDtypeStruct((t, MLA_HEADS * V_HEAD), BF16)),
        grid=(t // tm,),
        in_specs=[
            pl.BlockSpec((tm, KV_LORA), lambda m: (m, Q_LORA // KV_LORA)),
            pl.BlockSpec((tm, LANES), lambda m: (m, (Q_LORA + KV_LORA) // LANES)),
            pl.BlockSpec((1, KV_LORA), const),
            pl.BlockSpec(wkv.shape, const, pipeline_mode=pl.Buffered(1)),
            pl.BlockSpec((1, LANES), const),
            pl.BlockSpec((1, LANES), const),
            pl.BlockSpec((tm, LANES), row),
            pl.BlockSpec((tm, LANES), row),
        ],
        out_specs=(pl.BlockSpec((tm, MLA_HEADS * QK_PAD), row),
                   pl.BlockSpec((tm, MLA_HEADS * V_HEAD), row)),
        compiler_params=_params("parallel"),
        name="kv_prep",
    )(lat, lat, g_kv_lat, wkv, g1, g2, c, s)


def _attn_kernel(q_ref, k_ref, v_ref, *refs, tq, tk, heads, n_cast, ada):
    n_in = n_cast + (3 if ada else 0)
    o_ref, m_ref, acc_ref = refs[n_in], refs[-2], refs[-1]
    _do_casts(refs[:n_cast] + refs[n_in + 1:n_in + 1 + n_cast], n_cast)
    if ada:
        cond_ref, wada_ref, bada_ref = refs[n_cast:n_in]
        mod_ref = refs[n_in + 1 + n_cast]
        mod_ref[...] = jnp.dot(cond_ref[...], wada_ref[...].astype(BF16),
                               preferred_element_type=F32) + bada_ref[...]
    i = pl.program_id(2)
    nc = tk // LANES
    ratio = tq // tk
    m_ref[...] = jnp.full(m_ref.shape, -jnp.inf, F32)
    acc_ref[...] = jnp.zeros(acc_ref.shape, F32)

    def block(j, r0, masked):
        keys = pl.ds(pl.multiple_of(j * tk, tk), tk)
        nr = tq - r0
        for h in range(heads):
            q = q_ref[r0:, h * QK_PAD:(h + 1) * QK_PAD]
            k = k_ref[keys, h * QK_PAD:(h + 1) * QK_PAD]
            s = lax.dot_general(q, k, (((1,), (1,)), ((), ())), preferred_element_type=F32)
            if masked:
                row = lax.broadcasted_iota(jnp.int32, (nr, tk), 0)
                col = lax.broadcasted_iota(jnp.int32, (nr, tk), 1)
                s = jnp.where(col <= row, s, -jnp.inf)
            chunks = [s[:, c * LANES:(c + 1) * LANES] for c in range(nc)]
            m_part = functools.reduce(jnp.maximum, chunks)
            m_prev = m_ref[h, r0:, :]
            m_cur = jnp.broadcast_to(jnp.max(m_part, axis=1, keepdims=True), m_part.shape)
            m_new = jnp.maximum(m_prev, m_cur)
            alpha = jnp.exp2(m_prev - m_new)
            m_ref[h, r0:, :] = m_new
            p = jnp.concatenate([jnp.exp2((ch - m_new).astype(BF16)) for ch in chunks], axis=1)
            v = v_ref[keys, h * V_HEAD:(h + 1) * V_HEAD]
            v1 = jnp.concatenate([v, jnp.ones_like(v)], axis=1)
            alpha2 = jnp.concatenate([alpha, alpha], axis=1)
            acc_ref[h, r0:, :] = alpha2 * acc_ref[h, r0:, :] + jnp.dot(p, v1, preferred_element_type=F32)

    def body(j, carry):
        block(j, 0, False)
        return carry

    if tq < SEQ:
        lax.fori_loop(0, ratio * i, body, 0)
    for d in range(ratio):
        block(ratio * i + d, d * tk, True)
    for h in range(heads):
        acc = acc_ref[h]
        o_ref[:, h * V_HEAD:(h + 1) * V_HEAD] = (acc[:, :V_HEAD] / acc[:, V_HEAD:]).astype(o_ref.dtype)


def _attention(q, k, v, batch, tq=2048, tk=256, heads=2, casts=(), ada=None):
    t = q.shape[0]
    nq = SEQ // tq
    gh = MLA_HEADS // heads
    n_steps = batch * gh * nq
    step_of = lambda b, h, i: (b * gh + h) * nq + i
    cast_in, cast_out, cast_shapes, cast_ops = _cast_specs(casts, n_steps, step_of)
    ada_in, ada_out, ada_shapes, ada_ops = [], [], [], []
    if ada is not None:
        cond, w_ada, b_ada, first_col = ada
        width = (w_ada.shape[1] - first_col) // n_steps
        assert (w_ada.shape[1] - first_col) % (n_steps * LANES) == 0 and first_col % width == 0
        src = lambda b, h, i: (0, first_col // width + step_of(b, h, i))
        ada_in = [pl.BlockSpec(cond.shape, lambda b, h, i: (0, 0)),
                  pl.BlockSpec((w_ada.shape[0], width), src), pl.BlockSpec((1, width), src)]
        ada_out = [pl.BlockSpec((cond.shape[0], width), lambda b, h, i: (0, step_of(b, h, i)))]
        ada_shapes = [jax.ShapeDtypeStruct((cond.shape[0], w_ada.shape[1] - first_col), F32)]
        ada_ops = [cond, w_ada, b_ada]
    out = pl.pallas_call(
        functools.partial(_attn_kernel, tq=tq, tk=tk, heads=heads, n_cast=len(casts), ada=ada is not None),
        out_shape=[jax.ShapeDtypeStruct((t, MLA_HEADS * V_HEAD), BF16)] + cast_shapes + ada_shapes,
        grid=(batch, gh, nq),
        in_specs=[
            pl.BlockSpec((tq, heads * QK_PAD), lambda b, h, i: (b * nq + i, h)),
            pl.BlockSpec((SEQ, heads * QK_PAD), lambda b, h, i: (b, h)),
            pl.BlockSpec((SEQ, heads * V_HEAD), lambda b, h, i: (b, h)),
        ] + cast_in + ada_in,
        out_specs=[pl.BlockSpec((tq, heads * V_HEAD), lambda b, h, i: (b * nq + i, h))] + cast_out + ada_out,
        scratch_shapes=[pltpu.VMEM((heads, tq, LANES), F32), pltpu.VMEM((heads, tq, 2 * V_HEAD), F32)],
        compiler_params=_params("parallel", "parallel", "arbitrary"),
        name="attention",
    )(q, k, v, *cast_ops, *ada_ops)
    return out if (casts or ada is not None) else out[0]


def _branch_kernel(a_ref, wa_ref, b_ref, wb_ref, ga_ref, gb_ref, o_ref):
    ya = jnp.dot(a_ref[...], wa_ref[...], preferred_element_type=F32)
    yb = jnp.dot(b_ref[...], wb_ref[...], preferred_element_type=F32)
    o_ref[...] = (ga_ref[...].astype(F32) * ya + gb_ref[...].astype(F32) * yb).astype(o_ref.dtype)


def _branch(a, wa, b, wb, gates, tm=1024, tn=512):
    t = a.shape[0]
    nb = D_MODEL // tn
    return pl.pallas_call(
        _branch_kernel,
        out_shape=jax.ShapeDtypeStruct((t, D_MODEL), BF16),
        grid=(t // tm, nb),
        in_specs=[
            pl.BlockSpec((tm, a.shape[1]), lambda m, j: (m, 0)),
            pl.BlockSpec((wa.shape[0], tn), lambda m, j: (0, j)),
            pl.BlockSpec((tm, b.shape[1]), lambda m, j: (m, 0)),
            pl.BlockSpec((wb.shape[0], tn), lambda m, j: (0, j)),
            pl.BlockSpec((tm, tn), lambda m, j: (m, j)),
            pl.BlockSpec((tm, tn), lambda m, j: (m, nb + j)),
        ],
        out_specs=pl.BlockSpec((tm, tn), lambda m, j: (m, j)),
        compiler_params=_params("parallel", "parallel"),
        name="branch_merge",
    )(a, wa, b, wb, gates, gates)


def _mm_resid_kernel(x_ref, w_ref, res_ref, gate_ref, *refs, nk, n_cast):
    o_ref = refs[n_cast]
    _do_casts(refs, n_cast)
    kk = pl.program_id(2)
    if nk == 1:
        part = jnp.dot(x_ref[...], w_ref[...], preferred_element_type=F32)
        o_ref[...] = res_ref[...] + gate_ref[0, 0] * part
        return

    @pl.when(kk == 0)
    def _():
        o_ref[...] = jnp.dot(x_ref[...], w_ref[...], preferred_element_type=F32)

    @pl.when(kk > 0)
    def _():
        o_ref[...] += jnp.dot(x_ref[...], w_ref[...], preferred_element_type=F32)

    @pl.when(kk == nk - 1)
    def _():
        o_ref[...] = res_ref[...] + gate_ref[0, 0] * o_ref[...]


def _mm_resid(x, w, res, mod3, gate_idx, tm, tn, tk, name, casts=()):
    t, k = x.shape
    n = w.shape[1]
    gm, gn, nk = t // tm, n // tn, k // tk
    cast_in, cast_out, cast_shapes, cast_ops = _cast_specs(
        casts, gm * gn * nk, lambda m, j, kk: (m * gn + j) * nk + kk)
    out = pl.pallas_call(
        functools.partial(_mm_resid_kernel, nk=nk, n_cast=len(casts)),
        out_shape=[jax.ShapeDtypeStruct((t, n), F32)] + cast_shapes,
        grid=(gm, gn, nk),
        in_specs=[
            pl.BlockSpec((tm, tk), lambda m, j, kk: (m, kk)),
            pl.BlockSpec((tk, tn), lambda m, j, kk: (kk, j)),
            pl.BlockSpec((tm, tn), lambda m, j, kk: (m, j)),
            _mod_spec(gate_idx, tm, tn, ncol_arg=True),
        ] + cast_in,
        out_specs=[pl.BlockSpec((tm, tn), lambda m, j, kk: (m, j))] + cast_out,
        compiler_params=_params("parallel", "parallel", "arbitrary"),
        name=name,
    )(x, w, res, mod3, *cast_ops)
    return out if casts else out[0]


def _repack_kernel(wt_ref, o_ref, *, dup_tail):
    blk = wt_ref[...]
    if dup_tail:
        last = pl.program_id(0) == pl.num_programs(0) - 1
        half = blk.shape[0] // 2
        blk = jnp.where(last, jnp.concatenate([blk[:half], blk[:half]], axis=0), blk)
    o_ref[...] = blk.T.astype(BF16)


def _repack_w_in(w_in_t, start, width, cols, name, dup_tail=False):
    d = w_in_t.shape[1]
    return pl.pallas_call(
        functools.partial(_repack_kernel, dup_tail=dup_tail),
        out_shape=jax.ShapeDtypeStruct((d, width), BF16),
        grid=(width // cols,),
        in_specs=[pl.BlockSpec((pl.Element(cols), pl.Element(d)),
                               lambda i: (pl.multiple_of(start + i * cols, math.gcd(start, cols)), 0))],
        out_specs=pl.BlockSpec((d, cols), lambda i: (0, i)),
        compiler_params=_params("parallel"),
        name=name,
    )(w_in_t)


def _pad_cols(a, width):
    return jnp.pad(a, ((0, 0), (0, width - a.shape[1])))


def _layer(x2, batch, c8, rope_in, tables, w_ada, b_ada, g_norm1, w_in, g_v, w_s, b_s, g_q_lat, g_kv_lat,
           w_uq, w_ukv, g_qn, g_kn, w_branch_a, w_branch_b, w_out, g_norm2, w_ff1, w_ff2):
    d = x2.shape[1]
    b_ada = b_ada.reshape(1, -1)
    mod_a, cond = _ada(c8, w_ada, b_ada, 2 * d)
    mod_a = mod_a[:batch].reshape(batch, 2, 1, d)
    w_in_t = w_in.T
    w_lat = _repack_w_in(w_in_t, OFF_Q, LAT_COLS, LANES, "repack_lat", dup_tail=True)
    bias_full = jnp.repeat(b_s.T, GM_GROUP_W, axis=1)
    row = lambda v: v.reshape(1, -1)
    twice = lambda v: jnp.concatenate([v, v]).reshape(1, -1)
    qg1, qg2 = row(g_qn[:QK_NOPE]), twice(g_qn[QK_NOPE:])
    kg1, kg2 = row(g_kn[:QK_NOPE]), twice(g_kn[QK_NOPE:])

    if tables is None:
        h1, *tables = _normmod(x2, row(g_norm1), mod_a, 0, 1, rope_in=rope_in)
    else:
        h1 = _normmod(x2, row(g_norm1), mod_a, 0, 1)
    c_tab, s_tab = tables
    lat, wkv, w_uv = _mm_act(h1, w_lat, "none", F32, *TILE_IN_LAT, "in_proj_lat",
                             casts=(w_ukv, (w_in_t, 0, OFF_Q)))
    z, w_gate, wq = _mm_act(h1, w_uv, "gelu", BF16, *TILE_IN_UV, "in_proj_uv",
                            casts=((w_in_t, OFF_GATE, w_in.shape[1] - OFF_GATE), ("widen_heads", w_uq)))
    gates, w1 = _mm_act(h1, w_gate, "sigmoid", BF16, *TILE_IN_GATE, "in_proj_gate", casts=(w_ff1,))
    gm = _gmlp(z, row(g_v), w_s, bias_full)
    q = _qprep(lat, row(g_q_lat), wq, qg1, qg2, c_tab, s_tab)
    k, v = _kvprep(lat, row(g_kv_lat), wkv, kg1, kg2, c_tab, s_tab)
    att, wa, wb, wo, mod_b = _attention(q, k, v, batch, casts=(w_branch_a, w_branch_b, w_out),
                                        ada=(cond, w_ada, b_ada, 2 * d))
    mod_b = mod_b[:batch].reshape(batch, N_MOD - 2, 1, d)
    mixed = _branch(gm, wa, att, wb, gates)
    x2 = _mm_resid(mixed, wo, x2, mod_b, 0, *TILE_OUT, "out_proj")

    h2 = _normmod(x2, row(g_norm2), mod_b, 1, 2)
    hid, w2 = _mm_act(h2, w1, "relu2", BF16, *TILE_FF1, "ff1", casts=(w_ff2,))
    return _mm_resid(hid, w2, x2, mod_b, 3, *TILE_FF2, "ff2"), tables


def kernel(x, c, positions, w_ada, b_ada, g_norm1, w_in, g_v, w_s, b_s, g_q_lat, g_kv_lat, w_uq, w_ukv,
           g_qn, g_kn, w_branch_a, w_branch_b, w_out, g_norm2, w_ff1, w_ff2):
    batch, seq, d = x.shape
    assert (seq, d) == (SEQ, D_MODEL)
    depth = w_ada.shape[0]
    t = batch * seq
    x2 = x.reshape(t, d)

    inv_freq = 1.0 / (ROPE_THETA ** (jnp.arange(0, QK_ROPE, 2, dtype=F32) / QK_ROPE))
    freq_row = _pad_cols(jnp.concatenate([inv_freq, inv_freq]).reshape(1, QK_ROPE), LANES)
    rope_in = (positions.reshape(t, 1), freq_row)

    c8 = jnp.pad(c, ((0, 8 - batch), (0, 0)))
    tables = None
    for l in range(depth):
        x2, tables = _layer(x2, batch, c8, rope_in, tables, w_ada[l], b_ada[l], g_norm1[l], w_in[l], g_v[l],
                            w_s[l], b_s[l], g_q_lat[l], g_kv_lat[l], w_uq[l], w_ukv[l], g_qn[l], g_kn[l],
                            w_branch_a[l], w_branch_b[l], w_out[l], g_norm2[l], w_ff1[l], w_ff2[l])
    return x2.reshape(batch, seq, d)
```

```python
import functools
import math

import jax
import jax.numpy as jnp
from jax import lax
from jax.experimental import pallas as pl
from jax.experimental.pallas import tpu as pltpu

D_MODEL = 4096
SEQ = 2048
GM_WIDTH = 2048
GM_GROUPS = 8
GM_GROUP_W = GM_WIDTH // GM_GROUPS
CHUNK = 128
MLA_HEADS = 32
QK_NOPE = 128
QK_ROPE = 64
QK_HEAD = QK_NOPE + QK_ROPE
V_HEAD = 128
Q_LORA = 1024
KV_LORA = 512
ROPE_THETA = 10000.0
N_MOD = 6
EPS = 1e-6
OFF_Q = 2 * GM_WIDTH
OFF_KPE = OFF_Q + Q_LORA + KV_LORA
OFF_GATE = OFF_KPE + QK_ROPE

LANES = 128
QK_PAD = 2 * LANES
LAT_COLS = Q_LORA + KV_LORA + LANES
VMEM_LIMIT = 58 * 1024 * 1024

TILE_IN_UV = (1024, 512)
TILE_IN_LAT = (512, LAT_COLS)
TILE_IN_GATE = (1024, 1024)
TILE_OUT = (1024, 1024, D_MODEL)
TILE_FF1 = (1024, 1024)
TILE_FF2 = (1024, 1024, 4096)

F32 = jnp.float32
BF16 = jnp.bfloat16


def _params(*sem):
    return pltpu.CompilerParams(dimension_semantics=sem, vmem_limit_bytes=VMEM_LIMIT)


def _ada_kernel(c_ref, w_ref, b_ref, o_ref, cond_ref):
    c = c_ref[...]
    cond = c * jax.nn.sigmoid(c)
    cond_ref[...] = cond
    o_ref[...] = jnp.dot(cond.astype(BF16), w_ref[...].astype(BF16), preferred_element_type=F32) + b_ref[...]


def _ada(c8, w_ada, b_ada, n, tn=1024):
    return pl.pallas_call(
        _ada_kernel,
        out_shape=(jax.ShapeDtypeStruct((8, n), F32), jax.ShapeDtypeStruct((8, D_MODEL), F32)),
        grid=(n // tn,),
        in_specs=[
            pl.BlockSpec((8, D_MODEL), lambda j: (0, 0)),
            pl.BlockSpec((D_MODEL, tn), lambda j: (0, j)),
            pl.BlockSpec((1, tn), lambda j: (0, j)),
        ],
        out_specs=(pl.BlockSpec((8, tn), lambda j: (0, j)), pl.BlockSpec((8, D_MODEL), lambda j: (0, 0))),
        compiler_params=_params("arbitrary"),
        name="ada",
    )(c8, w_ada, b_ada)


def _normmod_kernel(x_ref, g_ref, sh_ref, sc_ref, *refs):
    o_ref = refs[-1] if len(refs) == 1 else refs[2]
    x = x_ref[...]
    y = x * lax.rsqrt(jnp.mean(x * x, axis=-1, keepdims=True) + EPS)
    y = y * g_ref[...]
    o_ref[...] = (y * (1.0 + sc_ref[0, 0]) + sh_ref[0, 0]).astype(o_ref.dtype)
    if len(refs) > 1:
        pos_ref, freq_ref, _, c_ref, s_ref = refs
        _rope_tab_kernel(pos_ref, freq_ref, c_ref, s_ref)


def _mod_spec(idx, tm, width, ncol_arg=False):
    per_batch = SEQ // tm
    if ncol_arg:
        return pl.BlockSpec((1, 1, 1, width), lambda m, n, *_: (m // per_batch, idx, 0, n))
    return pl.BlockSpec((1, 1, 1, width), lambda m, *_: (m // per_batch, idx, 0, 0))


def _normmod(x2, g, mod4, shift_idx, scale_idx, tm=512, rope_in=None):
    t, d = x2.shape
    rows = lambda width: pl.BlockSpec((tm, width), lambda m: (m, 0))
    in_specs = [rows(d), pl.BlockSpec((1, d), lambda m: (0, 0)),
                _mod_spec(shift_idx, tm, d), _mod_spec(scale_idx, tm, d)]
    out_shape, out_specs, extra = [jax.ShapeDtypeStruct((t, d), BF16)], [rows(d)], ()
    if rope_in is not None:
        extra = rope_in
        in_specs += [rows(1), pl.BlockSpec((1, LANES), lambda m: (0, 0))]
        out_shape += [jax.ShapeDtypeStruct((t, LANES), F32)] * 2
        out_specs += [rows(LANES)] * 2
    out = pl.pallas_call(
        _normmod_kernel,
        out_shape=out_shape,
        grid=(t // tm,),
        in_specs=in_specs,
        out_specs=out_specs,
        compiler_params=_params("parallel"),
        name="normmod",
    )(x2, g, mod4, mod4, *extra)
    return out if rope_in is not None else out[0]


def _gelu(v):
    return 0.5 * v * (1.0 + lax.erf(v * math.sqrt(0.5)))


def _relu2(v):
    r = jnp.maximum(v, 0.0)
    return r * r


def _sigmoid(v):
    return 0.5 + 0.5 * jnp.tanh(0.5 * v)


_ACTS = {"gelu": _gelu, "sigmoid": _sigmoid, "relu2": _relu2, "none": lambda v: v}


def _cast_specs(jobs, n_steps, step_of):
    in_specs, out_specs, shapes, operands = [], [], [], []

    def rows_job(a):
        rows, cols = a.shape
        assert rows % (n_steps * 16) == 0
        spec = pl.BlockSpec((rows // n_steps, cols), lambda *ids: (step_of(*ids), 0))
        return a, spec, spec, a.shape

    def transposed_job(w_t, start, width):
        d, cols = w_t.shape[1], width // n_steps
        assert width % (n_steps * LANES) == 0
        src = pl.BlockSpec(
            (pl.Element(cols), pl.Element(d)),
            lambda *ids: (pl.multiple_of(start + step_of(*ids) * cols, math.gcd(start, cols)), 0))
        return w_t, src, pl.BlockSpec((d, cols), lambda *ids: (0, step_of(*ids))), (d, width)

    def widen_heads_job(w_uq):
        rows = w_uq.shape[0] // n_steps
        assert w_uq.shape[0] % (n_steps * 16) == 0
        index = lambda *ids: (step_of(*ids), 0)
        wide = MLA_HEADS * QK_PAD
        return (w_uq, pl.BlockSpec((rows, w_uq.shape[1]), index), pl.BlockSpec((rows, wide), index),
                (w_uq.shape[0], wide))

    for job in jobs:
        if not isinstance(job, tuple):
            a, src, dst, shape = rows_job(job)
        elif isinstance(job[0], str):
            a, src, dst, shape = widen_heads_job(job[1])
        else:
            a, src, dst, shape = transposed_job(*job)
        operands.append(a)
        in_specs.append(src)
        out_specs.append(dst)
        shapes.append(jax.ShapeDtypeStruct(shape, BF16))
    return in_specs, out_specs, shapes, operands


def _widen_heads(w, o_ref):
    for h in range(MLA_HEADS):
        src, dst = h * QK_HEAD, h * QK_PAD
        o_ref[:, dst:dst + QK_HEAD] = w[:, src:src + QK_HEAD]
        o_ref[:, dst + QK_HEAD:dst + QK_PAD] = w[:, src + QK_NOPE:src + QK_HEAD]


def _do_casts(refs, n):
    for src, dst in zip(refs[:n], refs[len(refs) - n:]):
        blk = src[...]
        if src.shape == dst.shape:
            dst[...] = blk.astype(BF16)
        elif src.shape[0] == dst.shape[0]:
            _widen_heads(blk.astype(BF16), dst)
        else:
            dst[...] = blk.T.astype(BF16)


def _mm_act_kernel(x_ref, w_ref, *refs, act, n_cast, n_ada):
    n_in = n_cast + (3 if n_ada else 0)
    o_ref = refs[n_in]
    acc = jnp.dot(x_ref[...], w_ref[...], preferred_element_type=F32)
    o_ref[...] = _ACTS[act](acc).astype(o_ref.dtype)
    _do_casts(refs[:n_cast] + refs[n_in + 1:n_in + 1 + n_cast], n_cast)
    if n_ada:
        cond_ref, wada_ref, bada_ref = refs[n_cast:n_in]
        mod_ref = refs[n_in + 1 + n_cast]
        w, cond_t = wada_ref[...], cond_ref[...]
        rows = [jnp.sum(w * cond_t[:, b:b + 1], axis=0, keepdims=True) for b in range(n_ada)]
        rows += [jnp.zeros_like(rows[0])] * (mod_ref.shape[0] - n_ada)
        mod_ref[...] = jnp.concatenate(rows, axis=0) + bada_ref[...]


def _mm_act(x, w, act, out_dtype, tm, tn, name, casts=(), ada=None):
    t, k = x.shape
    n = w.shape[1]
    gm, gn = t // tm, n // tn
    step_of = lambda m, j: m * gn + j
    cast_in, cast_out, cast_shapes, cast_ops = _cast_specs(casts, gm * gn, step_of)
    ada_in, ada_out, ada_shapes, ada_ops, n_ada = [], [], [], [], 0
    if ada is not None:
        cond_t, w_ada, b_ada, first_col, n_ada = ada
        width = (w_ada.shape[1] - first_col) // (gm * gn)
        assert (w_ada.shape[1] - first_col) % (gm * gn * LANES) == 0 and first_col % width == 0
        src = lambda m, j: (0, first_col // width + step_of(m, j))
        ada_in = [pl.BlockSpec(cond_t.shape, lambda m, j: (0, 0)),
                  pl.BlockSpec((w_ada.shape[0], width), src), pl.BlockSpec((1, width), src)]
        ada_out = [pl.BlockSpec((8, width), lambda m, j: (0, step_of(m, j)))]
        ada_shapes = [jax.ShapeDtypeStruct((8, w_ada.shape[1] - first_col), F32)]
        ada_ops = [cond_t, w_ada, b_ada]
    out = pl.pallas_call(
        functools.partial(_mm_act_kernel, act=act, n_cast=len(casts), n_ada=n_ada),
        out_shape=[jax.ShapeDtypeStruct((t, n), out_dtype)] + cast_shapes + ada_shapes,
        grid=(gm, gn),
        in_specs=[pl.BlockSpec((tm, k), lambda m, j: (m, 0)),
                  pl.BlockSpec((k, tn), lambda m, j: (0, j))] + cast_in + ada_in,
        out_specs=[pl.BlockSpec((tm, tn), lambda m, j: (m, j))] + cast_out + ada_out,
        compiler_params=_params("parallel", "parallel"),
        name=name,
    )(x, w, *cast_ops, *ada_ops)
    return out if (casts or ada is not None) else out[0]


def _gmlp_kernel(u_ref, v_ref, gv_ref, ws_ref, bias_ref, o_ref, *, n_chunks):
    row = lax.broadcasted_iota(jnp.int32, (CHUNK, CHUNK), 0)
    col = lax.broadcasted_iota(jnp.int32, (CHUNK, CHUNK), 1)
    causal = col <= row
    gv = gv_ref[...]
    for ci in range(n_chunks):
        rows = pl.ds(ci * CHUNK, CHUNK)
        v = v_ref[rows, :].astype(F32)
        vn = v * lax.rsqrt(jnp.mean(v * v, axis=-1, keepdims=True) + EPS) * gv
        vn = vn.astype(BF16)
        for g in range(GM_GROUPS):
            cols = slice(g * GM_GROUP_W, (g + 1) * GM_GROUP_W)
            w = jnp.where(causal, ws_ref[g], 0.0).astype(BF16)
            mixed = jnp.dot(w, vn[:, cols], preferred_element_type=F32) + bias_ref[:, cols]
            o_ref[rows, cols] = (u_ref[rows, cols].astype(F32) * mixed).astype(o_ref.dtype)


def _gmlp(z, g_v, w_s, bias_full, tm=1024):
    t = z.shape[0]
    return pl.pallas_call(
        functools.partial(_gmlp_kernel, n_chunks=tm // CHUNK),
        out_shape=jax.ShapeDtypeStruct((t, GM_WIDTH), BF16),
        grid=(t // tm,),
        in_specs=[
            pl.BlockSpec((tm, GM_WIDTH), lambda m: (m, 0)),
            pl.BlockSpec((tm, GM_WIDTH), lambda m: (m, 1)),
            pl.BlockSpec((1, GM_WIDTH), lambda m: (0, 0)),
            pl.BlockSpec((GM_GROUPS, CHUNK, CHUNK), lambda m: (0, 0, 0)),
            pl.BlockSpec((CHUNK, GM_WIDTH), lambda m: (0, 0)),
        ],
        out_specs=pl.BlockSpec((tm, GM_WIDTH), lambda m: (m, 0)),
        compiler_params=_params("parallel"),
        name="gmlp",
    )(z, z, g_v, w_s, bias_full)


def _rope_tab_kernel(pos_ref, freq_ref, c_ref, s_ref):
    ang = pos_ref[...].astype(F32) * freq_ref[...]
    lane = lax.broadcasted_iota(jnp.int32, ang.shape, 1)
    cos = jnp.cos(ang)
    sin = jnp.sin(ang)
    c_ref[...] = jnp.where(lane < QK_ROPE, cos, 0.0)
    s_ref[...] = jnp.where(lane < QK_ROPE // 2, -sin, jnp.where(lane < QK_ROPE, sin, 0.0))


def _rope(v, c, s):
    return v * c + pltpu.roll(v, QK_ROPE // 2, 1) * s


def _rms_rows(x, g):
    return x * lax.rsqrt(jnp.mean(x * x, axis=-1, keepdims=True) + EPS) * g


def _qprep_kernel(lat_ref, glat_ref, w_ref, g1_ref, g2_ref, c_ref, s_ref, o_ref):
    xn = _rms_rows(lat_ref[...], glat_ref[...]).astype(BF16)
    c, s = c_ref[...], s_ref[...]
    g1, g2 = g1_ref[...], g2_ref[...]
    scale = QK_HEAD ** -0.5 * math.log2(math.e)
    for h in range(MLA_HEADS):
        acc = jnp.dot(xn, w_ref[:, h * QK_PAD:(h + 1) * QK_PAD], preferred_element_type=F32)
        a = acc[:, :LANES]
        b = acc[:, LANES:]
        ssq = jnp.sum(a * a + 0.5 * (b * b), axis=-1, keepdims=True)
        r = lax.rsqrt(ssq * (1.0 / QK_HEAD) + EPS) * scale
        o_ref[:, h * QK_PAD:h * QK_PAD + LANES] = (a * r * g1).astype(o_ref.dtype)
        o_ref[:, h * QK_PAD + LANES:(h + 1) * QK_PAD] = _rope(b * r * g2, c, s).astype(o_ref.dtype)


def _qprep(lat, g_q_lat, wq, g1, g2, c, s, tm=512):
    t = lat.shape[0]
    tn = MLA_HEADS * QK_PAD
    row = lambda m: (m, 0)
    const = lambda m: (0, 0)
    return pl.pallas_call(
        _qprep_kernel,
        out_shape=jax.ShapeDtypeStruct((t, tn), BF16),
        grid=(t // tm,),
        in_specs=[
            pl.BlockSpec((tm, Q_LORA), row),
            pl.BlockSpec((1, Q_LORA), const),
            pl.BlockSpec((Q_LORA, tn), const, pipeline_mode=pl.Buffered(1)),
            pl.BlockSpec((1, LANES), const),
            pl.BlockSpec((1, LANES), const),
            pl.BlockSpec((tm, LANES), row),
            pl.BlockSpec((tm, LANES), row),
        ],
        out_specs=pl.BlockSpec((tm, tn), row),
        compiler_params=_params("parallel"),
        name="q_prep",
    )(lat, g_q_lat, wq, g1, g2, c, s)


def _kvprep_kernel(lat_ref, kpe_ref, glat_ref, w_ref, g1_ref, g2_ref, c_ref, s_ref,
                   k_ref, v_ref):
    xn = _rms_rows(lat_ref[...], glat_ref[...]).astype(BF16)
    kpe = kpe_ref[...]
    ssq_pe = 0.5 * jnp.sum(kpe * kpe, axis=-1, keepdims=True)
    kr = _rope(kpe * g2_ref[...], c_ref[...], s_ref[...])
    g1 = g1_ref[...]
    width = QK_NOPE + V_HEAD
    for h in range(MLA_HEADS):
        acc = jnp.dot(xn, w_ref[:, h * width:(h + 1) * width], preferred_element_type=F32)
        a = acc[:, :QK_NOPE]
        ssq = jnp.sum(a * a, axis=-1, keepdims=True) + ssq_pe
        r = lax.rsqrt(ssq * (1.0 / QK_HEAD) + EPS)
        k_ref[:, h * QK_PAD:h * QK_PAD + LANES] = (a * r * g1).astype(k_ref.dtype)
        k_ref[:, h * QK_PAD + LANES:(h + 1) * QK_PAD] = (kr * r).astype(k_ref.dtype)
        v_ref[:, h * V_HEAD:(h + 1) * V_HEAD] = acc[:, QK_NOPE:].astype(v_ref.dtype)


def _kvprep(lat, g_kv_lat, wkv, g1, g2, c, s, tm=512):
    t = lat.shape[0]
    row = lambda m: (m, 0)
    const = lambda m: (0, 0)
    return pl.pallas_call(
        _kvprep_kernel,
        out_shape=(jax.ShapeDtypeStruct((t, MLA_HEADS * QK_PAD), BF16),
                   jax.ShapeDtypeStruct((t, MLA_HEADS * V_HEAD), BF16)),
        grid=(t // tm,),
        in_specs=[
            pl.BlockSpec((tm, KV_LORA), lambda m: (m, Q_LORA // KV_LORA)),
            pl.BlockSpec((tm, LANES), lambda m: (m, (Q_LORA + KV_LORA) // LANES)),
            pl.BlockSpec((1, KV_LORA), const),
            pl.BlockSpec(wkv.shape, const, pipeline_mode=pl.Buffered(1)),
            pl.BlockSpec((1, LANES), const),
            pl.BlockSpec((1, LANES), const),
            pl.BlockSpec((tm, LANES), row),
            pl.BlockSpec((tm, LANES), row),
        ],
        out_specs=(pl.BlockSpec((tm, MLA_HEADS * QK_PAD), row),
                   pl.BlockSpec((tm, MLA_HEADS * V_HEAD), row)),
        compiler_params=_params("parallel"),
        name="kv_prep",
    )(lat, lat, g_kv_lat, wkv, g1, g2, c, s)


def _attn_kernel(q_ref, k_ref, v_ref, *refs, tq, tk, heads, n_cast, ada):
    n_in = n_cast + (3 if ada else 0)
    o_ref, m_ref, acc_ref = refs[n_in], refs[-2], refs[-1]
    _do_casts(refs[:n_cast] + refs[n_in + 1:n_in + 1 + n_cast], n_cast)
    if ada:
        cond_ref, wada_ref, bada_ref = refs[n_cast:n_in]
        mod_ref = refs[n_in + 1 + n_cast]
        mod_ref[...] = jnp.dot(cond_ref[...], wada_ref[...].astype(BF16),
                               preferred_element_type=F32) + bada_ref[...]
    i = pl.program_id(2)
    nc = tk // LANES
    ratio = tq // tk
    m_ref[...] = jnp.full(m_ref.shape, -jnp.inf, F32)
    acc_ref[...] = jnp.zeros(acc_ref.shape, F32)

    def block(j, r0, masked):
        keys = pl.ds(pl.multiple_of(j * tk, tk), tk)
        nr = tq - r0
        for h in range(heads):
            q = q_ref[r0:, h * QK_PAD:(h + 1) * QK_PAD]
            k = k_ref[keys, h * QK_PAD:(h + 1) * QK_PAD]
            s = lax.dot_general(q, k, (((1,), (1,)), ((), ())), preferred_element_type=F32)
            if masked:
                row = lax.broadcasted_iota(jnp.int32, (nr, tk), 0)
                col = lax.broadcasted_iota(jnp.int32, (nr, tk), 1)
                s = jnp.where(col <= row, s, -jnp.inf)
            chunks = [s[:, c * LANES:(c + 1) * LANES] for c in range(nc)]
            m_part = functools.reduce(jnp.maximum, chunks)
            m_prev = m_ref[h, r0:, :]
            m_cur = jnp.broadcast_to(jnp.max(m_part, axis=1, keepdims=True), m_part.shape)
            m_new = jnp.maximum(m_prev, m_cur)
            alpha = jnp.exp2(m_prev - m_new)
            m_ref[h, r0:, :] = m_new
            p = jnp.concatenate([jnp.exp2((ch - m_new).astype(BF16)) for ch in chunks], axis=1)
            v = v_ref[keys, h * V_HEAD:(h + 1) * V_HEAD]
            v1 = jnp.concatenate([v, jnp.ones_like(v)], axis=1)
            alpha2 = jnp.concatenate([alpha, alpha], axis=1)
            acc_ref[h, r0:, :] = alpha2 * acc_ref[h, r0:, :] + jnp.dot(p, v1, preferred_element_type=F32)

    def body(j, carry):
        block(j, 0, False)
        return carry

    if tq < SEQ:
        lax.fori_loop(0, ratio * i, body, 0)
    for d in range(ratio):
        block(ratio * i + d, d * tk, True)
    for h in range(heads):
        acc = acc_ref[h]
        o_ref[:, h * V_HEAD:(h + 1) * V_HEAD] = (acc[:, :V_HEAD] / acc[:, V_HEAD:]).astype(o_ref.dtype)


def _attention(q, k, v, batch, tq=2048, tk=256, heads=2, casts=(), ada=None):
    t = q.shape[0]
    nq = SEQ // tq
    gh = MLA_HEADS // heads
    n_steps = batch * gh * nq
    step_of = lambda b, h, i: (b * gh + h) * nq + i
    cast_in, cast_out, cast_shapes, cast_ops = _cast_specs(casts, n_steps, step_of)
    ada_in, ada_out, ada_shapes, ada_ops = [], [], [], []
    if ada is not None:
        cond, w_ada, b_ada, first_col = ada
        width = (w_ada.shape[1] - first_col) // n_steps
        assert (w_ada.shape[1] - first_col) % (n_steps * LANES) == 0 and first_col % width == 0
        src = lambda b, h, i: (0, first_col // width + step_of(b, h, i))
        ada_in = [pl.BlockSpec(cond.shape, lambda b, h, i: (0, 0)),
                  pl.BlockSpec((w_ada.shape[0], width), src), pl.BlockSpec((1, width), src)]
        ada_out = [pl.BlockSpec((cond.shape[0], width), lambda b, h, i: (0, step_of(b, h, i)))]
        ada_shapes = [jax.ShapeDtypeStruct((cond.shape[0], w_ada.shape[1] - first_col), F32)]
        ada_ops = [cond, w_ada, b_ada]
    out = pl.pallas_call(
        functools.partial(_attn_kernel, tq=tq, tk=tk, heads=heads, n_cast=len(casts), ada=ada is not None),
        out_shape=[jax.ShapeDtypeStruct((t, MLA_HEADS * V_HEAD), BF16)] + cast_shapes + ada_shapes,
        grid=(batch, gh, nq),
        in_specs=[
            pl.BlockSpec((tq, heads * QK_PAD), lambda b, h, i: (b * nq + i, h)),
            pl.BlockSpec((SEQ, heads * QK_PAD), lambda b, h, i: (b, h)),
            pl.BlockSpec((SEQ, heads * V_HEAD), lambda b, h, i: (b, h)),
        ] + cast_in + ada_in,
        out_specs=[pl.BlockSpec((tq, heads * V_HEAD), lambda b, h, i: (b * nq + i, h))] + cast_out + ada_out,
        scratch_shapes=[pltpu.VMEM((heads, tq, LANES), F32), pltpu.VMEM((heads, tq, 2 * V_HEAD), F32)],
        compiler_params=_params("parallel", "parallel", "arbitrary"),
        name="attention",
    )(q, k, v, *cast_ops, *ada_ops)
    return out if (casts or ada is not None) else out[0]


def _branch_kernel(a_ref, wa_ref, b_ref, wb_ref, ga_ref, gb_ref, o_ref):
    ya = jnp.dot(a_ref[...], wa_ref[...], preferred_element_type=F32)
    yb = jnp.dot(b_ref[...], wb_ref[...], preferred_element_type=F32)
    o_ref[...] = (ga_ref[...].astype(F32) * ya + gb_ref[...].astype(F32) * yb).astype(o_ref.dtype)


def _branch(a, wa, b, wb, gates, tm=1024, tn=512):
    t = a.shape[0]
    nb = D_MODEL // tn
    return pl.pallas_call(
        _branch_kernel,
        out_shape=jax.ShapeDtypeStruct((t, D_MODEL), BF16),
        grid=(t // tm, nb),
        in_specs=[
            pl.BlockSpec((tm, a.shape[1]), lambda m, j: (m, 0)),
            pl.BlockSpec((wa.shape[0], tn), lambda m, j: (0, j)),
            pl.BlockSpec((tm, b.shape[1]), lambda m, j: (m, 0)),
            pl.BlockSpec((wb.shape[0], tn), lambda m, j: (0, j)),
            pl.BlockSpec((tm, tn), lambda m, j: (m, j)),
            pl.BlockSpec((tm, tn), lambda m, j: (m, nb + j)),
        ],
        out_specs=pl.BlockSpec((tm, tn), lambda m, j: (m, j)),
        compiler_params=_params("parallel", "parallel"),
        name="branch_merge",
    )(a, wa, b, wb, gates, gates)


def _mm_resid_kernel(x_ref, w_ref, res_ref, gate_ref, *refs, nk, n_cast):
    o_ref = refs[n_cast]
    _do_casts(refs, n_cast)
    kk = pl.program_id(2)
    if nk == 1:
        part = jnp.dot(x_ref[...], w_ref[...], preferred_element_type=F32)
        o_ref[...] = res_ref[...] + gate_ref[0, 0] * part
        return

    @pl.when(kk == 0)
    def _():
        o_ref[...] = jnp.dot(x_ref[...], w_ref[...], preferred_element_type=F32)

    @pl.when(kk > 0)
    def _():
        o_ref[...] += jnp.dot(x_ref[...], w_ref[...], preferred_element_type=F32)

    @pl.when(kk == nk - 1)
    def _():
        o_ref[...] = res_ref[...] + gate_ref[0, 0] * o_ref[...]


def _mm_resid(x, w, res, mod3, gate_idx, tm, tn, tk, name, casts=()):
    t, k = x.shape
    n = w.shape[1]
    gm, gn, nk = t // tm, n // tn, k // tk
    cast_in, cast_out, cast_shapes, cast_ops = _cast_specs(
        casts, gm * gn * nk, lambda m, j, kk: (m * gn + j) * nk + kk)
    out = pl.pallas_call(
        functools.partial(_mm_resid_kernel, nk=nk, n_cast=len(casts)),
        out_shape=[jax.ShapeDtypeStruct((t, n), F32)] + cast_shapes,
        grid=(gm, gn, nk),
        in_specs=[
            pl.BlockSpec((tm, tk), lambda m, j, kk: (m, kk)),
            pl.BlockSpec((tk, tn), lambda m, j, kk: (kk, j)),
            pl.BlockSpec((tm, tn), lambda m, j, kk: (m, j)),
            _mod_spec(gate_idx, tm, tn, ncol_arg=True),
        ] + cast_in,
        out_specs=[pl.BlockSpec((tm, tn), lambda m, j, kk: (m, j))] + cast_out,
        compiler_params=_params("parallel", "parallel", "arbitrary"),
        name=name,
    )(x, w, res, mod3, *cast_ops)
    return out if casts else out[0]


def _repack_kernel(wt_ref, o_ref, *, dup_tail):
    blk = wt_ref[...]
    if dup_tail:
        last = pl.program_id(0) == pl.num_programs(0) - 1
        half = blk.shape[0] // 2
        blk = jnp.where(last, jnp.concatenate([blk[:half], blk[:half]], axis=0), blk)
    o_ref[...] = blk.T.astype(BF16)


def _repack_w_in(w_in_t, start, width, cols, name, dup_tail=False):
    d = w_in_t.shape[1]
    return pl.pallas_call(
        functools.partial(_repack_kernel, dup_tail=dup_tail),
        out_shape=jax.ShapeDtypeStruct((d, width), BF16),
        grid=(width // cols,),
        in_specs=[pl.BlockSpec((pl.Element(cols), pl.Element(d)),
                               lambda i: (pl.multiple_of(start + i * cols, math.gcd(start, cols)), 0))],
        out_specs=pl.BlockSpec((d, cols), lambda i: (0, i)),
        compiler_params=_params("parallel"),
        name=name,
    )(w_in_t)


def _pad_cols(a, width):
    return jnp.pad(a, ((0, 0), (0, width - a.shape[1])))


def _layer(x2, batch, c8, rope_in, tables, w_ada, b_ada, g_norm1, w_in, g_v, w_s, b_s, g_q_lat, g_kv_lat,
           w_uq, w_ukv, g_qn, g_kn, w_branch_a, w_branch_b, w_out, g_norm2, w_ff1, w_ff2):
    d = x2.shape[1]
    b_ada = b_ada.reshape(1, -1)
    mod_a, cond = _ada(c8, w_ada, b_ada, 2 * d)
    mod_a = mod_a[:batch].reshape(batch, 2, 1, d)
    w_in_t = w_in.T
    w_lat = _repack_w_in(w_in_t, OFF_Q, LAT_COLS, LANES, "repack_lat", dup_tail=True)
    bias_full = jnp.repeat(b_s.T, GM_GROUP_W, axis=1)
    row = lambda v: v.reshape(1, -1)
    twice = lambda v: jnp.concatenate([v, v]).reshape(1, -1)
    qg1, qg2 = row(g_qn[:QK_NOPE]), twice(g_qn[QK_NOPE:])
    kg1, kg2 = row(g_kn[:QK_NOPE]), twice(g_kn[QK_NOPE:])

    if tables is None:
        h1, *tables = _normmod(x2, row(g_norm1), mod_a, 0, 1, rope_in=rope_in)
    else:
        h1 = _normmod(x2, row(g_norm1), mod_a, 0, 1)
    c_tab, s_tab = tables
    lat, wkv, w_uv = _mm_act(h1, w_lat, "none", F32, *TILE_IN_LAT, "in_proj_lat",
                             casts=(w_ukv, (w_in_t, 0, OFF_Q)))
    z, w_gate, wq, mod_b = _mm_act(
        h1, w_uv, "gelu", BF16, *TILE_IN_UV, "in_proj_uv",
        casts=((w_in_t, OFF_GATE, w_in.shape[1] - OFF_GATE), ("widen_heads", w_uq)),
        ada=(cond.T, w_ada, b_ada, 2 * d, batch))
    gates, w1 = _mm_act(h1, w_gate, "sigmoid", BF16, *TILE_IN_GATE, "in_proj_gate", casts=(w_ff1,))
    gm = _gmlp(z, row(g_v), w_s, bias_full)
    q = _qprep(lat, row(g_q_lat), wq, qg1, qg2, c_tab, s_tab)
    k, v = _kvprep(lat, row(g_kv_lat), wkv, kg1, kg2, c_tab, s_tab)
    att, wa, wb, wo = _attention(q, k, v, batch, casts=(w_branch_a, w_branch_b, w_out))
    mod_b = mod_b[:batch].reshape(batch, N_MOD - 2, 1, d)
    mixed = _branch(gm, wa, att, wb, gates)
    x2 = _mm_resid(mixed, wo, x2, mod_b, 0, *TILE_OUT, "out_proj")

    h2 = _normmod(x2, row(g_norm2), mod_b, 1, 2)
    hid, w2 = _mm_act(h2, w1, "relu2", BF16, *TILE_FF1, "ff1", casts=(w_ff2,))
    return _mm_resid(hid, w2, x2, mod_b, 3, *TILE_FF2, "ff2"), tables


def kernel(x, c, positions, w_ada, b_ada, g_norm1, w_in, g_v, w_s, b_s, g_q_lat, g_kv_lat, w_uq, w_ukv,
           g_qn, g_kn, w_branch_a, w_branch_b, w_out, g_norm2, w_ff1, w_ff2):
    batch, seq, d = x.shape
    assert (seq, d) == (SEQ, D_MODEL)
    depth = w_ada.shape[0]
    t = batch * seq
    x2 = x.reshape(t, d)

    inv_freq = 1.0 / (ROPE_THETA ** (jnp.arange(0, QK_ROPE, 2, dtype=F32) / QK_ROPE))
    freq_row = _pad_cols(jnp.concatenate([inv_freq, inv_freq]).reshape(1, QK_ROPE), LANES)
    rope_in = (positions.reshape(t, 1), freq_row)

    c8 = jnp.pad(c, ((0, 8 - batch), (0, 0)))
    tables = None
    for l in range(depth):
        x2, tables = _layer(x2, batch, c8, rope_in, tables, w_ada[l], b_ada[l], g_norm1[l], w_in[l], g_v[l],
                            w_s[l], b_s[l], g_q_lat[l], g_kv_lat[l], w_uq[l], w_ukv[l], g_qn[l], g_kn[l],
                            w_branch_a[l], w_branch_b[l], w_out[l], g_norm2[l], w_ff1[l], w_ff2[l])
    return x2.reshape(batch, seq, d)
```

```python
import functools
import math

import jax
import jax.numpy as jnp
from jax import lax
from jax.experimental import pallas as pl
from jax.experimental.pallas import tpu as pltpu

D_MODEL = 4096
SEQ = 2048
GM_WIDTH = 2048
GM_GROUPS = 8
GM_GROUP_W = GM_WIDTH // GM_GROUPS
CHUNK = 128
MLA_HEADS = 32
QK_NOPE = 128
QK_ROPE = 64
QK_HEAD = QK_NOPE + QK_ROPE
V_HEAD = 128
Q_LORA = 1024
KV_LORA = 512
ROPE_THETA = 10000.0
N_MOD = 6
EPS = 1e-6
OFF_Q = 2 * GM_WIDTH
OFF_KPE = OFF_Q + Q_LORA + KV_LORA
OFF_GATE = OFF_KPE + QK_ROPE

LANES = 128
QK_PAD = 2 * LANES
LAT_COLS = Q_LORA + KV_LORA + LANES
VMEM_LIMIT = 58 * 1024 * 1024

TILE_IN_UV = (1024, 512)
TILE_IN_LAT = (512, LAT_COLS)
TILE_IN_GATE = (1024, 1024)
TILE_OUT = (1024, 1024, D_MODEL)
TILE_FF1 = (1024, 1024)
TILE_FF2 = (1024, 1024, 4096)

F32 = jnp.float32
BF16 = jnp.bfloat16


def _params(*sem):
    return pltpu.CompilerParams(dimension_semantics=sem, vmem_limit_bytes=VMEM_LIMIT)


def _ada_kernel(c_ref, w_ref, b_ref, o_ref, cond_ref):
    c = c_ref[...]
    cond = (c * jax.nn.sigmoid(c)).astype(BF16)
    cond_ref[...] = cond
    o_ref[...] = jnp.dot(cond, w_ref[...].astype(BF16), preferred_element_type=F32) + b_ref[...]


def _ada(c8, w_ada, b_ada, n, tn=1024):
    return pl.pallas_call(
        _ada_kernel,
        out_shape=(jax.ShapeDtypeStruct((8, n), F32), jax.ShapeDtypeStruct((8, D_MODEL), BF16)),
        grid=(n // tn,),
        in_specs=[
            pl.BlockSpec((8, D_MODEL), lambda j: (0, 0)),
            pl.BlockSpec((D_MODEL, tn), lambda j: (0, j)),
            pl.BlockSpec((1, tn), lambda j: (0, j)),
        ],
        out_specs=(pl.BlockSpec((8, tn), lambda j: (0, j)), pl.BlockSpec((8, D_MODEL), lambda j: (0, 0))),
        compiler_params=_params("arbitrary"),
        name="ada",
    )(c8, w_ada, b_ada)


def _normmod_kernel(x_ref, g_ref, sh_ref, sc_ref, *refs):
    o_ref = refs[-1] if len(refs) == 1 else refs[2]
    x = x_ref[...]
    y = x * lax.rsqrt(jnp.mean(x * x, axis=-1, keepdims=True) + EPS)
    y = y * g_ref[...]
    o_ref[...] = (y * (1.0 + sc_ref[0, 0]) + sh_ref[0, 0]).astype(o_ref.dtype)
    if len(refs) > 1:
        pos_ref, freq_ref, _, c_ref, s_ref = refs
        _rope_tab_kernel(pos_ref, freq_ref, c_ref, s_ref)


def _mod_spec(idx, tm, width, ncol_arg=False):
    per_batch = SEQ // tm
    if ncol_arg:
        return pl.BlockSpec((1, 1, 1, width), lambda m, n, *_: (m // per_batch, idx, 0, n))
    return pl.BlockSpec((1, 1, 1, width), lambda m, *_: (m // per_batch, idx, 0, 0))


def _normmod(x2, g, mod4, shift_idx, scale_idx, tm=512, rope_in=None):
    t, d = x2.shape
    rows = lambda width: pl.BlockSpec((tm, width), lambda m: (m, 0))
    in_specs = [rows(d), pl.BlockSpec((1, d), lambda m: (0, 0)),
                _mod_spec(shift_idx, tm, d), _mod_spec(scale_idx, tm, d)]
    out_shape, out_specs, extra = [jax.ShapeDtypeStruct((t, d), BF16)], [rows(d)], ()
    if rope_in is not None:
        extra = rope_in
        in_specs += [rows(1), pl.BlockSpec((1, LANES), lambda m: (0, 0))]
        out_shape += [jax.ShapeDtypeStruct((t, LANES), F32)] * 2
        out_specs += [rows(LANES)] * 2
    out = pl.pallas_call(
        _normmod_kernel,
        out_shape=out_shape,
        grid=(t // tm,),
        in_specs=in_specs,
        out_specs=out_specs,
        compiler_params=_params("parallel"),
        name="normmod",
    )(x2, g, mod4, mod4, *extra)
    return out if rope_in is not None else out[0]


def _gelu(v):
    return 0.5 * v * (1.0 + lax.erf(v * math.sqrt(0.5)))


def _relu2(v):
    r = jnp.maximum(v, 0.0)
    return r * r


def _sigmoid(v):
    return 0.5 + 0.5 * jnp.tanh(0.5 * v)


_ACTS = {"gelu": _gelu, "sigmoid": _sigmoid, "relu2": _relu2, "none": lambda v: v}


def _cast_specs(jobs, n_steps, step_of):
    in_specs, out_specs, shapes, operands = [], [], [], []

    def rows_job(a):
        rows, cols = a.shape
        assert rows % (n_steps * 16) == 0
        spec = pl.BlockSpec((rows // n_steps, cols), lambda *ids: (step_of(*ids), 0))
        return a, spec, spec, a.shape

    def transposed_job(w_t, start, width):
        d, cols = w_t.shape[1], width // n_steps
        assert width % (n_steps * LANES) == 0
        src = pl.BlockSpec(
            (pl.Element(cols), pl.Element(d)),
            lambda *ids: (pl.multiple_of(start + step_of(*ids) * cols, math.gcd(start, cols)), 0))
        return w_t, src, pl.BlockSpec((d, cols), lambda *ids: (0, step_of(*ids))), (d, width)

    def widen_heads_job(w_uq):
        rows = w_uq.shape[0] // n_steps
        assert w_uq.shape[0] % (n_steps * 16) == 0
        index = lambda *ids: (step_of(*ids), 0)
        wide = MLA_HEADS * QK_PAD
        return (w_uq, pl.BlockSpec((rows, w_uq.shape[1]), index), pl.BlockSpec((rows, wide), index),
                (w_uq.shape[0], wide))

    for job in jobs:
        if not isinstance(job, tuple):
            a, src, dst, shape = rows_job(job)
        elif isinstance(job[0], str):
            a, src, dst, shape = widen_heads_job(job[1])
        else:
            a, src, dst, shape = transposed_job(*job)
        operands.append(a)
        in_specs.append(src)
        out_specs.append(dst)
        shapes.append(jax.ShapeDtypeStruct(shape, BF16))
    return in_specs, out_specs, shapes, operands


def _widen_heads(w, o_ref):
    for h in range(MLA_HEADS):
        src, dst = h * QK_HEAD, h * QK_PAD
        o_ref[:, dst:dst + QK_HEAD] = w[:, src:src + QK_HEAD]
        o_ref[:, dst + QK_HEAD:dst + QK_PAD] = w[:, src + QK_NOPE:src + QK_HEAD]


def _do_casts(refs, n):
    for src, dst in zip(refs[:n], refs[len(refs) - n:]):
        blk = src[...]
        if src.shape == dst.shape:
            dst[...] = blk.astype(BF16)
        elif src.shape[0] == dst.shape[0]:
            _widen_heads(blk.astype(BF16), dst)
        else:
            dst[...] = blk.T.astype(BF16)


def _mm_act_kernel(x_ref, w_ref, *refs, act, n_cast):
    o_ref = refs[n_cast]
    acc = jnp.dot(x_ref[...], w_ref[...], preferred_element_type=F32)
    o_ref[...] = _ACTS[act](acc).astype(o_ref.dtype)
    _do_casts(refs, n_cast)


def _mm_act(x, w, act, out_dtype, tm, tn, name, casts=()):
    t, k = x.shape
    n = w.shape[1]
    gm, gn = t // tm, n // tn
    cast_in, cast_out, cast_shapes, cast_ops = _cast_specs(casts, gm * gn, lambda m, j: m * gn + j)
    out = pl.pallas_call(
        functools.partial(_mm_act_kernel, act=act, n_cast=len(casts)),
        out_shape=[jax.ShapeDtypeStruct((t, n), out_dtype)] + cast_shapes,
        grid=(gm, gn),
        in_specs=[pl.BlockSpec((tm, k), lambda m, j: (m, 0)),
                  pl.BlockSpec((k, tn), lambda m, j: (0, j))] + cast_in,
        out_specs=[pl.BlockSpec((tm, tn), lambda m, j: (m, j))] + cast_out,
        compiler_params=_params("parallel", "parallel"),
        name=name,
    )(x, w, *cast_ops)
    return out if casts else out[0]


def _gmlp_kernel(u_ref, v_ref, gv_ref, ws_ref, bias_ref, o_ref, *, n_chunks):
    row = lax.broadcasted_iota(jnp.int32, (CHUNK, CHUNK), 0)
    col = lax.broadcasted_iota(jnp.int32, (CHUNK, CHUNK), 1)
    causal = col <= row
    gv = gv_ref[...]
    for ci in range(n_chunks):
        rows = pl.ds(ci * CHUNK, CHUNK)
        v = v_ref[rows, :].astype(F32)
        vn = v * lax.rsqrt(jnp.mean(v * v, axis=-1, keepdims=True) + EPS) * gv
        vn = vn.astype(BF16)
        for g in range(GM_GROUPS):
            cols = slice(g * GM_GROUP_W, (g + 1) * GM_GROUP_W)
            w = jnp.where(causal, ws_ref[g], 0.0).astype(BF16)
            mixed = jnp.dot(w, vn[:, cols], preferred_element_type=F32) + bias_ref[:, cols]
            o_ref[rows, cols] = (u_ref[rows, cols].astype(F32) * mixed).astype(o_ref.dtype)


def _gmlp(z, g_v, w_s, bias_full, tm=1024):
    t = z.shape[0]
    return pl.pallas_call(
        functools.partial(_gmlp_kernel, n_chunks=tm // CHUNK),
        out_shape=jax.ShapeDtypeStruct((t, GM_WIDTH), BF16),
        grid=(t // tm,),
        in_specs=[
            pl.BlockSpec((tm, GM_WIDTH), lambda m: (m, 0)),
            pl.BlockSpec((tm, GM_WIDTH), lambda m: (m, 1)),
            pl.BlockSpec((1, GM_WIDTH), lambda m: (0, 0)),
            pl.BlockSpec((GM_GROUPS, CHUNK, CHUNK), lambda m: (0, 0, 0)),
            pl.BlockSpec((CHUNK, GM_WIDTH), lambda m: (0, 0)),
        ],
        out_specs=pl.BlockSpec((tm, GM_WIDTH), lambda m: (m, 0)),
        compiler_params=_params("parallel"),
        name="gmlp",
    )(z, z, g_v, w_s, bias_full)


def _rope_tab_kernel(pos_ref, freq_ref, c_ref, s_ref):
    ang = pos_ref[...].astype(F32) * freq_ref[...]
    lane = lax.broadcasted_iota(jnp.int32, ang.shape, 1)
    cos = jnp.cos(ang)
    sin = jnp.sin(ang)
    c_ref[...] = jnp.where(lane < QK_ROPE, cos, 0.0)
    s_ref[...] = jnp.where(lane < QK_ROPE // 2, -sin, jnp.where(lane < QK_ROPE, sin, 0.0))


def _rope(v, c, s):
    return v * c + pltpu.roll(v, QK_ROPE // 2, 1) * s


def _rms_rows(x, g):
    return x * lax.rsqrt(jnp.mean(x * x, axis=-1, keepdims=True) + EPS) * g


def _qprep_kernel(lat_ref, glat_ref, w_ref, g1_ref, g2_ref, c_ref, s_ref, o_ref):
    xn = _rms_rows(lat_ref[...], glat_ref[...]).astype(BF16)
    c, s = c_ref[...], s_ref[...]
    g1, g2 = g1_ref[...], g2_ref[...]
    scale = QK_HEAD ** -0.5 * math.log2(math.e)
    for h in range(MLA_HEADS):
        acc = jnp.dot(xn, w_ref[:, h * QK_PAD:(h + 1) * QK_PAD], preferred_element_type=F32)
        a = acc[:, :LANES]
        b = acc[:, LANES:]
        ssq = jnp.sum(a * a + 0.5 * (b * b), axis=-1, keepdims=True)
        r = lax.rsqrt(ssq * (1.0 / QK_HEAD) + EPS) * scale
        o_ref[:, h * QK_PAD:h * QK_PAD + LANES] = (a * r * g1).astype(o_ref.dtype)
        o_ref[:, h * QK_PAD + LANES:(h + 1) * QK_PAD] = _rope(b * r * g2, c, s).astype(o_ref.dtype)


def _qprep(lat, g_q_lat, wq, g1, g2, c, s, tm=512):
    t = lat.shape[0]
    tn = MLA_HEADS * QK_PAD
    row = lambda m: (m, 0)
    const = lambda m: (0, 0)
    return pl.pallas_call(
        _qprep_kernel,
        out_shape=jax.ShapeDtypeStruct((t, tn), BF16),
        grid=(t // tm,),
        in_specs=[
            pl.BlockSpec((tm, Q_LORA), row),
            pl.BlockSpec((1, Q_LORA), const),
            pl.BlockSpec((Q_LORA, tn), const, pipeline_mode=pl.Buffered(1)),
            pl.BlockSpec((1, LANES), const),
            pl.BlockSpec((1, LANES), const),
            pl.BlockSpec((tm, LANES), row),
            pl.BlockSpec((tm, LANES), row),
        ],
        out_specs=pl.BlockSpec((tm, tn), row),
        compiler_params=_params("parallel"),
        name="q_prep",
    )(lat, g_q_lat, wq, g1, g2, c, s)


def _kvprep_kernel(lat_ref, kpe_ref, glat_ref, w_ref, g1_ref, g2_ref, c_ref, s_ref,
                   k_ref, v_ref):
    xn = _rms_rows(lat_ref[...], glat_ref[...]).astype(BF16)
    kpe = kpe_ref[...]
    ssq_pe = 0.5 * jnp.sum(kpe * kpe, axis=-1, keepdims=True)
    kr = _rope(kpe * g2_ref[...], c_ref[...], s_ref[...])
    g1 = g1_ref[...]
    width = QK_NOPE + V_HEAD
    for h in range(MLA_HEADS):
        acc = jnp.dot(xn, w_ref[:, h * width:(h + 1) * width], preferred_element_type=F32)
        a = acc[:, :QK_NOPE]
        ssq = jnp.sum(a * a, axis=-1, keepdims=True) + ssq_pe
        r = lax.rsqrt(ssq * (1.0 / QK_HEAD) + EPS)
        k_ref[:, h * QK_PAD:h * QK_PAD + LANES] = (a * r * g1).astype(k_ref.dtype)
        k_ref[:, h * QK_PAD + LANES:(h + 1) * QK_PAD] = (kr * r).astype(k_ref.dtype)
        v_ref[:, h * V_HEAD:(h + 1) * V_HEAD] = acc[:, QK_NOPE:].astype(v_ref.dtype)


def _kvprep(lat, g_kv_lat, wkv, g1, g2, c, s, tm=512):
    t = lat.shape[0]
    row = lambda m: (m, 0)
    const = lambda m: (0, 0)
    return pl.pallas_call(
        _kvprep_kernel,
        out_shape=(jax.ShapeDtypeStruct((t, MLA_HEADS * QK_PAD), BF16),
                   jax.ShapeDtypeStruct((t, MLA_HEADS * V_HEAD), BF16)),
        grid=(t // tm,),
        in_specs=[
            pl.BlockSpec((tm, KV_LORA), lambda m: (m, Q_LORA // KV_LORA)),
            pl.BlockSpec((tm, LANES), lambda m: (m, (Q_LORA + KV_LORA) // LANES)),
            pl.BlockSpec((1, KV_LORA), const),
            pl.BlockSpec(wkv.shape, const, pipeline_mode=pl.Buffered(1)),
            pl.BlockSpec((1, LANES), const),
            pl.BlockSpec((1, LANES), const),
            pl.BlockSpec((tm, LANES), row),
            pl.BlockSpec((tm, LANES), row),
        ],
        out_specs=(pl.BlockSpec((tm, MLA_HEADS * QK_PAD), row),
                   pl.BlockSpec((tm, MLA_HEADS * V_HEAD), row)),
        compiler_params=_params("parallel"),
        name="kv_prep",
    )(lat, lat, g_kv_lat, wkv, g1, g2, c, s)


def _attn_kernel(q_ref, k_ref, v_ref, *refs, tq, tk, heads, n_cast, ada):
    n_in = n_cast + (3 if ada else 0)
    o_ref, m_ref, acc_ref = refs[n_in], refs[-2], refs[-1]
    _do_casts(refs[:n_cast] + refs[n_in + 1:n_in + 1 + n_cast], n_cast)
    if ada:
        cond_ref, wada_ref, bada_ref = refs[n_cast:n_in]
        mod_ref = refs[n_in + 1 + n_cast]
        mod_ref[...] = jnp.dot(cond_ref[...], wada_ref[...].astype(BF16),
                               preferred_element_type=F32) + bada_ref[...]
    i = pl.program_id(2)
    nc = tk // LANES
    ratio = tq // tk
    whole_seq = tq == SEQ
    if not whole_seq:
        m_ref[...] = jnp.full(m_ref.shape, -jnp.inf, F32)
        acc_ref[...] = jnp.zeros(acc_ref.shape, F32)

    def block(j, r0, masked, first=False):
        keys = pl.ds(pl.multiple_of(j * tk, tk), tk)
        nr = tq - r0
        for h in range(heads):
            q = q_ref[r0:, h * QK_PAD:(h + 1) * QK_PAD]
            k = k_ref[keys, h * QK_PAD:(h + 1) * QK_PAD]
            s = lax.dot_general(q, k, (((1,), (1,)), ((), ())), preferred_element_type=F32)
            if masked:
                row = lax.broadcasted_iota(jnp.int32, (nr, tk), 0)
                col = lax.broadcasted_iota(jnp.int32, (nr, tk), 1)
                s = jnp.where(col <= row, s, -jnp.inf)
            chunks = [s[:, c * LANES:(c + 1) * LANES] for c in range(nc)]
            m_part = functools.reduce(jnp.maximum, chunks)
            m_cur = jnp.broadcast_to(jnp.max(m_part, axis=1, keepdims=True), m_part.shape)
            m_new = m_cur if first else jnp.maximum(m_ref[h, r0:, :], m_cur)
            p = jnp.concatenate([jnp.exp2((ch - m_new).astype(BF16)) for ch in chunks], axis=1)
            v = v_ref[keys, h * V_HEAD:(h + 1) * V_HEAD]
            v1 = jnp.concatenate([v, jnp.ones_like(v)], axis=1)
            pv = jnp.dot(p, v1, preferred_element_type=F32)
            if first:
                acc_ref[h, r0:, :] = pv
            else:
                alpha = jnp.exp2(m_ref[h, r0:, :] - m_new)
                acc_ref[h, r0:, :] = jnp.concatenate([alpha, alpha], axis=1) * acc_ref[h, r0:, :] + pv
            m_ref[h, r0:, :] = m_new

    def body(j, carry):
        block(j, 0, False)
        return carry

    if tq < SEQ:
        lax.fori_loop(0, ratio * i, body, 0)
    for d in range(ratio):
        block(ratio * i + d, d * tk, True, first=whole_seq and d == 0)
    for h in range(heads):
        acc = acc_ref[h]
        o_ref[:, h * V_HEAD:(h + 1) * V_HEAD] = (acc[:, :V_HEAD] / acc[:, V_HEAD:]).astype(o_ref.dtype)


def _attention(q, k, v, batch, tq=2048, tk=256, heads=2, casts=(), ada=None):
    t = q.shape[0]
    nq = SEQ // tq
    gh = MLA_HEADS // heads
    n_steps = batch * gh * nq
    step_of = lambda b, h, i: (b * gh + h) * nq + i
    cast_in, cast_out, cast_shapes, cast_ops = _cast_specs(casts, n_steps, step_of)
    ada_in, ada_out, ada_shapes, ada_ops = [], [], [], []
    if ada is not None:
        cond, w_ada, b_ada, first_col = ada
        width = (w_ada.shape[1] - first_col) // n_steps
        assert (w_ada.shape[1] - first_col) % (n_steps * LANES) == 0 and first_col % width == 0
        src = lambda b, h, i: (0, first_col // width + step_of(b, h, i))
        ada_in = [pl.BlockSpec(cond.shape, lambda b, h, i: (0, 0)),
                  pl.BlockSpec((w_ada.shape[0], width), src), pl.BlockSpec((1, width), src)]
        ada_out = [pl.BlockSpec((cond.shape[0], width), lambda b, h, i: (0, step_of(b, h, i)))]
        ada_shapes = [jax.ShapeDtypeStruct((cond.shape[0], w_ada.shape[1] - first_col), F32)]
        ada_ops = [cond, w_ada, b_ada]
    out = pl.pallas_call(
        functools.partial(_attn_kernel, tq=tq, tk=tk, heads=heads, n_cast=len(casts), ada=ada is not None),
        out_shape=[jax.ShapeDtypeStruct((t, MLA_HEADS * V_HEAD), BF16)] + cast_shapes + ada_shapes,
        grid=(batch, gh, nq),
        in_specs=[
            pl.BlockSpec((tq, heads * QK_PAD), lambda b, h, i: (b * nq + i, h)),
            pl.BlockSpec((SEQ, heads * QK_PAD), lambda b, h, i: (b, h)),
            pl.BlockSpec((SEQ, heads * V_HEAD), lambda b, h, i: (b, h)),
        ] + cast_in + ada_in,
        out_specs=[pl.BlockSpec((tq, heads * V_HEAD), lambda b, h, i: (b * nq + i, h))] + cast_out + ada_out,
        scratch_shapes=[pltpu.VMEM((heads, tq, LANES), F32), pltpu.VMEM((heads, tq, 2 * V_HEAD), F32)],
        compiler_params=_params("parallel", "parallel", "arbitrary"),
        name="attention",
    )(q, k, v, *cast_ops, *ada_ops)
    return out if (casts or ada is not None) else out[0]


def _branch_kernel(a_ref, wa_ref, b_ref, wb_ref, ga_ref, gb_ref, o_ref):
    ya = jnp.dot(a_ref[...], wa_ref[...], preferred_element_type=F32)
    yb = jnp.dot(b_ref[...], wb_ref[...], preferred_element_type=F32)
    o_ref[...] = (ga_ref[...].astype(F32) * ya + gb_ref[...].astype(F32) * yb).astype(o_ref.dtype)


def _branch(a, wa, b, wb, gates, tm=1024, tn=512):
    t = a.shape[0]
    nb = D_MODEL // tn
    return pl.pallas_call(
        _branch_kernel,
        out_shape=jax.ShapeDtypeStruct((t, D_MODEL), BF16),
        grid=(t // tm, nb),
        in_specs=[
            pl.BlockSpec((tm, a.shape[1]), lambda m, j: (m, 0)),
            pl.BlockSpec((wa.shape[0], tn), lambda m, j: (0, j)),
            pl.BlockSpec((tm, b.shape[1]), lambda m, j: (m, 0)),
            pl.BlockSpec((wb.shape[0], tn), lambda m, j: (0, j)),
            pl.BlockSpec((tm, tn), lambda m, j: (m, j)),
            pl.BlockSpec((tm, tn), lambda m, j: (m, nb + j)),
        ],
        out_specs=pl.BlockSpec((tm, tn), lambda m, j: (m, j)),
        compiler_params=_params("parallel", "parallel"),
        name="branch_merge",
    )(a, wa, b, wb, gates, gates)


def _mm_resid_kernel(x_ref, w_ref, res_ref, gate_ref, *refs, nk, n_cast):
    o_ref = refs[n_cast]
    _do_casts(refs, n_cast)
    kk = pl.program_id(2)
    if nk == 1:
        part = jnp.dot(x_ref[...], w_ref[...], preferred_element_type=F32)
        o_ref[...] = res_ref[...] + gate_ref[0, 0] * part
        return

    @pl.when(kk == 0)
    def _():
        o_ref[...] = jnp.dot(x_ref[...], w_ref[...], preferred_element_type=F32)

    @pl.when(kk > 0)
    def _():
        o_ref[...] += jnp.dot(x_ref[...], w_ref[...], preferred_element_type=F32)

    @pl.when(kk == nk - 1)
    def _():
        o_ref[...] = res_ref[...] + gate_ref[0, 0] * o_ref[...]


def _mm_resid(x, w, res, mod3, gate_idx, tm, tn, tk, name, casts=()):
    t, k = x.shape
    n = w.shape[1]
    gm, gn, nk = t // tm, n // tn, k // tk
    cast_in, cast_out, cast_shapes, cast_ops = _cast_specs(
        casts, gm * gn * nk, lambda m, j, kk: (m * gn + j) * nk + kk)
    out = pl.pallas_call(
        functools.partial(_mm_resid_kernel, nk=nk, n_cast=len(casts)),
        out_shape=[jax.ShapeDtypeStruct((t, n), F32)] + cast_shapes,
        grid=(gm, gn, nk),
        in_specs=[
            pl.BlockSpec((tm, tk), lambda m, j, kk: (m, kk)),
            pl.BlockSpec((tk, tn), lambda m, j, kk: (kk, j)),
            pl.BlockSpec((tm, tn), lambda m, j, kk: (m, j)),
            _mod_spec(gate_idx, tm, tn, ncol_arg=True),
        ] + cast_in,
        out_specs=[pl.BlockSpec((tm, tn), lambda m, j, kk: (m, j))] + cast_out,
        compiler_params=_params("parallel", "parallel", "arbitrary"),
        name=name,
    )(x, w, res, mod3, *cast_ops)
    return out if casts else out[0]


def _repack_kernel(wt_ref, o_ref, *, dup_tail):
    blk = wt_ref[...]
    if dup_tail:
        last = pl.program_id(0) == pl.num_programs(0) - 1
        half = blk.shape[0] // 2
        blk = jnp.where(last, jnp.concatenate([blk[:half], blk[:half]], axis=0), blk)
    o_ref[...] = blk.T.astype(BF16)


def _repack_w_in(w_in_t, start, width, cols, name, dup_tail=False):
    d = w_in_t.shape[1]
    return pl.pallas_call(
        functools.partial(_repack_kernel, dup_tail=dup_tail),
        out_shape=jax.ShapeDtypeStruct((d, width), BF16),
        grid=(width // cols,),
        in_specs=[pl.BlockSpec((pl.Element(cols), pl.Element(d)),
                               lambda i: (pl.multiple_of(start + i * cols, math.gcd(start, cols)), 0))],
        out_specs=pl.BlockSpec((d, cols), lambda i: (0, i)),
        compiler_params=_params("parallel"),
        name=name,
    )(w_in_t)


def _pad_cols(a, width):
    return jnp.pad(a, ((0, 0), (0, width - a.shape[1])))


def _layer(x2, batch, c8, rope_in, tables, w_ada, b_ada, g_norm1, w_in, g_v, w_s, b_s, g_q_lat, g_kv_lat,
           w_uq, w_ukv, g_qn, g_kn, w_branch_a, w_branch_b, w_out, g_norm2, w_ff1, w_ff2):
    d = x2.shape[1]
    b_ada = b_ada.reshape(1, -1)
    mod_a, cond = _ada(c8, w_ada, b_ada, 2 * d)
    mod_a = mod_a[:batch].reshape(batch, 2, 1, d)
    w_in_t = w_in.T
    w_lat = _repack_w_in(w_in_t, OFF_Q, LAT_COLS, LANES, "repack_lat", dup_tail=True)
    bias_full = jnp.repeat(b_s.T, GM_GROUP_W, axis=1)
    row = lambda v: v.reshape(1, -1)
    twice = lambda v: jnp.concatenate([v, v]).reshape(1, -1)
    qg1, qg2 = row(g_qn[:QK_NOPE]), twice(g_qn[QK_NOPE:])
    kg1, kg2 = row(g_kn[:QK_NOPE]), twice(g_kn[QK_NOPE:])

    if tables is None:
        h1, *tables = _normmod(x2, row(g_norm1), mod_a, 0, 1, rope_in=rope_in)
    else:
        h1 = _normmod(x2, row(g_norm1), mod_a, 0, 1)
    c_tab, s_tab = tables
    lat, wkv, w_uv = _mm_act(h1, w_lat, "none", F32, *TILE_IN_LAT, "in_proj_lat",
                             casts=(w_ukv, (w_in_t, 0, OFF_Q)))
    z, w_gate, wq = _mm_act(h1, w_uv, "gelu", BF16, *TILE_IN_UV, "in_proj_uv",
                            casts=((w_in_t, OFF_GATE, w_in.shape[1] - OFF_GATE), ("widen_heads", w_uq)))
    gates, w1 = _mm_act(h1, w_gate, "sigmoid", BF16, *TILE_IN_GATE, "in_proj_gate", casts=(w_ff1,))
    gm = _gmlp(z, row(g_v), w_s, bias_full)
    q = _qprep(lat, row(g_q_lat), wq, qg1, qg2, c_tab, s_tab)
    k, v = _kvprep(lat, row(g_kv_lat), wkv, kg1, kg2, c_tab, s_tab)
    att, wa, wb, wo, mod_b = _attention(q, k, v, batch, casts=(w_branch_a, w_branch_b, w_out),
                                        ada=(cond, w_ada, b_ada, 2 * d))
    mod_b = mod_b[:batch].reshape(batch, N_MOD - 2, 1, d)
    mixed = _branch(gm, wa, att, wb, gates)
    x2 = _mm_resid(mixed, wo, x2, mod_b, 0, *TILE_OUT, "out_proj")

    h2 = _normmod(x2, row(g_norm2), mod_b, 1, 2)
    hid, w2 = _mm_act(h2, w1, "relu2", BF16, *TILE_FF1, "ff1", casts=(w_ff2,))
    return _mm_resid(hid, w2, x2, mod_b, 3, *TILE_FF2, "ff2"), tables


def kernel(x, c, positions, w_ada, b_ada, g_norm1, w_in, g_v, w_s, b_s, g_q_lat, g_kv_lat, w_uq, w_ukv,
           g_qn, g_kn, w_branch_a, w_branch_b, w_out, g_norm2, w_ff1, w_ff2):
    batch, seq, d = x.shape
    assert (seq, d) == (SEQ, D_MODEL)
    depth = w_ada.shape[0]
    t = batch * seq
    x2 = x.reshape(t, d)

    inv_freq = 1.0 / (ROPE_THETA ** (jnp.arange(0, QK_ROPE, 2, dtype=F32) / QK_ROPE))
    freq_row = _pad_cols(jnp.concatenate([inv_freq, inv_freq]).reshape(1, QK_ROPE), LANES)
    rope_in = (positions.reshape(t, 1), freq_row)

    c8 = jnp.pad(c, ((0, 8 - batch), (0, 0)))
    tables = None
    for l in range(depth):
        x2, tables = _layer(x2, batch, c8, rope_in, tables, w_ada[l], b_ada[l], g_norm1[l], w_in[l], g_v[l],
                            w_s[l], b_s[l], g_q_lat[l], g_kv_lat[l], w_uq[l], w_ukv[l], g_qn[l], g_kn[l],
                            w_branch_a[l], w_branch_b[l], w_out[l], g_norm2[l], w_ff1[l], w_ff2[l])
    return x2.reshape(batch, seq, d)
```
